```python
import math
import jax, jax.numpy as jnp
from jax import lax
import numpy as np

D_MODEL = 4096
BATCH = 4
SEQ = 2048
DEPTH = 1
DEC_BATCH = 128
DEC_SEQ = 4
PAST_LEN = 2048
PAGE_SIZE = 128

NORM_EPS = 1e-6
ATT_HEAD_DIM = 128
ATT_KV_HEADS = 8
DILATED_PAIRS = ((128, 1), (512, 4), (2048, 16))
ATT_GROUPS = len(DILATED_PAIRS)
ATT_Q_HEADS = ATT_GROUPS * ATT_KV_HEADS
MAX_WINDOW = max(w for w, _ in DILATED_PAIRS)
ROPE_THETA = 10000.0
ATT_SCALE = ATT_HEAD_DIM ** -0.5
ATT_Q_W = ATT_Q_HEADS * ATT_HEAD_DIM
ATT_KV_W = ATT_KV_HEADS * ATT_HEAD_DIM
ATT_OUT_W = ATT_KV_W
RW_HEAD_DIM = 64
RW_HEADS = 32
RW_W = RW_HEADS * RW_HEAD_DIM
RW_LORA_W = 96
RW_LORA_A = 96
RW_LORA_G = 256
RW_FEAT = 3 * RW_W + RW_LORA_W + RW_LORA_A + RW_LORA_G
RW_GN_EPS = 64e-5
RW_DECAY_SCALE = math.exp(-0.5)
RW_SPLITS = [RW_W, 2 * RW_W, 3 * RW_W, 3 * RW_W + RW_LORA_W, 3 * RW_W + RW_LORA_W + RW_LORA_A]
IN_W = ATT_Q_W + 2 * ATT_KV_W + RW_FEAT + 2 * D_MODEL
IN_SPLITS = [ATT_Q_W, ATT_Q_W + ATT_KV_W, ATT_Q_W + 2 * ATT_KV_W,
             ATT_Q_W + 2 * ATT_KV_W + RW_FEAT, ATT_Q_W + 2 * ATT_KV_W + RW_FEAT + D_MODEL]
MOE_GROUPS = 8
MOE_PER_GROUP = 8
N_EXPERTS = MOE_GROUPS * MOE_PER_GROUP
TOP_K_IN_GROUP = 2
EXPERT_FF = 1024
MOE_BLOCK = 64

kernel_name = 'hybrid_dilated_rwkv7_hmoe_step'


def rmsnorm(x, g):
    xf = x.astype(jnp.float32)
    y = xf * lax.rsqrt(jnp.mean(xf * xf, axis=-1, keepdims=True) + NORM_EPS)
    return (y * g.astype(jnp.float32)).astype(x.dtype)


def rope(x, pos):
    half = ATT_HEAD_DIM // 2
    inv_freq = ROPE_THETA ** (-jnp.arange(half, dtype=jnp.float32) / half)
    ang = pos.astype(jnp.float32)[:, None] * inv_freq[None, :]
    cos = jnp.cos(ang)[:, None, :]
    sin = jnp.sin(ang)[:, None, :]
    xf = x.astype(jnp.float32)
    x1, x2 = xf[..., :half], xf[..., half:]
    return jnp.concatenate([x1 * cos - x2 * sin, x2 * cos + x1 * sin], axis=-1).astype(x.dtype)


def banded_causal_attention(q, k, v, back):
    n, L, H, hd = q.shape
    blk = back
    nb = -(-L // blk)
    pad = nb * blk - L

    def blocks(t):
        t = jnp.pad(t, ((0, 0), (0, pad), (0, 0), (0, 0)))
        return t.reshape(n, nb, blk, H, hd)

    def with_prev(t):
        prev = jnp.concatenate([jnp.zeros_like(t[:, :1]), t[:, :-1]], axis=1)
        return jnp.concatenate([prev, t], axis=2)

    qb = blocks(q)
    kk = with_prev(blocks(k))
    vv = with_prev(blocks(v))
    s = jnp.einsum('nbqhd,nbkhd->nbhqk', qb, kk, preferred_element_type=jnp.float32) * ATT_SCALE
    qi = jnp.arange(blk)[:, None]
    kj = jnp.arange(2 * blk)[None, :]
    dist = blk + qi - kj
    band = (dist >= 0) & (dist <= back)
    after_start = (jnp.arange(nb)[:, None, None] > 0) | (kj[None] >= blk)
    mask = band[None] & after_start
    s = jnp.where(mask[None, :, None], s, -jnp.inf)
    m = jnp.max(s, axis=-1, keepdims=True)
    e = jnp.exp(s - m)
    den = jnp.sum(e, axis=-1)
    o = jnp.einsum('nbhqk,nbkhd->nbqhd', e, vv.astype(jnp.float32))
    o = o / jnp.moveaxis(den, 2, 3)[..., None]
    lse = jnp.moveaxis(m[..., 0] + jnp.log(den), 2, 3)
    o = o.reshape(n, nb * blk, H, hd)[:, :L]
    lse = lse.reshape(n, nb * blk, H)[:, :L]
    return o, lse


def combine_groups(outs, lses):
    w = jax.nn.softmax(jnp.stack(lses, axis=0), axis=0)
    return jnp.einsum('gbth,gbthd->bthd', w, jnp.stack(outs, axis=0))


def dilated_attention_prompt(q, k, v):
    B, S = q.shape[0], q.shape[1]
    H, hd = ATT_KV_HEADS, ATT_HEAD_DIM
    outs, lses = [], []
    for g, (window, dil) in enumerate(DILATED_PAIRS):
        back = window // dil
        sub = S // dil

        def to_sub(t):
            return t.reshape(B, sub, dil, H, hd).transpose(0, 2, 1, 3, 4).reshape(B * dil, sub, H, hd)

        o, lse = banded_causal_attention(to_sub(q[:, :, g]), to_sub(k), to_sub(v), back)
        outs.append(o.reshape(B, dil, sub, H, hd).transpose(0, 2, 1, 3, 4).reshape(B, S, H, hd))
        lses.append(lse.reshape(B, dil, sub, H).transpose(0, 2, 1, 3).reshape(B, S, H))
    return combine_groups(outs, lses)


def dilated_attention_sample(q, k_all, v_all):
    n_new = q.shape[1]
    q_base = k_all.shape[1] - n_new
    outs, lses = [], []
    for g, (window, dil) in enumerate(DILATED_PAIRS):
        n_keys = window // dil + 1
        idx = q_base + jnp.arange(n_new)[:, None] - dil * jnp.arange(n_keys)[None, :]
        valid = idx >= 0
        idxc = jnp.maximum(idx, 0)
        kg = k_all[:, idxc]
        vg = v_all[:, idxc]
        s = jnp.einsum('bshd,bsjhd->bshj', q[:, :, g], kg, preferred_element_type=jnp.float32) * ATT_SCALE
        s = jnp.where(valid[None, :, None, :], s, -jnp.inf)
        m = jnp.max(s, axis=-1, keepdims=True)
        e = jnp.exp(s - m)
        den = jnp.sum(e, axis=-1)
        o = jnp.einsum('bshj,bsjhd->bshd', e, vg.astype(jnp.float32)) / den[..., None]
        outs.append(o)
        lses.append(m[..., 0] + jnp.log(den))
    return combine_groups(outs, lses)


def rwkv7_step(S, inp):
    r_t, w_t, k_t, v_t, kk_t, a_t = inp
    sa = jnp.einsum('bhvk,bhk->bhv', S, -kk_t)
    S = (S * w_t[:, :, None, :] + sa[..., None] * (kk_t * a_t)[:, :, None, :]
         + v_t[..., None] * k_t[:, :, None, :])
    return S, jnp.einsum('bhvk,bhk->bhv', S, r_t)


def rwkv7_time_mix(u, shift_prev, wkv_prev, mu, w0, w2, a0, a2, g2, k_k, k_a, r_k, ln_w, ln_b):
    B, T, _ = u.shape
    H, N = RW_HEADS, RW_HEAD_DIM
    uf = u.astype(jnp.float32)
    u_prev = jnp.concatenate([shift_prev.astype(jnp.float32)[:, None], uf[:, :-1]], axis=1)
    xm = uf + (u_prev - uf) * mu
    r, k, v, wl, al, gl = jnp.split(xm, RW_SPLITS, axis=-1)
    decay = jnp.exp(-RW_DECAY_SCALE * jax.nn.sigmoid(w0 + jnp.tanh(wl) @ w2))
    a = jax.nn.sigmoid(a0 + al @ a2)
    g = jax.nn.sigmoid(gl) @ g2
    kk = (k * k_k).reshape(B, T, H, N)
    kk = kk / jnp.maximum(jnp.sqrt(jnp.sum(kk * kk, axis=-1, keepdims=True)), 1e-12)
    k = k * (1.0 + (a - 1.0) * k_a)
    r, k, v, decay, a = (t.reshape(B, T, H, N) for t in (r, k, v, decay, a))
    xs = tuple(jnp.moveaxis(t, 1, 0) for t in (r, decay, k, v, kk, a))
    S_final, y = lax.scan(rwkv7_step, wkv_prev.astype(jnp.float32), xs)
    y = jnp.moveaxis(y, 0, 1)
    mean = jnp.mean(y, axis=-1, keepdims=True)
    var = jnp.mean(jnp.square(y - mean), axis=-1, keepdims=True)
    y = ((y - mean) * lax.rsqrt(var + RW_GN_EPS)).reshape(B, T, RW_W) * ln_w + ln_b
    bonus = jnp.sum(r * k * r_k, axis=-1, keepdims=True) * v
    y = (y + bonus.reshape(B, T, RW_W)) * g
    return y.astype(u.dtype), u[:, -1], S_final


def hierarchical_moe(h, wg_r, bg_r, we_r, be_r, w_gate, w_up, w_down):
    T, D = h.shape
    gl = (h @ wg_r).astype(jnp.float32) + bg_r.astype(jnp.float32)
    gp = jax.nn.softmax(gl, axis=-1)
    gi = jnp.argmax(gl, axis=-1).astype(jnp.int32)
    g_prob = jnp.take_along_axis(gp, gi[:, None], axis=1)
    el = ((h @ we_r).astype(jnp.float32) + be_r.astype(jnp.float32)).reshape(T, MOE_GROUPS, MOE_PER_GROUP)
    el = jnp.take_along_axis(el, gi[:, None, None], axis=1)[:, 0]
    top_w, top_i = lax.top_k(jax.nn.softmax(el, axis=-1), TOP_K_IN_GROUP)
    top_w = top_w / jnp.sum(top_w, axis=-1, keepdims=True) * g_prob
    expert_id = (gi[:, None] * MOE_PER_GROUP + top_i).astype(jnp.int32)
    A = T * TOP_K_IN_GROUP
    flat_e = expert_id.reshape(A)
    flat_tok = jnp.arange(A, dtype=jnp.int32) // TOP_K_IN_GROUP
    flat_w = top_w.reshape(A)
    order = jnp.argsort(flat_e)
    e_sorted, tok_sorted, w_sorted = flat_e[order], flat_tok[order], flat_w[order]
    counts = jnp.bincount(flat_e, length=N_EXPERTS)
    starts = jnp.cumsum(counts) - counts
    padded = (counts + MOE_BLOCK - 1) // MOE_BLOCK * MOE_BLOCK
    pad_ends = jnp.cumsum(padded)
    pad_starts = pad_ends - padded
    dest = pad_starts[e_sorted] + (jnp.arange(A, dtype=jnp.int32) - starts[e_sorted])
    n_blocks = -(-A // MOE_BLOCK) + N_EXPERTS
    slot_tok = jnp.full((n_blocks * MOE_BLOCK,), T, jnp.int32).at[dest].set(tok_sorted)
    h_pad = jnp.concatenate([h, jnp.zeros((1, D), h.dtype)], axis=0)
    xb = h_pad[slot_tok].reshape(n_blocks, MOE_BLOCK, D)
    block_start = jnp.arange(n_blocks, dtype=jnp.int32) * MOE_BLOCK
    block_expert = jnp.minimum(jnp.searchsorted(pad_ends, block_start, side='right'), N_EXPERTS - 1).astype(jnp.int32)

    def run_block(args):
        xblk, e = args
        return (jax.nn.silu(xblk @ w_gate[e]) * (xblk @ w_up[e])) @ w_down[e]

    yb = lax.map(run_block, (xb, block_expert)).reshape(n_blocks * MOE_BLOCK, D)
    out = jnp.zeros((T, D), jnp.float32).at[tok_sorted].add(yb[dest].astype(jnp.float32) * w_sorted[:, None])
    return out.astype(h.dtype)


def hybrid_layer(x, pos, k_past, v_past, shift_prev, wkv_prev, params):
    (norm_mix_g, w_in, rw_mu, rw_w0, rw_w2, rw_a0, rw_a2, rw_g2, rw_k_k, rw_k_a, rw_r_k,
     rw_ln_w, rw_ln_b, w_branch_a, w_branch_b, w_out, norm_ffn_g, router_group_w,
     router_group_b, router_expert_w, router_expert_b, exp_gate, exp_up, exp_down) = params
    B, T, _ = x.shape
    h = rmsnorm(x, norm_mix_g)
    u = h @ w_in
    q, k, v, urw, gate_a, gate_b = jnp.split(u, IN_SPLITS, axis=-1)
    q = rope(q.reshape(B, T, ATT_Q_HEADS, ATT_HEAD_DIM), pos).reshape(B, T, ATT_GROUPS, ATT_KV_HEADS, ATT_HEAD_DIM)
    k = rope(k.reshape(B, T, ATT_KV_HEADS, ATT_HEAD_DIM), pos)
    v = v.reshape(B, T, ATT_KV_HEADS, ATT_HEAD_DIM)
    if k_past is None:
        o_a = dilated_attention_prompt(q, k, v)
        keep = min(MAX_WINDOW, T)
        k_rows, v_rows = k[:, T - keep:], v[:, T - keep:]
    else:
        o_a = dilated_attention_sample(q, jnp.concatenate([k_past, k], axis=1),
                                       jnp.concatenate([v_past, v], axis=1))
        k_rows, v_rows = k, v
    o_a = o_a.reshape(B, T, ATT_OUT_W).astype(x.dtype)
    o_b, shift_new, wkv_new = rwkv7_time_mix(urw, shift_prev, wkv_prev, rw_mu, rw_w0, rw_w2, rw_a0,
                                             rw_a2, rw_g2, rw_k_k, rw_k_a, rw_r_k, rw_ln_w, rw_ln_b)
    merged = jax.nn.sigmoid(gate_a) * (o_a @ w_branch_a) + jax.nn.sigmoid(gate_b) * (o_b @ w_branch_b)
    x = x + merged @ w_out
    h2 = rmsnorm(x, norm_ffn_g)
    x = x + hierarchical_moe(h2.reshape(B * T, D_MODEL), router_group_w, router_group_b,
                             router_expert_w, router_expert_b, exp_gate, exp_up, exp_down).reshape(B, T, D_MODEL)
    return x, k_rows, v_rows, shift_new, wkv_new


def setup_inputs(seed: int = 0) -> dict:
    key = jax.random.key(seed)
    ks = jax.random.split(key, 32)
    f32 = jnp.float32

    def nrm(k, shape, scale):
        return jax.random.normal(k, shape, f32) * scale

    w_buf = min(MAX_WINDOW, PAST_LEN)
    L = DEPTH
    return {
        'x_prompt': nrm(ks[0], (BATCH, SEQ, D_MODEL), 1.0),
        'x_sample': nrm(ks[1], (DEC_BATCH, DEC_SEQ, D_MODEL), 1.0),
        'cache_k': nrm(ks[2], (L, DEC_BATCH, w_buf, ATT_KV_HEADS, ATT_HEAD_DIM), 1.0),
        'cache_v': nrm(ks[3], (L, DEC_BATCH, w_buf, ATT_KV_HEADS, ATT_HEAD_DIM), 1.0),
        'state_shift': nrm(ks[4], (L, DEC_BATCH, RW_FEAT), 1.0),
        'state_wkv': nrm(ks[5], (L, DEC_BATCH, RW_HEADS, RW_HEAD_DIM, RW_HEAD_DIM), 0.5),
        'norm_mix_g': 1.0 + nrm(ks[6], (L, D_MODEL), 0.02),
        'w_in': nrm(ks[7], (L, D_MODEL, IN_W), D_MODEL ** -0.5),
        'rw_mu': jax.random.uniform(ks[8], (L, RW_FEAT), f32),
        'rw_w0': nrm(ks[9], (L, RW_W), 0.5),
        'rw_w2': nrm(ks[10], (L, RW_LORA_W, RW_W), RW_LORA_W ** -0.5),
        'rw_a0': nrm(ks[11], (L, RW_W), 0.1),
        'rw_a2': nrm(ks[12], (L, RW_LORA_A, RW_W), RW_LORA_A ** -0.5),
        'rw_g2': nrm(ks[13], (L, RW_LORA_G, RW_W), RW_LORA_G ** -0.5),
        'rw_k_k': 0.85 + nrm(ks[14], (L, RW_W), 0.02),
        'rw_k_a': 1.0 + nrm(ks[15], (L, RW_W), 0.02),
        'rw_r_k': nrm(ks[16], (L, RW_HEADS, RW_HEAD_DIM), 0.1),
        'rw_ln_w': 1.0 + nrm(ks[17], (L, RW_W), 0.02),
        'rw_ln_b': nrm(ks[18], (L, RW_W), 0.02),
        'w_branch_a': nrm(ks[19], (L, ATT_OUT_W, D_MODEL), ATT_OUT_W ** -0.5),
        'w_branch_b': nrm(ks[20], (L, RW_W, D_MODEL), RW_W ** -0.5),
        'w_out': nrm(ks[21], (L, D_MODEL, D_MODEL), D_MODEL ** -0.5),
        'norm_ffn_g': 1.0 + nrm(ks[22], (L, D_MODEL), 0.02),
        'router_group_w': nrm(ks[23], (L, D_MODEL, MOE_GROUPS), D_MODEL ** -0.5),
        'router_group_b': nrm(ks[24], (L, MOE_GROUPS), 0.01),
        'router_expert_w': nrm(ks[25], (L, D_MODEL, N_EXPERTS), D_MODEL ** -0.5),
        'router_expert_b': nrm(ks[26], (L, N_EXPERTS), 0.01),
        'exp_gate': nrm(ks[27], (L, N_EXPERTS, D_MODEL, EXPERT_FF), D_MODEL ** -0.5),
        'exp_up': nrm(ks[28], (L, N_EXPERTS, D_MODEL, EXPERT_FF), D_MODEL ** -0.5),
        'exp_down': nrm(ks[29], (L, N_EXPERTS, EXPERT_FF, D_MODEL), EXPERT_FF ** -0.5),
        'norm_final_g': 1.0 + nrm(ks[30], (D_MODEL,), 0.02),
    }


def reference(x_prompt, x_sample, cache_k, cache_v, state_shift, state_wkv,
              norm_mix_g, w_in, rw_mu, rw_w0, rw_w2, rw_a0, rw_a2, rw_g2, rw_k_k, rw_k_a,
              rw_r_k, rw_ln_w, rw_ln_b, w_branch_a, w_branch_b, w_out, norm_ffn_g,
              router_group_w, router_group_b, router_expert_w, router_expert_b,
              exp_gate, exp_up, exp_down, norm_final_g):
    bp = x_prompt.shape[0]
    pos_prompt = jnp.arange(x_prompt.shape[1], dtype=jnp.int32)
    pos_sample = PAST_LEN + jnp.arange(x_sample.shape[1], dtype=jnp.int32)
    hp, hs = x_prompt, x_sample
    kp_l, vp_l, ksm_l, vsm_l, sp_l, ssm_l, wp_l, wsm_l = [], [], [], [], [], [], [], []
    for l in range(DEPTH):
        params = (norm_mix_g[l], w_in[l], rw_mu[l], rw_w0[l], rw_w2[l], rw_a0[l], rw_a2[l], rw_g2[l],
                  rw_k_k[l], rw_k_a[l], rw_r_k[l], rw_ln_w[l], rw_ln_b[l], w_branch_a[l], w_branch_b[l],
                  w_out[l], norm_ffn_g[l], router_group_w[l], router_group_b[l], router_expert_w[l],
                  router_expert_b[l], exp_gate[l], exp_up[l], exp_down[l])
        hp, kp, vp, sp, wp = hybrid_layer(
            hp, pos_prompt, None, None,
            jnp.zeros((bp, RW_FEAT), x_prompt.dtype),
            jnp.zeros((bp, RW_HEADS, RW_HEAD_DIM, RW_HEAD_DIM), jnp.float32), params)
        hs, ksm, vsm, ssm, wsm = hybrid_layer(
            hs, pos_sample, cache_k[l], cache_v[l], state_shift[l], state_wkv[l], params)
        kp_l.append(kp); vp_l.append(vp); sp_l.append(sp); wp_l.append(wp)
        ksm_l.append(ksm); vsm_l.append(vsm); ssm_l.append(ssm); wsm_l.append(wsm)
    y_prompt = rmsnorm(hp, norm_final_g)
    y_sample = rmsnorm(hs, norm_final_g)
    k_prompt = jnp.stack(kp_l, axis=0)
    v_prompt = jnp.stack(vp_l, axis=0)
    k_sample = jnp.stack(ksm_l, axis=0)
    v_sample = jnp.stack(vsm_l, axis=0)
    shift_prompt = jnp.stack(sp_l, axis=0)
    shift_sample = jnp.stack(ssm_l, axis=0)
    wkv_prompt = jnp.stack(wp_l, axis=0).astype(state_wkv.dtype)
    wkv_sample = jnp.stack(wsm_l, axis=0).astype(state_wkv.dtype)
    return (y_prompt, y_sample, k_prompt, v_prompt, k_sample, v_sample,
            shift_prompt, shift_sample, wkv_prompt, wkv_sample)
```

```python
import functools
import math

import jax
import jax.numpy as jnp
from jax import lax
from jax.experimental import pallas as pl
from jax.experimental.pallas import tpu as pltpu

F32 = jnp.float32
BF16 = jnp.bfloat16
HI = lax.Precision.HIGHEST

D_MODEL = 4096
NORM_EPS = 1e-6
HEAD_DIM = 128
KV_HEADS = 8
DILATIONS = (1, 4, 16)
BACK = 128
N_GROUPS = 3
ROPE_THETA = 10000.0
ATT_SCALE = HEAD_DIM ** -0.5
Q_W = N_GROUPS * KV_HEADS * HEAD_DIM
KV_W = KV_HEADS * HEAD_DIM
RW_N = 64
RW_HEADS = 32
RW_W = RW_HEADS * RW_N
LORA_W, LORA_A, LORA_G = 96, 96, 256
LORA_ALL = LORA_W + LORA_A + LORA_G
RW_FEAT = 3 * RW_W + LORA_ALL
RW_GN_EPS = 64e-5
RW_DECAY_SCALE = math.exp(-0.5)
COL_K = Q_W
COL_V = Q_W + KV_W
COL_RW = Q_W + 2 * KV_W
COL_LORA = COL_RW + 3 * RW_W
COL_GATE = COL_RW + RW_FEAT
MOE_GROUPS = 8
MOE_PER_GROUP = 8
N_EXPERTS = 64
EXPERT_FF = 1024

LANE = 128
VMEM_LIMIT_BYTES = 56 * 1024 * 1024
MM_TN = 512
LORA_PAD = 512
MOE_TM = 384
MOE_TF = 256
RW_CHUNK = 64
RW_GROUP = 4
NEG = -1e30


def _cparams(*sem):
    return pltpu.CompilerParams(dimension_semantics=sem, vmem_limit_bytes=VMEM_LIMIT_BYTES)


def _dot(a, b):
    return jnp.dot(a.astype(BF16), b.astype(BF16), preferred_element_type=F32)


def _dot_nt(a, b):
    return lax.dot_general(a.astype(BF16), b.astype(BF16), (((1,), (1,)), ((), ())),
                           preferred_element_type=F32)


def _dot_tn(a, b):
    return lax.dot_general(a.astype(BF16), b.astype(BF16), (((0,), (0,)), ((), ())),
                           preferred_element_type=F32)


def _dot_hi(a, b):
    return jnp.dot(a, b, preferred_element_type=F32, precision=HI)


def _sigmoid(x):
    return 1.0 / (1.0 + jnp.exp(-x))


def _div_pow2(x, n):
    return x >> (n.bit_length() - 1)


def _mod_pow2(x, n):
    return x & (n - 1)


def _rmsnorm_kernel(x_ref, g_ref, o_ref):
    x = x_ref[...]
    ms = jnp.mean(x * x, axis=-1, keepdims=True)
    o_ref[...] = (x * lax.rsqrt(ms + NORM_EPS) * g_ref[...]).astype(o_ref.dtype)


def _rmsnorm(x, g, out_dtype, tm=512):
    m, d = x.shape
    return pl.pallas_call(
        _rmsnorm_kernel,
        grid=(m // tm,),
        in_specs=[pl.BlockSpec((tm, d), lambda i: (i, 0)), pl.BlockSpec((1, d), lambda i: (0, 0))],
        out_specs=pl.BlockSpec((tm, d), lambda i: (i, 0)),
        out_shape=jax.ShapeDtypeStruct((m, d), out_dtype),
        compiler_params=_cparams("parallel"),
        name="rmsnorm",
    )(x, g.reshape(1, d))


def _mm_kernel(a_ref, w_ref, *refs, n_extra, epilogue):
    extra = refs[:n_extra]
    outs = refs[n_extra:-1]
    wbf = refs[-1]

    @pl.when(pl.program_id(1) == 0)
    def _():
        wbf[...] = w_ref[...].astype(BF16)

    acc = jnp.dot(a_ref[...], wbf[...], preferred_element_type=F32)
    epilogue(acc, extra, outs)


def _matmul(a, w, n_cols, col_off_tiles, epilogue, extras, out_dtypes, tm, name):
    m, k = a.shape
    tn = MM_TN
    in_specs = [pl.BlockSpec((tm, k), lambda j, i: (i, 0)),
                pl.BlockSpec((k, tn), lambda j, i: (0, j + col_off_tiles))]
    args = [a, w]
    for arr, kind, off in extras:
        if kind == "row":
            in_specs.append(pl.BlockSpec((tm, arr.shape[1]), lambda j, i: (i, 0)))
        elif kind == "tile":
            in_specs.append(pl.BlockSpec((tm, tn), functools.partial(lambda j, i, o: (i, j + o), o=off)))
        else:
            in_specs.append(pl.BlockSpec((1, tn), functools.partial(lambda j, i, o: (0, j + o), o=off)))
        args.append(arr)
    out_specs = [pl.BlockSpec((tm, tn), lambda j, i: (i, j)) for _ in out_dtypes]
    out_shape = [jax.ShapeDtypeStruct((m, n_cols), dt) for dt in out_dtypes]
    res = pl.pallas_call(
        functools.partial(_mm_kernel, n_extra=len(extras), epilogue=epilogue),
        grid=(n_cols // tn, m // tm),
        in_specs=in_specs,
        out_specs=out_specs,
        out_shape=out_shape,
        scratch_shapes=[pltpu.VMEM((k, tn), BF16)],
        compiler_params=_cparams("parallel", "arbitrary"),
        name=name,
    )(*args)
    return res


def _rope_tile(acc, cos, sin_signed):
    parts = []
    for h in range(MM_TN // HEAD_DIM):
        x = acc[:, h * HEAD_DIM:(h + 1) * HEAD_DIM]
        parts.append(x * cos + pltpu.roll(x, HEAD_DIM // 2, axis=1) * sin_signed)
    return jnp.concatenate(parts, axis=1)


def _ep_rope_q(acc, extra, outs):
    outs[0][...] = (_rope_tile(acc, extra[0][...], extra[1][...]) * ATT_SCALE).astype(outs[0].dtype)


def _ep_rope_k(acc, extra, outs):
    outs[0][...] = _rope_tile(acc, extra[0][...], extra[1][...]).astype(outs[0].dtype)


def _ep_plain(acc, extra, outs):
    outs[0][...] = acc.astype(outs[0].dtype)


def _ep_sigmoid(acc, extra, outs):
    outs[0][...] = _sigmoid(acc).astype(outs[0].dtype)


def _ep_residual(acc, extra, outs):
    outs[0][...] = (extra[0][...] + acc).astype(outs[0].dtype)


def _attn_prompt_kernel(q_ref, kp_ref, kc_ref, vp_ref, vc_ref, o_ref, l_ref, *, tq):
    first_neg = jnp.where(pl.program_id(2) == 0, NEG, 0.0)
    qi = lax.broadcasted_iota(jnp.int32, (BACK, BACK), 0)
    kj = lax.broadcasted_iota(jnp.int32, (BACK, BACK), 1)
    prev_band = kj >= qi
    cur_band = kj <= qi
    for h in range(KV_HEADS):
        hs = slice(h * HEAD_DIM, (h + 1) * HEAD_DIM)
        for jb in range(tq // BACK):
            rows = slice(jb * BACK, (jb + 1) * BACK)
            qb = q_ref[0, rows, hs]
            if jb == 0:
                k_prev, v_prev = kp_ref[0, :, hs], vp_ref[0, :, hs]
                prev_neg = first_neg
            else:
                prow = slice((jb - 1) * BACK, jb * BACK)
                k_prev, v_prev = kc_ref[0, prow, hs], vc_ref[0, prow, hs]
                prev_neg = 0.0
            k_cur, v_cur = kc_ref[0, rows, hs], vc_ref[0, rows, hs]
            s_p = jnp.where(prev_band, _dot_nt(qb, k_prev), NEG) + prev_neg
            s_c = jnp.where(cur_band, _dot_nt(qb, k_cur), NEG)
            m = jnp.maximum(jnp.max(s_p, axis=-1, keepdims=True), jnp.max(s_c, axis=-1, keepdims=True))
            e_p = jnp.exp(s_p - m)
            e_c = jnp.exp(s_c - m)
            den = jnp.sum(e_p, axis=-1, keepdims=True) + jnp.sum(e_c, axis=-1, keepdims=True)
            o = (_dot(e_p, v_prev) + _dot(e_c, v_cur)) / den
            o_ref[0, rows, hs] = o
            l_ref[0, rows, hs] = jnp.broadcast_to(m + jnp.log(den), (BACK, HEAD_DIM))


def _attn_prompt_group(q, k, v, g, dil):
    b, s, _ = q.shape
    sub = s // dil
    tq = min(sub, 512)
    nq = sub // tq
    qv = q.reshape(b, sub, dil * Q_W)
    kv_ = k.reshape(b, sub, dil * KV_W)
    vv = v.reshape(b, sub, dil * KV_W)
    per = tq // BACK
    prev_map = lambda bi, r, i: (bi, jnp.maximum(i * per - 1, 0), r)
    cur_map = lambda bi, r, i: (bi, i, r)
    o, lse = pl.pallas_call(
        functools.partial(_attn_prompt_kernel, tq=tq),
        grid=(b, dil, nq),
        in_specs=[pl.BlockSpec((1, tq, KV_W), lambda bi, r, i: (bi, i, r * N_GROUPS + g)),
                  pl.BlockSpec((1, BACK, KV_W), prev_map),
                  pl.BlockSpec((1, tq, KV_W), cur_map),
                  pl.BlockSpec((1, BACK, KV_W), prev_map),
                  pl.BlockSpec((1, tq, KV_W), cur_map)],
        out_specs=[pl.BlockSpec((1, tq, KV_W), cur_map), pl.BlockSpec((1, tq, KV_W), cur_map)],
        out_shape=[jax.ShapeDtypeStruct((b, sub, dil * KV_W), F32)] * 2,
        compiler_params=_cparams("parallel", "parallel", "arbitrary"),
        name=f"attn_prompt_g{g}",
    )(qv, kv_, kv_, vv, vv)
    return o.reshape(b, s, KV_W), lse.reshape(b, s, KV_W)


def _attn_combine_kernel(o0, l0, o1, l1, o2, l2, out_ref):
    la, lb, lc = l0[...], l1[...], l2[...]
    m = jnp.maximum(jnp.maximum(la, lb), lc)
    wa, wb, wc = jnp.exp(la - m), jnp.exp(lb - m), jnp.exp(lc - m)
    out = (wa * o0[...] + wb * o1[...] + wc * o2[...]) / (wa + wb + wc)
    out_ref[...] = out.astype(out_ref.dtype)


def _attn_combine(parts, tm=512):
    m, w = parts[0].shape
    spec = pl.BlockSpec((tm, w), lambda i: (i, 0))
    return pl.pallas_call(
        _attn_combine_kernel,
        grid=(m // tm,),
        in_specs=[spec] * 6,
        out_specs=spec,
        out_shape=jax.ShapeDtypeStruct((m, w), BF16),
        compiler_params=_cparams("parallel"),
        name="attn_combine",
    )(*parts)


def _attn_sample_kernel(q_ref, kn_ref, vn_ref, k0_ref, v0_ref, k1_ref, v1_ref, k2_ref, v2_ref, o_ref, *, n_new):
    q = q_ref[0].astype(F32)
    kn = kn_ref[0]
    vn = vn_ref[0]
    row = lax.broadcasted_iota(jnp.int32, (BACK, 1), 0)
    kc_refs = (k0_ref, k1_ref, k2_ref)
    vc_refs = (v0_ref, v1_ref, v2_ref)
    for h in range(KV_HEADS):
        hs = slice(h * HEAD_DIM, (h + 1) * HEAD_DIM)
        for s in range(n_new):
            outs, lses = [], []
            for g in range(N_GROUPS):
                qv = q[s:s + 1, g * KV_W + h * HEAD_DIM:g * KV_W + (h + 1) * HEAD_DIM]
                if g == 0:
                    cs = hs
                    new = slice(0, s + 1)
                else:
                    cs = slice(s * KV_W + h * HEAD_DIM, s * KV_W + (h + 1) * HEAD_DIM)
                    new = slice(s, s + 1)
                kc = kc_refs[g][0, :, cs]
                vc = vc_refs[g][0, :, cs]
                sc = jnp.sum(kc * qv, axis=-1, keepdims=True)
                if g == 0:
                    sc = jnp.where(row >= s, sc, NEG)
                sn = jnp.sum(kn[new, hs] * qv, axis=-1, keepdims=True)
                m = jnp.maximum(jnp.max(sc, axis=0, keepdims=True), jnp.max(sn, axis=0, keepdims=True))
                p = jnp.exp(sc - m)
                pn = jnp.exp(sn - m)
                den = jnp.sum(p, axis=0, keepdims=True) + jnp.sum(pn, axis=0, keepdims=True)
                o = (jnp.sum(p * vc, axis=0, keepdims=True)
                     + jnp.sum(pn * vn[new, hs], axis=0, keepdims=True)) / den
                outs.append(o)
                lses.append(m + jnp.log(den))
            mm = jnp.maximum(jnp.maximum(lses[0], lses[1]), lses[2])
            ws = [jnp.exp(l - mm) for l in lses]
            comb = (ws[0] * outs[0] + ws[1] * outs[1] + ws[2] * outs[2]) / (ws[0] + ws[1] + ws[2])
            o_ref[0, s:s + 1, hs] = comb.astype(o_ref.dtype)


def _attn_sample(q, k_new, v_new, cache_k, cache_v):
    db, n_new, _ = q.shape
    w_buf = cache_k.shape[1]
    assert w_buf == BACK * DILATIONS[-1] and n_new <= DILATIONS[1]
    specs = [pl.BlockSpec((1, n_new, Q_W), lambda b: (b, 0, 0)),
             pl.BlockSpec((1, n_new, KV_W), lambda b: (b, 0, 0)),
             pl.BlockSpec((1, n_new, KV_W), lambda b: (b, 0, 0))]
    args = [q, k_new, v_new]
    for dil in DILATIONS:
        sub = w_buf // dil
        width = min(dil, n_new) * KV_W
        last = sub // BACK - 1
        spec = pl.BlockSpec((1, BACK, width), functools.partial(lambda b, l: (b, l, 0), l=last))
        for c in (cache_k, cache_v):
            specs.append(spec)
            args.append(c.reshape(db, sub, dil * KV_W))
    return pl.pallas_call(
        functools.partial(_attn_sample_kernel, n_new=n_new),
        grid=(db,),
        in_specs=specs,
        out_specs=pl.BlockSpec((1, n_new, KV_W), lambda b: (b, 0, 0)),
        out_shape=jax.ShapeDtypeStruct((db, n_new, KV_W), BF16),
        compiler_params=_cparams("parallel"),
        name="attn_sample",
    )(*args)


def _head_indicator(width):
    l = lax.broadcasted_iota(jnp.int32, (width, LANE), 0)
    h = lax.broadcasted_iota(jnp.int32, (width, LANE), 1)
    return (_div_pow2(l, RW_N) == h).astype(F32)


def _head_indicator_t(width):
    h = lax.broadcasted_iota(jnp.int32, (LANE, width), 0)
    l = lax.broadcasted_iota(jnp.int32, (LANE, width), 1)
    return (_div_pow2(l, RW_N) == h).astype(F32)


def _rw_prep_kernel(ur, uk, uv, ul, pr, pk, pv, pl_, mur, muk, muv, mul, ww, wa, wg, w0, a0, kk_, ka,
                    r_o, lw_o, k_o, v_o, kk_o, b_o, g_o):
    xr = ur[...] + (pr[...] - ur[...]) * mur[...]
    xk = uk[...] + (pk[...] - uk[...]) * muk[...]
    xv = uv[...] + (pv[...] - uv[...]) * muv[...]
    xl = ul[...] + (pl_[...] - ul[...]) * mul[...]
    col = lax.broadcasted_iota(jnp.int32, xl.shape, 1)
    act = jnp.where(col < LORA_W, jnp.tanh(xl),
                    jnp.where(col < LORA_W + LORA_A, xl,
                              jnp.where(col < LORA_ALL, _sigmoid(xl), 0.0)))
    zw = _dot(act, ww[...])
    za = _dot(act, wa[...])
    zg = _dot(act, wg[...])
    lw = -RW_DECAY_SCALE * _sigmoid(w0[...] + zw)
    a = _sigmoid(a0[...] + za)
    kk = xk * kk_[...]
    width = kk.shape[1]
    ss = _dot_hi(kk * kk, _head_indicator(width))
    inv = 1.0 / jnp.maximum(jnp.sqrt(ss), 1e-12)
    kkn = kk * _dot_hi(inv, _head_indicator_t(width))
    r_o[...] = xr
    lw_o[...] = lw
    k_o[...] = xk * (1.0 + (a - 1.0) * ka[...])
    v_o[...] = xv
    kk_o[...] = kkn
    b_o[...] = kkn * a
    g_o[...] = zg


def _rw_prep(u_rkv, u_lora, p_rkv, p_lora, mu, w_w, w_a, w_g, w0, a0, k_k, k_a, tm=512, tw=512):
    m = u_rkv.shape[0]
    nj = RW_W // tw
    mu_rkv = mu[:3 * RW_W].reshape(1, 3 * RW_W)
    mu_l = jnp.pad(mu[3 * RW_W:], (0, LORA_PAD - LORA_ALL)).reshape(1, LORA_PAD)

    def feat(off):
        return pl.BlockSpec((tm, tw), functools.partial(lambda i, j, o: (i, j + o), o=off))

    def vec(off):
        return pl.BlockSpec((1, tw), functools.partial(lambda i, j, o: (0, j + o), o=off))

    lspec = pl.BlockSpec((tm, LORA_PAD), lambda i, j: (i, 0))
    wspec = pl.BlockSpec((LORA_PAD, tw), lambda i, j: (0, j))
    in_specs = [feat(0), feat(nj), feat(2 * nj), lspec,
                feat(0), feat(nj), feat(2 * nj), lspec,
                vec(0), vec(nj), vec(2 * nj), pl.BlockSpec((1, LORA_PAD), lambda i, j: (0, 0)),
                wspec, wspec, wspec, vec(0), vec(0), vec(0), vec(0)]
    out_spec = pl.BlockSpec((tm, tw), lambda i, j: (i, j))
    outs = pl.pallas_call(
        _rw_prep_kernel,
        grid=(m // tm, nj),
        in_specs=in_specs,
        out_specs=[out_spec] * 7,
        out_shape=[jax.ShapeDtypeStruct((m, RW_W), F32)] * 7,
        compiler_params=_cparams("parallel", "arbitrary"),
        name="rw_prep",
    )(u_rkv, u_rkv, u_rkv, u_lora, p_rkv, p_rkv, p_rkv, p_lora, mu_rkv, mu_rkv, mu_rkv, mu_l,
      w_w, w_a, w_g, w0.reshape(1, RW_W), a0.reshape(1, RW_W), k_k.reshape(1, RW_W), k_a.reshape(1, RW_W))
    return outs


def _rw_scan_kernel(r_ref, lw_ref, k_ref, v_ref, kk_ref, b_ref, s_in, y_ref, s_out, s_bd, *, chunk):
    t_id = pl.program_id(1)
    n_t = pl.num_programs(1)
    gw = RW_GROUP * RW_N
    n_groups = RW_W // gw
    rows = RW_GROUP * chunk

    ri = lax.broadcasted_iota(jnp.int32, (gw, gw), 0)
    ci = lax.broadcasted_iota(jnp.int32, (gw, gw), 1)
    state_mask = _div_pow2(ri, RW_N) == _div_pow2(ci, RW_N)

    @pl.when(t_id == 0)
    def _():
        kr = lax.broadcasted_iota(jnp.int32, (RW_N, gw), 0)
        kc = lax.broadcasted_iota(jnp.int32, (RW_N, gw), 1)
        spread = (_mod_pow2(kc, RW_N) == kr).astype(F32)
        for g in range(n_groups):
            tiled = _dot_hi(s_in[0, g * gw:(g + 1) * gw, :], spread)
            s_bd[g] = jnp.where(state_mask, tiled, 0.0)

    ti = lax.broadcasted_iota(jnp.int32, (chunk, chunk), 0)
    tj = lax.broadcasted_iota(jnp.int32, (chunk, chunk), 1)
    lw = lw_ref[0]
    cum = _dot_hi((tj <= ti).astype(F32), lw)
    e_pos = jnp.exp(cum)
    e_neg = jnp.exp(-cum)
    e_prev = jnp.exp(cum - lw)
    cum_last = cum[chunk - 1:chunk, :]
    e_rem = jnp.exp(cum_last - cum)
    c_all = jnp.exp(cum_last)
    kt = kk_ref[0] * e_prev
    bt = b_ref[0] * e_neg
    k2t = k_ref[0] * e_neg
    rt = r_ref[0] * e_pos
    btc = b_ref[0] * e_rem
    k2tc = k_ref[0] * e_rem
    vv = v_ref[0]

    sr = lax.broadcasted_iota(jnp.int32, (rows, gw), 0)
    sc = lax.broadcasted_iota(jnp.int32, (rows, gw), 1)
    stack_mask = _div_pow2(sr, chunk) == _div_pow2(sc, RW_N)
    ar = lax.broadcasted_iota(jnp.int32, (rows, rows), 0)
    ac = lax.broadcasted_iota(jnp.int32, (rows, rows), 1)
    strict = ar > ac
    incl = ar >= ac
    eye = (ar == ac).astype(F32)

    def stack(x):
        return jnp.where(stack_mask, jnp.concatenate([x] * RW_GROUP, axis=0), 0.0).astype(BF16)

    for g in range(n_groups):
        ls = slice(g * gw, (g + 1) * gw)
        kt_s, rt_s = stack(kt[:, ls]), stack(rt[:, ls])
        bt_s, k2t_s = stack(bt[:, ls]), stack(k2t[:, ls])
        v_s = stack(vv[:, ls])
        s0 = s_bd[g]
        a_b = jnp.where(strict, _dot_nt(kt_s, bt_s), 0.0)
        a_k = jnp.where(strict, _dot_nt(kt_s, k2t_s), 0.0)
        r_b = jnp.where(incl, _dot_nt(rt_s, bt_s), 0.0)
        r_k = jnp.where(incl, _dot_nt(rt_s, k2t_s), 0.0)
        rhs = _dot_nt(kt_s, s0) + _dot(a_k, v_s)
        inv = eye - a_b
        pw = a_b
        n = 2
        while n < chunk:
            pw = _dot(pw, pw)
            inv = inv + _dot(inv, pw)
            n *= 2
        u = -_dot(inv, rhs)
        y_bd = _dot_nt(rt_s, s0) + _dot(r_b, u) + _dot(r_k, v_s)
        y = y_bd[0:chunk]
        for h in range(1, RW_GROUP):
            y = y + y_bd[h * chunk:(h + 1) * chunk]
        y_ref[0, :, ls] = y
        s_bd[g] = s0 * c_all[:, ls] + _dot_tn(u, stack(btc[:, ls])) + _dot_tn(v_s, stack(k2tc[:, ls]))

    @pl.when(t_id == n_t - 1)
    def _():
        gr = lax.broadcasted_iota(jnp.int32, (gw, RW_N), 0)
        gc = lax.broadcasted_iota(jnp.int32, (gw, RW_N), 1)
        gather = (_mod_pow2(gr, RW_N) == gc).astype(F32)
        for g in range(n_groups):
            s_out[0, g * gw:(g + 1) * gw, :] = _dot_hi(s_bd[g], gather)


def _rw_scan(r, lw, k, v, kk, b, state, chunk):
    bn, t, _ = r.shape
    gw = RW_GROUP * RW_N
    seq = pl.BlockSpec((1, chunk, RW_W), lambda bi, ti: (bi, ti, 0))
    st = pl.BlockSpec((1, RW_W, RW_N), lambda bi, ti: (bi, 0, 0))
    y, s_new = pl.pallas_call(
        functools.partial(_rw_scan_kernel, chunk=chunk),
        grid=(bn, t // chunk),
        in_specs=[seq] * 6 + [st],
        out_specs=[seq, st],
        out_shape=[jax.ShapeDtypeStruct((bn, t, RW_W), F32), jax.ShapeDtypeStruct((bn, RW_W, RW_N), F32)],
        scratch_shapes=[pltpu.VMEM((RW_W // gw, gw, gw), F32)],
        compiler_params=_cparams("parallel", "arbitrary"),
        name=f"rw_scan_c{chunk}",
    )(r, lw, k, v, kk, b, state.reshape(bn, RW_W, RW_N))
    return y, s_new.reshape(bn, RW_HEADS, RW_N, RW_N)


def _rw_post_kernel(y_ref, r_ref, k_ref, v_ref, g_ref, lnw, lnb, rk, o_ref):
    y = y_ref[...]
    width = y.shape[1]
    ind, ind_t = _head_indicator(width), _head_indicator_t(width)
    mean = _dot_hi(_dot_hi(y, ind) * (1.0 / RW_N), ind_t)
    d = y - mean
    var_h = _dot_hi(d * d, ind) * (1.0 / RW_N)
    rstd = _dot_hi(lax.rsqrt(var_h + RW_GN_EPS), ind_t)
    yn = d * rstd * lnw[...] + lnb[...]
    bonus = _dot_hi(_dot_hi(r_ref[...] * k_ref[...] * rk[...], ind), ind_t) * v_ref[...]
    o_ref[...] = ((yn + bonus) * g_ref[...]).astype(o_ref.dtype)


def _rw_post(y, r, k, v, g, ln_w, ln_b, r_k, tm=512, tw=512):
    m = y.shape[0]
    feat = pl.BlockSpec((tm, tw), lambda i, j: (i, j))
    vec = pl.BlockSpec((1, tw), lambda i, j: (0, j))
    return pl.pallas_call(
        _rw_post_kernel,
        grid=(m // tm, RW_W // tw),
        in_specs=[feat] * 5 + [vec] * 3,
        out_specs=feat,
        out_shape=jax.ShapeDtypeStruct((m, RW_W), BF16),
        compiler_params=_cparams("parallel", "parallel"),
        name="rw_post",
    )(y, r, k, v, g, ln_w.reshape(1, RW_W), ln_b.reshape(1, RW_W), r_k.reshape(1, RW_W))


def _merge_kernel(oa_ref, ob_ref, wa_ref, wb_ref, ga_ref, gb_ref, out_ref, wa_bf, wb_bf):
    @pl.when(pl.program_id(1) == 0)
    def _():
        wa_bf[...] = wa_ref[...].astype(BF16)
        wb_bf[...] = wb_ref[...].astype(BF16)

    ya = jnp.dot(oa_ref[...], wa_bf[...], preferred_element_type=F32)
    yb = jnp.dot(ob_ref[...], wb_bf[...], preferred_element_type=F32)
    out_ref[...] = (ga_ref[...].astype(F32) * ya + gb_ref[...].astype(F32) * yb).astype(out_ref.dtype)


def _merge(o_a, o_b, w_a, w_b, gates, tm):
    m = o_a.shape[0]
    tn = MM_TN
    nj = D_MODEL // tn
    return pl.pallas_call(
        _merge_kernel,
        grid=(nj, m // tm),
        in_specs=[pl.BlockSpec((tm, KV_W), lambda j, i: (i, 0)),
                  pl.BlockSpec((tm, RW_W), lambda j, i: (i, 0)),
                  pl.BlockSpec((KV_W, tn), lambda j, i: (0, j)),
                  pl.BlockSpec((RW_W, tn), lambda j, i: (0, j)),
                  pl.BlockSpec((tm, tn), lambda j, i: (i, j)),
                  pl.BlockSpec((tm, tn), lambda j, i: (i, j + nj))],
        out_specs=pl.BlockSpec((tm, tn), lambda j, i: (i, j)),
        out_shape=jax.ShapeDtypeStruct((m, D_MODEL), BF16),
        scratch_shapes=[pltpu.VMEM((KV_W, tn), BF16), pltpu.VMEM((RW_W, tn), BF16)],
        compiler_params=_cparams("parallel", "arbitrary"),
        name="merge",
    )(o_a, o_b, w_a, w_b, gates, gates)


def _router_kernel(x_ref, g_ref, wr_ref, br_ref, h_ref, id_ref, wt_ref):
    x = x_ref[...]
    ms = jnp.mean(x * x, axis=-1, keepdims=True)
    h = x * lax.rsqrt(ms + NORM_EPS) * g_ref[...]
    h_ref[...] = h.astype(h_ref.dtype)
    logits = _dot_hi(h, wr_ref[...]) + br_ref[...]
    lane = lax.broadcasted_iota(jnp.int32, logits.shape, 1)
    lane_f = lane.astype(F32)
    gmask = lane < MOE_GROUPS
    gl = jnp.where(gmask, logits, NEG)
    gm = jnp.max(gl, axis=-1, keepdims=True)
    gi = jnp.min(jnp.where(gl == gm, lane_f, float(LANE)), axis=-1, keepdims=True)
    g_prob = 1.0 / jnp.sum(jnp.where(gmask, jnp.exp(gl - gm), 0.0), axis=-1, keepdims=True)
    lo = MOE_GROUPS + gi * MOE_PER_GROUP
    emask = jnp.logical_and(lane_f >= lo, lane_f < lo + MOE_PER_GROUP)
    el = jnp.where(emask, logits, NEG)
    m1 = jnp.max(el, axis=-1, keepdims=True)
    i1 = jnp.min(jnp.where(el == m1, lane_f, float(LANE)), axis=-1, keepdims=True)
    el2 = jnp.where(lane_f == i1, NEG, el)
    m2 = jnp.max(el2, axis=-1, keepdims=True)
    i2 = jnp.min(jnp.where(el2 == m2, lane_f, float(LANE)), axis=-1, keepdims=True)
    t = jnp.exp(m2 - m1)
    w1 = g_prob / (1.0 + t)
    w2 = g_prob * t / (1.0 + t)
    ids = jnp.where(lane == 0, i1 - MOE_GROUPS, jnp.where(lane == 1, i2 - MOE_GROUPS, 0.0))
    id_ref[...] = ids.astype(jnp.int32)
    wt_ref[...] = jnp.where(lane == 0, w1, jnp.where(lane == 1, w2, 0.0))


def _router(x, g, w_router, b_router, tm=256):
    m, d = x.shape
    row = pl.BlockSpec((tm, d), lambda i: (i, 0))
    small = pl.BlockSpec((tm, LANE), lambda i: (i, 0))
    return pl.pallas_call(
        _router_kernel,
        grid=(m // tm,),
        in_specs=[row, pl.BlockSpec((1, d), lambda i: (0, 0)),
                  pl.BlockSpec((d, LANE), lambda i: (0, 0)), pl.BlockSpec((1, LANE), lambda i: (0, 0))],
        out_specs=[row, small, small],
        out_shape=[jax.ShapeDtypeStruct((m, d), BF16), jax.ShapeDtypeStruct((m, LANE), jnp.int32),
                   jax.ShapeDtypeStruct((m, LANE), F32)],
        compiler_params=_cparams("parallel"),
        name="ffn_norm_router",
    )(x, g.reshape(1, d), w_router, b_router)


def _moe_kernel(te_ref, tv_ref, x_ref, wg_ref, wu_ref, wd_ref, rw_ref, o_ref):
    i = pl.program_id(0)
    j = pl.program_id(1)
    valid = tv_ref[i] > 0

    @pl.when(j == 0)
    def _():
        o_ref[...] = jnp.zeros_like(o_ref)

    @pl.when(valid)
    def _():
        x = x_ref[...]
        gate = jnp.dot(x, wg_ref[0].astype(BF16), preferred_element_type=F32)
        up = jnp.dot(x, wu_ref[0].astype(BF16), preferred_element_type=F32)
        hidden = (gate * _sigmoid(gate) * up).astype(BF16)
        o_ref[...] += jnp.dot(hidden, wd_ref[0].astype(BF16), preferred_element_type=F32)

    @pl.when(jnp.logical_and(valid, j == pl.num_programs(1) - 1))
    def _():
        o_ref[...] = o_ref[...] * rw_ref[...]


def _moe_experts(xs, row_w, tile_expert, tile_valid, w_gate, w_up, w_down):
    p, d = xs.shape
    n_tiles = p // MOE_TM
    nf = EXPERT_FF // MOE_TF

    def f_idx(i, j, tv):
        return jnp.where(tv[i] > 0, j, nf - 1)

    grid_spec = pltpu.PrefetchScalarGridSpec(
        num_scalar_prefetch=2,
        grid=(n_tiles, nf),
        in_specs=[pl.BlockSpec((MOE_TM, d), lambda i, j, te, tv: (i, 0)),
                  pl.BlockSpec((1, d, MOE_TF), lambda i, j, te, tv: (te[i], 0, f_idx(i, j, tv))),
                  pl.BlockSpec((1, d, MOE_TF), lambda i, j, te, tv: (te[i], 0, f_idx(i, j, tv))),
                  pl.BlockSpec((1, MOE_TF, d), lambda i, j, te, tv: (te[i], f_idx(i, j, tv), 0)),
                  pl.BlockSpec((MOE_TM, 1), lambda i, j, te, tv: (i, 0))],
        out_specs=pl.BlockSpec((MOE_TM, d), lambda i, j, te, tv: (i, 0)),
    )
    return pl.pallas_call(
        _moe_kernel,
        grid_spec=grid_spec,
        out_shape=jax.ShapeDtypeStruct((p, d), F32),
        compiler_params=_cparams("arbitrary", "arbitrary"),
        name="moe_experts",
    )(tile_expert, tile_valid, xs, w_gate, w_up, w_down, row_w)


def _dispatch_plan(ids, wts):
    t = ids.shape[0]
    a = t * 2
    flat_e = ids.reshape(a)
    onehot = (flat_e[:, None] == jnp.arange(N_EXPERTS, dtype=jnp.int32)[None, :]).astype(jnp.int32)
    csum = jnp.cumsum(onehot, axis=0)
    counts = csum[-1]
    rank = jnp.take_along_axis(csum, flat_e[:, None], axis=1)[:, 0] - 1
    tiles_per = (counts + MOE_TM - 1) // MOE_TM
    tile_end = jnp.cumsum(tiles_per)
    tile_start = tile_end - tiles_per
    dest = tile_start[flat_e] * MOE_TM + rank
    n_tiles = -(-a // MOE_TM) + N_EXPERTS
    total = tile_end[-1]
    slot_tok = jnp.zeros((n_tiles * MOE_TM,), jnp.int32).at[dest].set(jnp.arange(a, dtype=jnp.int32) // 2)
    slot_w = jnp.zeros((n_tiles * MOE_TM,), F32).at[dest].set(wts.reshape(a))
    tile_ids = jnp.arange(n_tiles, dtype=jnp.int32)
    tile_valid = (tile_ids < total).astype(jnp.int32)
    tile_expert = jnp.searchsorted(tile_end, jnp.minimum(tile_ids, total - 1), side="right").astype(jnp.int32)
    tile_expert = jnp.minimum(tile_expert, N_EXPERTS - 1)
    return slot_tok, slot_w, dest.reshape(t, 2), tile_expert, tile_valid


def _final_kernel(x_ref, ya_ref, yb_ref, g_ref, o_ref):
    x = x_ref[...] + (ya_ref[...] + yb_ref[...])
    ms = jnp.mean(x * x, axis=-1, keepdims=True)
    o_ref[...] = x * lax.rsqrt(ms + NORM_EPS) * g_ref[...]


def _final(x, ya, yb, g, row_off, rows, tm=256):
    d = x.shape[1]
    off = row_off // tm
    spec = pl.BlockSpec((tm, d), lambda i: (i + off, 0))
    return pl.pallas_call(
        _final_kernel,
        grid=(rows // tm,),
        in_specs=[spec, spec, spec, pl.BlockSpec((1, d), lambda i: (0, 0))],
        out_specs=pl.BlockSpec((tm, d), lambda i: (i, 0)),
        out_shape=jax.ShapeDtypeStruct((rows, d), F32),
        compiler_params=_cparams("parallel"),
        name="final_norm",
    )(x, ya, yb, g.reshape(1, d))


def _rope_tables(pos):
    half = HEAD_DIM // 2
    inv_freq = ROPE_THETA ** (-jnp.arange(half, dtype=F32) / half)
    ang = pos.astype(F32)[:, None] * inv_freq[None, :]
    cos, sin = jnp.cos(ang), jnp.sin(ang)
    return jnp.concatenate([cos, cos], axis=-1), jnp.concatenate([-sin, sin], axis=-1)


def kernel(x_prompt, x_sample, cache_k, cache_v, state_shift, state_wkv, norm_mix_g, w_in, rw_mu, rw_w0, rw_w2,
           rw_a0, rw_a2, rw_g2, rw_k_k, rw_k_a, rw_r_k, rw_ln_w, rw_ln_b, w_branch_a, w_branch_b, w_out,
           norm_ffn_g, router_group_w, router_group_b, router_expert_w, router_expert_b, exp_gate, exp_up,
           exp_down, norm_final_g):
    assert w_in.shape[0] == 1, "single-layer trunk"
    bp, sp, d = x_prompt.shape
    db, ds, _ = x_sample.shape
    mp, ms_ = bp * sp, db * ds
    m = mp + ms_
    past = cache_k.shape[2]
    tm_mm = m // 8

    x_all = jnp.concatenate([x_prompt.reshape(mp, d), x_sample.reshape(ms_, d)], axis=0)
    h = _rmsnorm(x_all, norm_mix_g[0], BF16)

    w_in2 = w_in.reshape(d, w_in.shape[2])
    cos_p, sin_p = _rope_tables(jnp.arange(sp, dtype=jnp.int32))
    cos_s, sin_s = _rope_tables(past + jnp.arange(ds, dtype=jnp.int32))
    cos = jnp.concatenate([jnp.tile(cos_p, (bp, 1)), jnp.tile(cos_s, (db, 1))], axis=0)
    sin = jnp.concatenate([jnp.tile(sin_p, (bp, 1)), jnp.tile(sin_s, (db, 1))], axis=0)
    rope_extras = [(cos, "row", 0), (sin, "row", 0)]
    (q,) = _matmul(h, w_in2, Q_W, 0, _ep_rope_q, rope_extras, [BF16], tm_mm, "inproj_q")
    (k,) = _matmul(h, w_in2, KV_W, COL_K // MM_TN, _ep_rope_k, rope_extras, [F32], tm_mm, "inproj_k")
    (v,) = _matmul(h, w_in2, KV_W, COL_V // MM_TN, _ep_plain, [], [F32], tm_mm, "inproj_v")
    (u_rkv,) = _matmul(h, w_in2, 3 * RW_W, COL_RW // MM_TN, _ep_plain, [], [F32], tm_mm, "inproj_rkv")
    (u_lora,) = _matmul(h, w_in2, LORA_PAD, COL_LORA // MM_TN, _ep_plain, [], [F32], tm_mm, "inproj_lora")
    w_gates = w_in2[:, COL_GATE:]
    (gates,) = _matmul(h, w_gates, 2 * d, 0, _ep_sigmoid, [], [BF16], tm_mm, "inproj_gates")

    qp = q[:mp].reshape(bp, sp, Q_W)
    kp = k[:mp].reshape(bp, sp, KV_W)
    vp = v[:mp].reshape(bp, sp, KV_W)
    parts = []
    for g, dil in enumerate(DILATIONS):
        o_g, l_g = _attn_prompt_group(qp, kp, vp, g, dil)
        parts += [o_g.reshape(mp, KV_W), l_g.reshape(mp, KV_W)]
    oa_p = _attn_combine(parts)
    ks = k[mp:].reshape(db, ds, KV_W)
    vs = v[mp:].reshape(db, ds, KV_W)
    oa_s = _attn_sample(q[mp:].reshape(db, ds, Q_W), ks, vs,
                        cache_k[0].reshape(db, past, KV_W), cache_v[0].reshape(db, past, KV_W))
    o_a = jnp.concatenate([oa_p, oa_s.reshape(ms_, KV_W)], axis=0)

    def shifted(u, width, first_rows):
        up = u[:mp].reshape(bp, sp, width)
        us = u[mp:].reshape(db, ds, width)
        pp = jnp.concatenate([jnp.zeros((bp, 1, width), F32), up[:, :-1]], axis=1)
        ps = jnp.concatenate([first_rows[:, None, :], us[:, :-1]], axis=1)
        return jnp.concatenate([pp.reshape(mp, width), ps.reshape(ms_, width)], axis=0)

    shift0 = state_shift[0]
    p_rkv = shifted(u_rkv, 3 * RW_W, shift0[:, :3 * RW_W])
    p_lora = shifted(u_lora, LORA_PAD, jnp.pad(shift0[:, 3 * RW_W:], ((0, 0), (0, LORA_PAD - LORA_ALL))))
    zl = functools.partial(jnp.zeros, dtype=F32)
    w_w = zl((LORA_PAD, RW_W)).at[:LORA_W].set(rw_w2[0])
    w_a = zl((LORA_PAD, RW_W)).at[LORA_W:LORA_W + LORA_A].set(rw_a2[0])
    w_g = zl((LORA_PAD, RW_W)).at[LORA_W + LORA_A:LORA_ALL].set(rw_g2[0])
    r_, lw_, k2_, v_, kk_, b_, g_ = _rw_prep(u_rkv, u_lora, p_rkv, p_lora, rw_mu[0], w_w, w_a, w_g,
                                             rw_w0[0], rw_a0[0], rw_k_k[0], rw_k_a[0])

    def seqs(arr, lo, hi, nb, t):
        return arr[lo:hi].reshape(nb, t, RW_W)

    scan_in = (r_, lw_, k2_, v_, kk_, b_)
    y_p, wkv_p = _rw_scan(*(seqs(a, 0, mp, bp, sp) for a in scan_in),
                          jnp.zeros((bp, RW_HEADS, RW_N, RW_N), F32), RW_CHUNK)
    pad_t = 8 - ds
    y_s, wkv_s = _rw_scan(*(jnp.pad(seqs(a, mp, m, db, ds), ((0, 0), (0, pad_t), (0, 0))) for a in scan_in),
                          state_wkv[0], 8)
    y_rw = jnp.concatenate([y_p.reshape(mp, RW_W), y_s[:, :ds].reshape(ms_, RW_W)], axis=0)
    o_b = _rw_post(y_rw, r_, k2_, v_, g_, rw_ln_w[0], rw_ln_b[0], rw_r_k[0])

    merged = _merge(o_a, o_b, w_branch_a[0], w_branch_b[0], gates, tm_mm)
    (x1,) = _matmul(merged, w_out.reshape(d, d), d, 0, _ep_residual, [(x_all, "tile", 0)], [F32], tm_mm, "out_proj")

    w_router = jnp.concatenate([router_group_w[0], router_expert_w[0],
                                jnp.zeros((d, LANE - MOE_GROUPS - N_EXPERTS), F32)], axis=1)
    b_router = jnp.concatenate([router_group_b[0], router_expert_b[0],
                                jnp.zeros((LANE - MOE_GROUPS - N_EXPERTS,), F32)]).reshape(1, LANE)
    h2, ids, wts = _router(x1, norm_ffn_g[0], w_router, b_router)
    slot_tok, slot_w, dest, tile_expert, tile_valid = _dispatch_plan(ids[:, :2], wts[:, :2])
    xs = jnp.take(h2, slot_tok, axis=0)
    yb = _moe_experts(xs, slot_w[:, None], tile_expert, tile_valid, exp_gate[0], exp_up[0], exp_down[0])
    ya0 = jnp.take(yb, dest[:, 0], axis=0)
    ya1 = jnp.take(yb, dest[:, 1], axis=0)
    y_prompt = _final(x1, ya0, ya1, norm_final_g, 0, mp).reshape(bp, sp, d)
    y_sample = _final(x1, ya0, ya1, norm_final_g, mp, ms_).reshape(db, ds, d)

    keep = min(BACK * DILATIONS[-1], sp)
    k_prompt = k[:mp].reshape(1, bp, sp, KV_HEADS, HEAD_DIM)[:, :, sp - keep:]
    v_prompt = v[:mp].reshape(1, bp, sp, KV_HEADS, HEAD_DIM)[:, :, sp - keep:]
    k_sample = ks.reshape(1, db, ds, KV_HEADS, HEAD_DIM)
    v_sample = vs.reshape(1, db, ds, KV_HEADS, HEAD_DIM)

    def last_rows(lo, nb, t):
        a = u_rkv[lo:lo + nb * t].reshape(nb, t, 3 * RW_W)[:, -1]
        b = u_lora[lo:lo + nb * t].reshape(nb, t, LORA_PAD)[:, -1, :LORA_ALL]
        return jnp.concatenate([a, b], axis=-1)[None]

    return (y_prompt, y_sample, k_prompt, v_prompt, k_sample, v_sample,
            last_rows(0, bp, sp), last_rows(mp, db, ds), wkv_p[None], wkv_s[None])
```

```python
import functools
import math

import jax
import jax.numpy as jnp
from jax import lax
from jax.experimental import pallas as pl
from jax.experimental.pallas import tpu as pltpu

F32 = jnp.float32
BF16 = jnp.bfloat16
HI = lax.Precision.HIGHEST

D_MODEL = 4096
NORM_EPS = 1e-6
HEAD_DIM = 128
KV_HEADS = 8
DILATIONS = (1, 4, 16)
BACK = 128
N_GROUPS = 3
ROPE_THETA = 10000.0
ATT_SCALE = HEAD_DIM ** -0.5
Q_W = N_GROUPS * KV_HEADS * HEAD_DIM
KV_W = KV_HEADS * HEAD_DIM
RW_N = 64
RW_HEADS = 32
RW_W = RW_HEADS * RW_N
LORA_W, LORA_A, LORA_G = 96, 96, 256
LORA_ALL = LORA_W + LORA_A + LORA_G
RW_FEAT = 3 * RW_W + LORA_ALL
RW_GN_EPS = 64e-5
RW_DECAY_SCALE = math.exp(-0.5)
COL_K = Q_W
COL_V = Q_W + KV_W
COL_RW = Q_W + 2 * KV_W
COL_LORA = COL_RW + 3 * RW_W
COL_GATE = COL_RW + RW_FEAT
MOE_GROUPS = 8
MOE_PER_GROUP = 8
N_EXPERTS = 64
EXPERT_FF = 1024

LANE = 128
VMEM_LIMIT_BYTES = 56 * 1024 * 1024
MM_TN = 512
LORA_PAD = 512
MOE_TM = 384
MOE_TF = 256
RW_CHUNK = 64
RW_GROUP = 4
NEG = -1e30


def _cparams(*sem):
    return pltpu.CompilerParams(dimension_semantics=sem, vmem_limit_bytes=VMEM_LIMIT_BYTES)


def _dot(a, b):
    return jnp.dot(a.astype(BF16), b.astype(BF16), preferred_element_type=F32)


def _dot_nt(a, b):
    return lax.dot_general(a.astype(BF16), b.astype(BF16), (((1,), (1,)), ((), ())),
                           preferred_element_type=F32)


def _dot_tn(a, b):
    return lax.dot_general(a.astype(BF16), b.astype(BF16), (((0,), (0,)), ((), ())),
                           preferred_element_type=F32)


def _dot_hi(a, b):
    return jnp.dot(a, b, preferred_element_type=F32, precision=HI)


def _bf16_terms(x, terms):
    parts = []
    for _ in range(terms):
        p = x.astype(BF16)
        parts.append(p)
        x = x - p.astype(F32)
    return parts


def _dot_sel(x, sel, terms):
    sel = sel.astype(BF16)
    return sum(jnp.dot(p, sel, preferred_element_type=F32) for p in _bf16_terms(x, terms))


def _sel_dot(sel, x, terms):
    sel = sel.astype(BF16)
    return sum(jnp.dot(sel, p, preferred_element_type=F32) for p in _bf16_terms(x, terms))


def _sigmoid(x):
    return 1.0 / (1.0 + jnp.exp(-x))


def _div_pow2(x, n):
    return x >> (n.bit_length() - 1)


def _mod_pow2(x, n):
    return x & (n - 1)


def _rmsnorm_kernel(x_ref, g_ref, o_ref):
    x = x_ref[...]
    ms = jnp.mean(x * x, axis=-1, keepdims=True)
    o_ref[...] = (x * lax.rsqrt(ms + NORM_EPS) * g_ref[...]).astype(o_ref.dtype)


def _rmsnorm(x, g, out_dtype, tm=512):
    m, d = x.shape
    return pl.pallas_call(
        _rmsnorm_kernel,
        grid=(m // tm,),
        in_specs=[pl.BlockSpec((tm, d), lambda i: (i, 0)), pl.BlockSpec((1, d), lambda i: (0, 0))],
        out_specs=pl.BlockSpec((tm, d), lambda i: (i, 0)),
        out_shape=jax.ShapeDtypeStruct((m, d), out_dtype),
        compiler_params=_cparams("parallel"),
        name="rmsnorm",
    )(x, g.reshape(1, d))


def _mm_kernel(a_ref, w_ref, *refs, n_extra, epilogue):
    extra = refs[:n_extra]
    outs = refs[n_extra:-1]
    wbf = refs[-1]

    @pl.when(pl.program_id(1) == 0)
    def _():
        wbf[...] = w_ref[...].astype(BF16)

    acc = jnp.dot(a_ref[...], wbf[...], preferred_element_type=F32)
    epilogue(acc, extra, outs)


def _matmul(a, w, n_cols, col_off_tiles, epilogue, extras, out_dtypes, tm, name):
    m, k = a.shape
    tn = MM_TN
    in_specs = [pl.BlockSpec((tm, k), lambda j, i: (i, 0)),
                pl.BlockSpec((k, tn), lambda j, i: (0, j + col_off_tiles))]
    args = [a, w]
    for arr, kind, off in extras:
        if kind == "row":
            in_specs.append(pl.BlockSpec((tm, arr.shape[1]), lambda j, i: (i, 0)))
        elif kind == "tile":
            in_specs.append(pl.BlockSpec((tm, tn), functools.partial(lambda j, i, o: (i, j + o), o=off)))
        else:
            in_specs.append(pl.BlockSpec((1, tn), functools.partial(lambda j, i, o: (0, j + o), o=off)))
        args.append(arr)
    out_specs = [pl.BlockSpec((tm, tn), lambda j, i: (i, j)) for _ in out_dtypes]
    out_shape = [jax.ShapeDtypeStruct((m, n_cols), dt) for dt in out_dtypes]
    res = pl.pallas_call(
        functools.partial(_mm_kernel, n_extra=len(extras), epilogue=epilogue),
        grid=(n_cols // tn, m // tm),
        in_specs=in_specs,
        out_specs=out_specs,
        out_shape=out_shape,
        scratch_shapes=[pltpu.VMEM((k, tn), BF16)],
        compiler_params=_cparams("parallel", "arbitrary"),
        name=name,
    )(*args)
    return res


def _rope_tile(acc, cos, sin_signed):
    parts = []
    for h in range(MM_TN // HEAD_DIM):
        x = acc[:, h * HEAD_DIM:(h + 1) * HEAD_DIM]
        parts.append(x * cos + pltpu.roll(x, HEAD_DIM // 2, axis=1) * sin_signed)
    return jnp.concatenate(parts, axis=1)


def _ep_rope_q(acc, extra, outs):
    outs[0][...] = (_rope_tile(acc, extra[0][...], extra[1][...]) * ATT_SCALE).astype(outs[0].dtype)


def _ep_rope_k(acc, extra, outs):
    outs[0][...] = _rope_tile(acc, extra[0][...], extra[1][...]).astype(outs[0].dtype)


def _ep_plain(acc, extra, outs):
    outs[0][...] = acc.astype(outs[0].dtype)


def _ep_sigmoid(acc, extra, outs):
    outs[0][...] = _sigmoid(acc).astype(outs[0].dtype)


def _ep_residual(acc, extra, outs):
    outs[0][...] = (extra[0][...] + acc).astype(outs[0].dtype)


def _attn_prompt_kernel(q_ref, kp_ref, kc_ref, vp_ref, vc_ref, o_ref, l_ref, *, tq):
    first_neg = jnp.where(pl.program_id(2) == 0, NEG, 0.0)
    qi = lax.broadcasted_iota(jnp.int32, (BACK, BACK), 0)
    kj = lax.broadcasted_iota(jnp.int32, (BACK, BACK), 1)
    prev_band = kj >= qi
    cur_band = kj <= qi
    for h in range(KV_HEADS):
        hs = slice(h * HEAD_DIM, (h + 1) * HEAD_DIM)
        for jb in range(tq // BACK):
            rows = slice(jb * BACK, (jb + 1) * BACK)
            qb = q_ref[0, rows, hs]
            if jb == 0:
                k_prev, v_prev = kp_ref[0, :, hs], vp_ref[0, :, hs]
                prev_neg = first_neg
            else:
                prow = slice((jb - 1) * BACK, jb * BACK)
                k_prev, v_prev = kc_ref[0, prow, hs], vc_ref[0, prow, hs]
                prev_neg = 0.0
            k_cur, v_cur = kc_ref[0, rows, hs], vc_ref[0, rows, hs]
            s_p = jnp.where(prev_band, _dot_nt(qb, k_prev), NEG) + prev_neg
            s_c = jnp.where(cur_band, _dot_nt(qb, k_cur), NEG)
            m = jnp.maximum(jnp.max(s_p, axis=-1, keepdims=True), jnp.max(s_c, axis=-1, keepdims=True))
            e_p = jnp.exp(s_p - m)
            e_c = jnp.exp(s_c - m)
            den = jnp.sum(e_p, axis=-1, keepdims=True) + jnp.sum(e_c, axis=-1, keepdims=True)
            o = (_dot(e_p, v_prev) + _dot(e_c, v_cur)) / den
            o_ref[0, rows, hs] = o
            l_ref[0, rows, hs] = jnp.broadcast_to(m + jnp.log(den), (BACK, HEAD_DIM))


def _attn_prompt_group(q, k, v, g, dil):
    b, s, _ = q.shape
    sub = s // dil
    tq = min(sub, 512)
    nq = sub // tq
    qv = q.reshape(b, sub, dil * Q_W)
    kv_ = k.reshape(b, sub, dil * KV_W)
    vv = v.reshape(b, sub, dil * KV_W)
    per = tq // BACK
    prev_map = lambda bi, r, i: (bi, jnp.maximum(i * per - 1, 0), r)
    cur_map = lambda bi, r, i: (bi, i, r)
    o, lse = pl.pallas_call(
        functools.partial(_attn_prompt_kernel, tq=tq),
        grid=(b, dil, nq),
        in_specs=[pl.BlockSpec((1, tq, KV_W), lambda bi, r, i: (bi, i, r * N_GROUPS + g)),
                  pl.BlockSpec((1, BACK, KV_W), prev_map),
                  pl.BlockSpec((1, tq, KV_W), cur_map),
                  pl.BlockSpec((1, BACK, KV_W), prev_map),
                  pl.BlockSpec((1, tq, KV_W), cur_map)],
        out_specs=[pl.BlockSpec((1, tq, KV_W), cur_map), pl.BlockSpec((1, tq, KV_W), cur_map)],
        out_shape=[jax.ShapeDtypeStruct((b, sub, dil * KV_W), F32)] * 2,
        compiler_params=_cparams("parallel", "parallel", "arbitrary"),
        name=f"attn_prompt_g{g}",
    )(qv, kv_, kv_, vv, vv)
    return o.reshape(b, s, KV_W), lse.reshape(b, s, KV_W)


def _attn_combine_kernel(o0, l0, o1, l1, o2, l2, out_ref):
    la, lb, lc = l0[...], l1[...], l2[...]
    m = jnp.maximum(jnp.maximum(la, lb), lc)
    wa, wb, wc = jnp.exp(la - m), jnp.exp(lb - m), jnp.exp(lc - m)
    out = (wa * o0[...] + wb * o1[...] + wc * o2[...]) / (wa + wb + wc)
    out_ref[...] = out.astype(out_ref.dtype)


def _attn_combine(parts, tm=512):
    m, w = parts[0].shape
    spec = pl.BlockSpec((tm, w), lambda i: (i, 0))
    return pl.pallas_call(
        _attn_combine_kernel,
        grid=(m // tm,),
        in_specs=[spec] * 6,
        out_specs=spec,
        out_shape=jax.ShapeDtypeStruct((m, w), BF16),
        compiler_params=_cparams("parallel"),
        name="attn_combine",
    )(*parts)


def _attn_sample_kernel(q_ref, kn_ref, vn_ref, k0_ref, v0_ref, k1_ref, v1_ref, k2_ref, v2_ref, o_ref, *, n_new):
    tok = lax.broadcasted_iota(jnp.int32, (BACK, KV_HEADS, 1), 0)
    kc_refs = (k0_ref, k1_ref, k2_ref)
    vc_refs = (v0_ref, v1_ref, v2_ref)
    for s in range(n_new):
        outs, lses = [], []
        for g in range(N_GROUPS):
            qh = q_ref[0, s, g * KV_HEADS:(g + 1) * KV_HEADS, :][None]
            if g == 0:
                kc, vc = kc_refs[g][0], vc_refs[g][0]
                new = slice(0, s + 1)
            else:
                kc, vc = kc_refs[g][0, :, s], vc_refs[g][0, :, s]
                new = slice(s, s + 1)
            sc = jnp.sum(kc * qh, axis=-1, keepdims=True)
            if g == 0:
                sc = jnp.where(tok >= s, sc, NEG)
            sn = jnp.sum(kn_ref[0, new] * qh, axis=-1, keepdims=True)
            m = jnp.maximum(jnp.max(sc, axis=0, keepdims=True), jnp.max(sn, axis=0, keepdims=True))
            p = jnp.exp(sc - m)
            pn = jnp.exp(sn - m)
            den = jnp.sum(p, axis=0, keepdims=True) + jnp.sum(pn, axis=0, keepdims=True)
            o = (jnp.sum(p * vc, axis=0, keepdims=True)
                 + jnp.sum(pn * vn_ref[0, new], axis=0, keepdims=True)) / den
            outs.append(o)
            lses.append(m + jnp.log(den))
        mm = jnp.maximum(jnp.maximum(lses[0], lses[1]), lses[2])
        ws = [jnp.exp(l - mm) for l in lses]
        comb = (ws[0] * outs[0] + ws[1] * outs[1] + ws[2] * outs[2]) / (ws[0] + ws[1] + ws[2])
        o_ref[0, s] = comb[0].astype(o_ref.dtype)


def _attn_sample(q, k_new, v_new, cache_k, cache_v):
    db, n_new = q.shape[0], q.shape[1]
    w_buf = cache_k.shape[1]
    assert w_buf == BACK * DILATIONS[-1] and n_new <= DILATIONS[1]
    new_spec = pl.BlockSpec((1, n_new, KV_HEADS, HEAD_DIM), lambda b: (b, 0, 0, 0))
    specs = [pl.BlockSpec((1, n_new, N_GROUPS * KV_HEADS, HEAD_DIM), lambda b: (b, 0, 0, 0)), new_spec, new_spec]
    args = [q, k_new, v_new]
    for dil in DILATIONS:
        sub = w_buf // dil
        last = sub // BACK - 1
        if dil == 1:
            shape = (db, sub, KV_HEADS, HEAD_DIM)
            spec = pl.BlockSpec((1, BACK, KV_HEADS, HEAD_DIM), functools.partial(lambda b, l: (b, l, 0, 0), l=last))
        else:
            shape = (db, sub, dil, KV_HEADS, HEAD_DIM)
            spec = pl.BlockSpec((1, BACK, n_new, KV_HEADS, HEAD_DIM),
                                functools.partial(lambda b, l: (b, l, 0, 0, 0), l=last))
        for c in (cache_k, cache_v):
            specs.append(spec)
            args.append(c.reshape(shape))
    return pl.pallas_call(
        functools.partial(_attn_sample_kernel, n_new=n_new),
        grid=(db,),
        in_specs=specs,
        out_specs=new_spec,
        out_shape=jax.ShapeDtypeStruct((db, n_new, KV_HEADS, HEAD_DIM), F32),
        compiler_params=_cparams("parallel"),
        name="attn_sample",
    )(*args)


def _head_indicator(width):
    l = lax.broadcasted_iota(jnp.int32, (width, LANE), 0)
    h = lax.broadcasted_iota(jnp.int32, (width, LANE), 1)
    return (_div_pow2(l, RW_N) == h).astype(F32)


def _head_indicator_t(width):
    h = lax.broadcasted_iota(jnp.int32, (LANE, width), 0)
    l = lax.broadcasted_iota(jnp.int32, (LANE, width), 1)
    return (_div_pow2(l, RW_N) == h).astype(F32)


def _rw_prep_kernel(ur, uk, uv, ul, fpr, fpk, fpv, fpl, fsr, fsk, fsv, fsl, mur, muk, muv, mul, ww, wa, wg, w0, a0,
                    kk_, ka, r_o, lw_o, k_o, v_o, kk_o, b_o, g_o, *, n_prompt_tiles, ds):
    is_sample = pl.program_id(0) >= n_prompt_tiles

    def mixed(u_ref, fp_ref, fs_ref, mu_ref):
        u = u_ref[...]
        row = lax.broadcasted_iota(jnp.int32, u.shape, 0)
        rolled = pltpu.roll(u, 1, axis=0)
        p_prompt = jnp.where(row == 0, fp_ref[0], rolled)
        p_sample = jnp.where(_mod_pow2(row, ds) == 0, fs_ref[...], rolled)
        prev = jnp.where(is_sample, p_sample, p_prompt)
        return u + (prev - u) * mu_ref[...]

    xr = mixed(ur, fpr, fsr, mur)
    xk = mixed(uk, fpk, fsk, muk)
    xv = mixed(uv, fpv, fsv, muv)
    xl = mixed(ul, fpl, fsl, mul)
    col = lax.broadcasted_iota(jnp.int32, xl.shape, 1)
    act = jnp.where(col < LORA_W, jnp.tanh(xl),
                    jnp.where(col < LORA_W + LORA_A, xl,
                              jnp.where(col < LORA_ALL, _sigmoid(xl), 0.0)))
    zw = _dot(act, ww[...])
    za = _dot(act, wa[...])
    zg = _dot(act, wg[...])
    lw = -RW_DECAY_SCALE * _sigmoid(w0[...] + zw)
    a = _sigmoid(a0[...] + za)
    kk = xk * kk_[...]
    width = kk.shape[1]
    ss = _dot_sel(kk * kk, _head_indicator(width), 2)
    inv = 1.0 / jnp.maximum(jnp.sqrt(ss), 1e-12)
    kkn = kk * _dot_sel(inv, _head_indicator_t(width), 2)
    r_o[...] = xr
    lw_o[...] = lw
    k_o[...] = xk * (1.0 + (a - 1.0) * ka[...])
    v_o[...] = xv
    kk_o[...] = kkn
    b_o[...] = kkn * a
    g_o[...] = zg


def _rw_prep(u_rkv, u_lora, shift, mp, sp, ds, mu, w_w, w_a, w_g, w0, a0, k_k, k_a, tm=512, tw=512):
    m = u_rkv.shape[0]
    nj = RW_W // tw
    n_p = mp // tm
    assert sp % tm == 0 and mp % tm == 0 and (m - mp) % tm == 0 and tm % ds == 0 and ds & (ds - 1) == 0
    mu_rkv = mu[:3 * RW_W].reshape(1, 3 * RW_W)
    lpad = LORA_PAD - LORA_ALL
    mu_l = jnp.pad(mu[3 * RW_W:], (0, lpad)).reshape(1, LORA_PAD)

    starts = jnp.arange(n_p, dtype=jnp.int32) * tm
    inside = ((starts % sp) != 0)[:, None]
    prev_idx = jnp.maximum(starts - 1, 0)

    def prompt_fix(u):
        return jnp.where(inside, jnp.take(u, prev_idx, axis=0), 0.0)[:, None, :]

    def sample_fix(first_rows):
        db, w = first_rows.shape
        return jnp.zeros((db, ds, w), F32).at[:, 0].set(first_rows).reshape(db * ds, w)

    fp_rkv, fp_l = prompt_fix(u_rkv), prompt_fix(u_lora)
    fs_rkv = sample_fix(shift[:, :3 * RW_W])
    fs_l = sample_fix(jnp.pad(shift[:, 3 * RW_W:], ((0, 0), (0, lpad))))

    def feat(off):
        return pl.BlockSpec((tm, tw), functools.partial(lambda i, j, o: (i, j + o), o=off))

    def fixp(off):
        return pl.BlockSpec((1, 1, tw), functools.partial(lambda i, j, o: (jnp.minimum(i, n_p - 1), 0, j + o), o=off))

    def fixs(off):
        return pl.BlockSpec((tm, tw), functools.partial(lambda i, j, o: (jnp.maximum(i - n_p, 0), j + o), o=off))

    def vec(off):
        return pl.BlockSpec((1, tw), functools.partial(lambda i, j, o: (0, j + o), o=off))

    lspec = pl.BlockSpec((tm, LORA_PAD), lambda i, j: (i, 0))
    lfixp = pl.BlockSpec((1, 1, LORA_PAD), lambda i, j: (jnp.minimum(i, n_p - 1), 0, 0))
    lfixs = pl.BlockSpec((tm, LORA_PAD), lambda i, j: (jnp.maximum(i - n_p, 0), 0))
    wspec = pl.BlockSpec((LORA_PAD, tw), lambda i, j: (0, j))
    in_specs = [feat(0), feat(nj), feat(2 * nj), lspec,
                fixp(0), fixp(nj), fixp(2 * nj), lfixp,
                fixs(0), fixs(nj), fixs(2 * nj), lfixs,
                vec(0), vec(nj), vec(2 * nj), pl.BlockSpec((1, LORA_PAD), lambda i, j: (0, 0)),
                wspec, wspec, wspec, vec(0), vec(0), vec(0), vec(0)]
    out_spec = pl.BlockSpec((tm, tw), lambda i, j: (i, j))
    outs = pl.pallas_call(
        functools.partial(_rw_prep_kernel, n_prompt_tiles=n_p, ds=ds),
        grid=(m // tm, nj),
        in_specs=in_specs,
        out_specs=[out_spec] * 7,
        out_shape=[jax.ShapeDtypeStruct((m, RW_W), F32)] * 7,
        compiler_params=_cparams("parallel", "arbitrary"),
        name="rw_prep",
    )(u_rkv, u_rkv, u_rkv, u_lora, fp_rkv, fp_rkv, fp_rkv, fp_l, fs_rkv, fs_rkv, fs_rkv, fs_l,
      mu_rkv, mu_rkv, mu_rkv, mu_l,
      w_w, w_a, w_g, w0.reshape(1, RW_W), a0.reshape(1, RW_W), k_k.reshape(1, RW_W), k_a.reshape(1, RW_W))
    return outs


def _rw_scan_kernel(r_ref, lw_ref, k_ref, v_ref, kk_ref, b_ref, s_in, y_ref, s_out, s_bd, *, chunk):
    t_id = pl.program_id(1)
    n_t = pl.num_programs(1)
    gw = RW_GROUP * RW_N
    n_groups = RW_W // gw
    rows = RW_GROUP * chunk

    ri = lax.broadcasted_iota(jnp.int32, (gw, gw), 0)
    ci = lax.broadcasted_iota(jnp.int32, (gw, gw), 1)
    state_mask = _div_pow2(ri, RW_N) == _div_pow2(ci, RW_N)

    @pl.when(t_id == 0)
    def _():
        kr = lax.broadcasted_iota(jnp.int32, (RW_N, gw), 0)
        kc = lax.broadcasted_iota(jnp.int32, (RW_N, gw), 1)
        spread = (_mod_pow2(kc, RW_N) == kr).astype(F32)
        for g in range(n_groups):
            tiled = _dot_sel(s_in[0, g * gw:(g + 1) * gw, :], spread, 3)
            s_bd[g] = jnp.where(state_mask, tiled, 0.0)

    ti = lax.broadcasted_iota(jnp.int32, (chunk, chunk), 0)
    tj = lax.broadcasted_iota(jnp.int32, (chunk, chunk), 1)
    lw = lw_ref[0]
    cum = _sel_dot((tj <= ti).astype(F32), lw, 3)
    e_pos = jnp.exp(cum)
    e_neg = jnp.exp(-cum)
    e_prev = jnp.exp(cum - lw)
    cum_last = cum[chunk - 1:chunk, :]
    e_rem = jnp.exp(cum_last - cum)
    c_all = jnp.exp(cum_last)
    kt = kk_ref[0] * e_prev
    bt = b_ref[0] * e_neg
    k2t = k_ref[0] * e_neg
    rt = r_ref[0] * e_pos
    btc = b_ref[0] * e_rem
    k2tc = k_ref[0] * e_rem
    vv = v_ref[0]

    sr = lax.broadcasted_iota(jnp.int32, (rows, gw), 0)
    sc = lax.broadcasted_iota(jnp.int32, (rows, gw), 1)
    stack_mask = _div_pow2(sr, chunk) == _div_pow2(sc, RW_N)
    ar = lax.broadcasted_iota(jnp.int32, (rows, rows), 0)
    ac = lax.broadcasted_iota(jnp.int32, (rows, rows), 1)
    strict = ar > ac
    incl = ar >= ac
    eye = (ar == ac).astype(F32)

    def stack(x):
        return jnp.where(stack_mask, jnp.concatenate([x] * RW_GROUP, axis=0), 0.0).astype(BF16)

    for g in range(n_groups):
        ls = slice(g * gw, (g + 1) * gw)
        kt_s, rt_s = stack(kt[:, ls]), stack(rt[:, ls])
        bt_s, k2t_s = stack(bt[:, ls]), stack(k2t[:, ls])
        v_s = stack(vv[:, ls])
        s0 = s_bd[g]
        a_b = jnp.where(strict, _dot_nt(kt_s, bt_s), 0.0)
        a_k = jnp.where(strict, _dot_nt(kt_s, k2t_s), 0.0)
        r_b = jnp.where(incl, _dot_nt(rt_s, bt_s), 0.0)
        r_k = jnp.where(incl, _dot_nt(rt_s, k2t_s), 0.0)
        rhs = _dot_nt(kt_s, s0) + _dot(a_k, v_s)
        inv = eye - a_b
        pw = a_b
        n = 2
        while n < chunk:
            pw = _dot(pw, pw)
            inv = inv + _dot(inv, pw)
            n *= 2
        u = -_dot(inv, rhs)
        y_bd = _dot_nt(rt_s, s0) + _dot(r_b, u) + _dot(r_k, v_s)
        y = y_bd[0:chunk]
        for h in range(1, RW_GROUP):
            y = y + y_bd[h * chunk:(h + 1) * chunk]
        y_ref[0, :, ls] = y
        s_bd[g] = s0 * c_all[:, ls] + _dot_tn(u, stack(btc[:, ls])) + _dot_tn(v_s, stack(k2tc[:, ls]))

    @pl.when(t_id == n_t - 1)
    def _():
        gr = lax.broadcasted_iota(jnp.int32, (gw, RW_N), 0)
        gc = lax.broadcasted_iota(jnp.int32, (gw, RW_N), 1)
        gather = (_mod_pow2(gr, RW_N) == gc).astype(F32)
        for g in range(n_groups):
            s_out[0, g * gw:(g + 1) * gw, :] = _dot_sel(s_bd[g], gather, 3)


def _rw_scan(r, lw, k, v, kk, b, state, chunk, bn, t):
    gw = RW_GROUP * RW_N
    nt = t // chunk
    seq = pl.BlockSpec((1, chunk, RW_W), lambda bi, ti: (0, bi * nt + ti, 0))
    st = pl.BlockSpec((1, RW_W, RW_N), lambda bi, ti: (bi, 0, 0))
    y, s_new = pl.pallas_call(
        functools.partial(_rw_scan_kernel, chunk=chunk),
        grid=(bn, nt),
        in_specs=[seq] * 6 + [st],
        out_specs=[seq, st],
        out_shape=[jax.ShapeDtypeStruct((1, bn * t, RW_W), F32), jax.ShapeDtypeStruct((bn, RW_W, RW_N), F32)],
        scratch_shapes=[pltpu.VMEM((RW_W // gw, gw, gw), F32)],
        compiler_params=_cparams("parallel", "arbitrary"),
        name=f"rw_scan_c{chunk}",
    )(*(a[None] for a in (r, lw, k, v, kk, b)), state.reshape(bn, RW_W, RW_N))
    return y[0], s_new.reshape(bn, RW_HEADS, RW_N, RW_N)


def _rw_post_kernel(y_ref, r_ref, k_ref, v_ref, g_ref, lnw, lnb, rk, o_ref):
    y = y_ref[...]
    width = y.shape[1]
    ind, ind_t = _head_indicator(width), _head_indicator_t(width)
    mean = _dot_sel(_dot_sel(y, ind, 2) * (1.0 / RW_N), ind_t, 2)
    d = y - mean
    var_h = _dot_sel(d * d, ind, 2) * (1.0 / RW_N)
    rstd = _dot_sel(lax.rsqrt(var_h + RW_GN_EPS), ind_t, 2)
    yn = d * rstd * lnw[...] + lnb[...]
    bonus = _dot_sel(_dot_sel(r_ref[...] * k_ref[...] * rk[...], ind, 2), ind_t, 2) * v_ref[...]
    o_ref[...] = ((yn + bonus) * g_ref[...]).astype(o_ref.dtype)


def _rw_post(y, r, k, v, g, ln_w, ln_b, r_k, tm=512, tw=512):
    m = y.shape[0]
    feat = pl.BlockSpec((tm, tw), lambda i, j: (i, j))
    vec = pl.BlockSpec((1, tw), lambda i, j: (0, j))
    return pl.pallas_call(
        _rw_post_kernel,
        grid=(m // tm, RW_W // tw),
        in_specs=[feat] * 5 + [vec] * 3,
        out_specs=feat,
        out_shape=jax.ShapeDtypeStruct((m, RW_W), BF16),
        compiler_params=_cparams("parallel", "parallel"),
        name="rw_post",
    )(y, r, k, v, g, ln_w.reshape(1, RW_W), ln_b.reshape(1, RW_W), r_k.reshape(1, RW_W))


def _merge_kernel(oa_ref, ob_ref, wa_ref, wb_ref, ga_ref, gb_ref, out_ref, wa_bf, wb_bf):
    @pl.when(pl.program_id(1) == 0)
    def _():
        wa_bf[...] = wa_ref[...].astype(BF16)
        wb_bf[...] = wb_ref[...].astype(BF16)

    ya = jnp.dot(oa_ref[...], wa_bf[...], preferred_element_type=F32)
    yb = jnp.dot(ob_ref[...], wb_bf[...], preferred_element_type=F32)
    out_ref[...] = (ga_ref[...].astype(F32) * ya + gb_ref[...].astype(F32) * yb).astype(out_ref.dtype)


def _merge(o_a, o_b, w_a, w_b, gates, tm):
    m = o_a.shape[0]
    tn = MM_TN
    nj = D_MODEL // tn
    return pl.pallas_call(
        _merge_kernel,
        grid=(nj, m // tm),
        in_specs=[pl.BlockSpec((tm, KV_W), lambda j, i: (i, 0)),
                  pl.BlockSpec((tm, RW_W), lambda j, i: (i, 0)),
                  pl.BlockSpec((KV_W, tn), lambda j, i: (0, j)),
                  pl.BlockSpec((RW_W, tn), lambda j, i: (0, j)),
                  pl.BlockSpec((tm, tn), lambda j, i: (i, j)),
                  pl.BlockSpec((tm, tn), lambda j, i: (i, j + nj))],
        out_specs=pl.BlockSpec((tm, tn), lambda j, i: (i, j)),
        out_shape=jax.ShapeDtypeStruct((m, D_MODEL), BF16),
        scratch_shapes=[pltpu.VMEM((KV_W, tn), BF16), pltpu.VMEM((RW_W, tn), BF16)],
        compiler_params=_cparams("parallel", "arbitrary"),
        name="merge",
    )(o_a, o_b, w_a, w_b, gates, gates)


def _router_kernel(x_ref, g_ref, wr_ref, br_ref, h_ref, id_ref, wt_ref):
    x = x_ref[...]
    ms = jnp.mean(x * x, axis=-1, keepdims=True)
    h = x * lax.rsqrt(ms + NORM_EPS) * g_ref[...]
    h_ref[...] = h.astype(h_ref.dtype)
    logits = _dot_hi(h, wr_ref[...]) + br_ref[...]
    lane = lax.broadcasted_iota(jnp.int32, logits.shape, 1)
    lane_f = lane.astype(F32)
    gmask = lane < MOE_GROUPS
    gl = jnp.where(gmask, logits, NEG)
    gm = jnp.max(gl, axis=-1, keepdims=True)
    gi = jnp.min(jnp.where(gl == gm, lane_f, float(LANE)), axis=-1, keepdims=True)
    g_prob = 1.0 / jnp.sum(jnp.where(gmask, jnp.exp(gl - gm), 0.0), axis=-1, keepdims=True)
    lo = MOE_GROUPS + gi * MOE_PER_GROUP
    emask = jnp.logical_and(lane_f >= lo, lane_f < lo + MOE_PER_GROUP)
    el = jnp.where(emask, logits, NEG)
    m1 = jnp.max(el, axis=-1, keepdims=True)
    i1 = jnp.min(jnp.where(el == m1, lane_f, float(LANE)), axis=-1, keepdims=True)
    el2 = jnp.where(lane_f == i1, NEG, el)
    m2 = jnp.max(el2, axis=-1, keepdims=True)
    i2 = jnp.min(jnp.where(el2 == m2, lane_f, float(LANE)), axis=-1, keepdims=True)
    t = jnp.exp(m2 - m1)
    w1 = g_prob / (1.0 + t)
    w2 = g_prob * t / (1.0 + t)
    ids = jnp.where(lane == 0, i1 - MOE_GROUPS, jnp.where(lane == 1, i2 - MOE_GROUPS, 0.0))
    id_ref[...] = ids.astype(jnp.int32)
    wt_ref[...] = jnp.where(lane == 0, w1, jnp.where(lane == 1, w2, 0.0))


def _router(x, g, w_router, b_router, tm=256):
    m, d = x.shape
    row = pl.BlockSpec((tm, d), lambda i: (i, 0))
    small = pl.BlockSpec((tm, LANE), lambda i: (i, 0))
    return pl.pallas_call(
        _router_kernel,
        grid=(m // tm,),
        in_specs=[row, pl.BlockSpec((1, d), lambda i: (0, 0)),
                  pl.BlockSpec((d, LANE), lambda i: (0, 0)), pl.BlockSpec((1, LANE), lambda i: (0, 0))],
        out_specs=[row, small, small],
        out_shape=[jax.ShapeDtypeStruct((m, d), F32), jax.ShapeDtypeStruct((m, LANE), jnp.int32),
                   jax.ShapeDtypeStruct((m, LANE), F32)],
        compiler_params=_cparams("parallel"),
        name="ffn_norm_router",
    )(x, g.reshape(1, d), w_router, b_router)


def _moe_kernel(te_ref, tv_ref, last_ref, x_ref, wg_ref, wu_ref, wd_ref, rw_ref, o_ref):
    i = pl.program_id(0)
    j = pl.program_id(1)
    valid = tv_ref[i] > 0

    @pl.when(j == 0)
    def _():
        o_ref[...] = jnp.zeros_like(o_ref)

    @pl.when(valid)
    def _():
        x = x_ref[...]
        gate = jnp.dot(x, wg_ref[0].astype(BF16), preferred_element_type=F32)
        up = jnp.dot(x, wu_ref[0].astype(BF16), preferred_element_type=F32)
        hidden = (gate * _sigmoid(gate) * up).astype(BF16)
        o_ref[...] += jnp.dot(hidden, wd_ref[0].astype(BF16), preferred_element_type=F32)

    @pl.when(jnp.logical_and(valid, j == pl.num_programs(1) - 1))
    def _():
        o_ref[...] = o_ref[...] * rw_ref[...]


def _moe_experts(xs, row_w, tile_expert, tile_valid, last_tile, w_gate, w_up, w_down):
    p, d = xs.shape
    n_tiles = p // MOE_TM
    nf = EXPERT_FF // MOE_TF

    def f_idx(i, j, tv):
        return jnp.where(tv[i] > 0, j, nf - 1)

    def rows(i, j, te, tv, last):
        return (jnp.minimum(i, last[0]), 0)

    grid_spec = pltpu.PrefetchScalarGridSpec(
        num_scalar_prefetch=3,
        grid=(n_tiles, nf),
        in_specs=[pl.BlockSpec((MOE_TM, d), rows),
                  pl.BlockSpec((1, d, MOE_TF), lambda i, j, te, tv, last: (te[i], 0, f_idx(i, j, tv))),
                  pl.BlockSpec((1, d, MOE_TF), lambda i, j, te, tv, last: (te[i], 0, f_idx(i, j, tv))),
                  pl.BlockSpec((1, MOE_TF, d), lambda i, j, te, tv, last: (te[i], f_idx(i, j, tv), 0)),
                  pl.BlockSpec((MOE_TM, 1), rows)],
        out_specs=pl.BlockSpec((MOE_TM, d), lambda i, j, te, tv, last: (i, 0)),
    )
    return pl.pallas_call(
        _moe_kernel,
        grid_spec=grid_spec,
        out_shape=jax.ShapeDtypeStruct((p, d), F32),
        compiler_params=_cparams("arbitrary", "arbitrary"),
        name="moe_experts",
    )(tile_expert, tile_valid, last_tile, xs, w_gate, w_up, w_down, row_w)


def _row_copy(src_hbm, row, dst_vmem, slot, sem):
    return pltpu.make_async_copy(src_hbm.at[pl.ds(row, 1)], dst_vmem.at[pl.ds(slot, 1)], sem)


def _dispatch_kernel(tok_ref, nv_ref, h_hbm, o_ref, buf, sem):
    i = pl.program_id(0)
    n = nv_ref[i]

    @pl.when(i == 0)
    def _():
        buf[...] = jnp.zeros_like(buf)

    def issue(r, c):
        _row_copy(h_hbm, tok_ref[i * MOE_TM + r], buf, r, sem).start()
        return c

    def wait(r, c):
        _row_copy(h_hbm, 0, buf, r, sem).wait()
        return c

    lax.fori_loop(0, n, issue, 0)
    lax.fori_loop(0, n, wait, 0)
    o_ref[...] = buf[...].astype(o_ref.dtype)


def _dispatch_rows(h, slot_tok, tile_rows):
    d = h.shape[1]
    p = slot_tok.shape[0]
    grid_spec = pltpu.PrefetchScalarGridSpec(
        num_scalar_prefetch=2,
        grid=(p // MOE_TM,),
        in_specs=[pl.BlockSpec(memory_space=pl.ANY)],
        out_specs=pl.BlockSpec((MOE_TM, d), lambda i, tok, nv: (i, 0)),
        scratch_shapes=[pltpu.VMEM((MOE_TM, d), F32), pltpu.SemaphoreType.DMA(())],
    )
    return pl.pallas_call(
        _dispatch_kernel,
        grid_spec=grid_spec,
        out_shape=jax.ShapeDtypeStruct((p, d), BF16),
        compiler_params=_cparams("arbitrary"),
        name="moe_dispatch",
    )(slot_tok, tile_rows, h)


def _dispatch_plan(ids, wts):
    t = ids.shape[0]
    a = t * 2
    flat_e = ids.reshape(a)
    onehot = (flat_e[:, None] == jnp.arange(N_EXPERTS, dtype=jnp.int32)[None, :]).astype(jnp.int32)
    csum = jnp.cumsum(onehot, axis=0)
    counts = csum[-1]
    rank = jnp.take_along_axis(csum, flat_e[:, None], axis=1)[:, 0] - 1
    tiles_per = (counts + MOE_TM - 1) // MOE_TM
    tile_end = jnp.cumsum(tiles_per)
    tile_start = tile_end - tiles_per
    dest = tile_start[flat_e] * MOE_TM + rank
    n_tiles = -(-a // MOE_TM) + N_EXPERTS
    total = tile_end[-1]
    slot_tok = jnp.zeros((n_tiles * MOE_TM,), jnp.int32).at[dest].set(jnp.arange(a, dtype=jnp.int32) // 2)
    slot_w = jnp.zeros((n_tiles * MOE_TM,), F32).at[dest].set(wts.reshape(a))
    tile_ids = jnp.arange(n_tiles, dtype=jnp.int32)
    tile_valid = (tile_ids < total).astype(jnp.int32)
    tile_expert = jnp.searchsorted(tile_end, jnp.minimum(tile_ids, total - 1), side="right").astype(jnp.int32)
    tile_expert = jnp.minimum(tile_expert, N_EXPERTS - 1)
    tile_pos = tile_ids - tile_start[tile_expert]
    tile_rows = jnp.where(tile_valid > 0, jnp.clip(counts[tile_expert] - tile_pos * MOE_TM, 0, MOE_TM), 0)
    last_tile = (total - 1).astype(jnp.int32).reshape(1)
    return slot_tok, slot_w, dest, tile_expert, tile_valid, tile_rows.astype(jnp.int32), last_tile


def _final_kernel(dest_ref, x_ref, y_hbm, g_ref, o_ref, buf, sem, *, row_off, tm):
    base = (row_off + pl.program_id(0) * tm) * 2

    def issue(r, c):
        _row_copy(y_hbm, dest_ref[base + r], buf, (r & 1) * tm + (r >> 1), sem).start()
        return c

    def wait(r, c):
        _row_copy(y_hbm, 0, buf, r, sem).wait()
        return c

    lax.fori_loop(0, 2 * tm, issue, 0)
    lax.fori_loop(0, 2 * tm, wait, 0)
    x = x_ref[...] + (buf[0:tm, :] + buf[tm:2 * tm, :])
    ms = jnp.mean(x * x, axis=-1, keepdims=True)
    o_ref[...] = x * lax.rsqrt(ms + NORM_EPS) * g_ref[...]


def _final(x, y_rows, dest, g, row_off, rows, tm=128):
    d = x.shape[1]
    off = row_off // tm
    grid_spec = pltpu.PrefetchScalarGridSpec(
        num_scalar_prefetch=1,
        grid=(rows // tm,),
        in_specs=[pl.BlockSpec((tm, d), lambda i, dst: (i + off, 0)),
                  pl.BlockSpec(memory_space=pl.ANY),
                  pl.BlockSpec((1, d), lambda i, dst: (0, 0))],
        out_specs=pl.BlockSpec((tm, d), lambda i, dst: (i, 0)),
        scratch_shapes=[pltpu.VMEM((2 * tm, d), F32), pltpu.SemaphoreType.DMA(())],
    )
    return pl.pallas_call(
        functools.partial(_final_kernel, row_off=row_off, tm=tm),
        grid_spec=grid_spec,
        out_shape=jax.ShapeDtypeStruct((rows, d), F32),
        compiler_params=_cparams("arbitrary"),
        name="final_norm",
    )(dest, x, y_rows, g.reshape(1, d))


def _rope_tables(pos):
    half = HEAD_DIM // 2
    inv_freq = ROPE_THETA ** (-jnp.arange(half, dtype=F32) / half)
    ang = pos.astype(F32)[:, None] * inv_freq[None, :]
    cos, sin = jnp.cos(ang), jnp.sin(ang)
    return jnp.concatenate([cos, cos], axis=-1), jnp.concatenate([-sin, sin], axis=-1)


def kernel(x_prompt, x_sample, cache_k, cache_v, state_shift, state_wkv, norm_mix_g, w_in, rw_mu, rw_w0, rw_w2,
           rw_a0, rw_a2, rw_g2, rw_k_k, rw_k_a, rw_r_k, rw_ln_w, rw_ln_b, w_branch_a, w_branch_b, w_out,
           norm_ffn_g, router_group_w, router_group_b, router_expert_w, router_expert_b, exp_gate, exp_up,
           exp_down, norm_final_g):
    assert w_in.shape[0] == 1, "single-layer trunk"
    bp, sp, d = x_prompt.shape
    db, ds, _ = x_sample.shape
    mp, ms_ = bp * sp, db * ds
    m = mp + ms_
    past = cache_k.shape[2]
    tm_mm = m // 8

    x_all = jnp.concatenate([x_prompt.reshape(mp, d), x_sample.reshape(ms_, d)], axis=0)
    h = _rmsnorm(x_all, norm_mix_g[0], BF16)

    w_in2 = w_in.reshape(d, w_in.shape[2])
    cos_p, sin_p = _rope_tables(jnp.arange(sp, dtype=jnp.int32))
    cos_s, sin_s = _rope_tables(past + jnp.arange(ds, dtype=jnp.int32))
    cos = jnp.concatenate([jnp.tile(cos_p, (bp, 1)), jnp.tile(cos_s, (db, 1))], axis=0)
    sin = jnp.concatenate([jnp.tile(sin_p, (bp, 1)), jnp.tile(sin_s, (db, 1))], axis=0)
    rope_extras = [(cos, "row", 0), (sin, "row", 0)]
    (q,) = _matmul(h, w_in2, Q_W, 0, _ep_rope_q, rope_extras, [BF16], tm_mm, "inproj_q")
    (k,) = _matmul(h, w_in2, KV_W, COL_K // MM_TN, _ep_rope_k, rope_extras, [F32], tm_mm, "inproj_k")
    (v,) = _matmul(h, w_in2, KV_W, COL_V // MM_TN, _ep_plain, [], [F32], tm_mm, "inproj_v")
    (u_rkv,) = _matmul(h, w_in2, 3 * RW_W, COL_RW // MM_TN, _ep_plain, [], [F32], tm_mm, "inproj_rkv")
    (u_lora,) = _matmul(h, w_in2, LORA_PAD, COL_LORA // MM_TN, _ep_plain, [], [F32], tm_mm, "inproj_lora")
    w_gates = w_in2[:, COL_GATE:]
    (gates,) = _matmul(h, w_gates, 2 * d, 0, _ep_sigmoid, [], [BF16], tm_mm, "inproj_gates")

    qp = q[:mp].reshape(bp, sp, Q_W)
    kp = k[:mp].reshape(bp, sp, KV_W)
    vp = v[:mp].reshape(bp, sp, KV_W)
    parts = []
    for g, dil in enumerate(DILATIONS):
        o_g, l_g = _attn_prompt_group(qp, kp, vp, g, dil)
        parts += [o_g.reshape(mp, KV_W), l_g.reshape(mp, KV_W)]
    oa_p = _attn_combine(parts)
    ks = k[mp:].reshape(db, ds, KV_HEADS, HEAD_DIM)
    vs = v[mp:].reshape(db, ds, KV_HEADS, HEAD_DIM)
    oa_s = _attn_sample(q[mp:].astype(F32).reshape(db, ds, N_GROUPS * KV_HEADS, HEAD_DIM), ks, vs,
                        cache_k.reshape(db, past, KV_HEADS, HEAD_DIM), cache_v.reshape(db, past, KV_HEADS, HEAD_DIM))
    o_a = jnp.concatenate([oa_p, oa_s.reshape(ms_, KV_W).astype(BF16)], axis=0)

    zl = functools.partial(jnp.zeros, dtype=F32)
    w_w = zl((LORA_PAD, RW_W)).at[:LORA_W].set(rw_w2[0])
    w_a = zl((LORA_PAD, RW_W)).at[LORA_W:LORA_W + LORA_A].set(rw_a2[0])
    w_g = zl((LORA_PAD, RW_W)).at[LORA_W + LORA_A:LORA_ALL].set(rw_g2[0])
    r_, lw_, k2_, v_, kk_, b_, g_ = _rw_prep(u_rkv, u_lora, state_shift[0], mp, sp, ds, rw_mu[0], w_w, w_a, w_g,
                                             rw_w0[0], rw_a0[0], rw_k_k[0], rw_k_a[0])

    scan_in = (r_, lw_, k2_, v_, kk_, b_)
    y_p, wkv_p = _rw_scan(*scan_in, jnp.zeros((bp, RW_HEADS, RW_N, RW_N), F32), RW_CHUNK, bp, sp)
    ts = 8
    samp = [jnp.pad(a[mp:].reshape(db, ds, RW_W), ((0, 0), (0, ts - ds), (0, 0))).reshape(db * ts, RW_W)
            for a in scan_in]
    y_s, wkv_s = _rw_scan(*samp, state_wkv[0], ts, db, ts)
    y_rw = jnp.concatenate([y_p, y_s.reshape(db, ts, RW_W)[:, :ds].reshape(ms_, RW_W)], axis=0)
    o_b = _rw_post(y_rw, r_, k2_, v_, g_, rw_ln_w[0], rw_ln_b[0], rw_r_k[0])

    merged = _merge(o_a, o_b, w_branch_a[0], w_branch_b[0], gates, tm_mm)
    (x1,) = _matmul(merged, w_out.reshape(d, d), d, 0, _ep_residual, [(x_all, "tile", 0)], [F32], tm_mm, "out_proj")

    w_router = jnp.concatenate([router_group_w[0], router_expert_w[0],
                                jnp.zeros((d, LANE - MOE_GROUPS - N_EXPERTS), F32)], axis=1)
    b_router = jnp.concatenate([router_group_b[0], router_expert_b[0],
                                jnp.zeros((LANE - MOE_GROUPS - N_EXPERTS,), F32)]).reshape(1, LANE)
    h2, ids, wts = _router(x1, norm_ffn_g[0], w_router, b_router)
    slot_tok, slot_w, dest, tile_expert, tile_valid, tile_rows, last_tile = _dispatch_plan(ids[:, :2], wts[:, :2])
    xs = _dispatch_rows(h2, slot_tok, tile_rows)
    yb = _moe_experts(xs, slot_w[:, None], tile_expert, tile_valid, last_tile,
                      exp_gate[0], exp_up[0], exp_down[0])
    y_prompt = _final(x1, yb, dest, norm_final_g, 0, mp).reshape(bp, sp, d)
    y_sample = _final(x1, yb, dest, norm_final_g, mp, ms_).reshape(db, ds, d)

    keep = min(BACK * DILATIONS[-1], sp)
    k_prompt = k[:mp].reshape(1, bp, sp, KV_HEADS, HEAD_DIM)[:, :, sp - keep:]
    v_prompt = v[:mp].reshape(1, bp, sp, KV_HEADS, HEAD_DIM)[:, :, sp - keep:]
    k_sample = ks.reshape(1, db, ds, KV_HEADS, HEAD_DIM)
    v_sample = vs.reshape(1, db, ds, KV_HEADS, HEAD_DIM)

    def last_rows(lo, nb, t):
        a = u_rkv[lo:lo + nb * t].reshape(nb, t, 3 * RW_W)[:, -1]
        b = u_lora[lo:lo + nb * t].reshape(nb, t, LORA_PAD)[:, -1, :LORA_ALL]
        return jnp.concatenate([a, b], axis=-1)[None]

    return (y_prompt, y_sample, k_prompt, v_prompt, k_sample, v_sample,
            last_rows(0, bp, sp), last_rows(mp, db, ds), wkv_p[None], wkv_s[None])
```

```python
import functools
import math

import jax
import jax.numpy as jnp
from jax import lax
from jax.experimental import pallas as pl
from jax.experimental.pallas import tpu as pltpu

F32 = jnp.float32
BF16 = jnp.bfloat16
HI = lax.Precision.HIGHEST

D_MODEL = 4096
NORM_EPS = 1e-6
HEAD_DIM = 128
KV_HEADS = 8
DILATIONS = (1, 4, 16)
BACK = 128
N_GROUPS = 3
ROPE_THETA = 10000.0
ATT_SCALE = HEAD_DIM ** -0.5
Q_W = N_GROUPS * KV_HEADS * HEAD_DIM
KV_W = KV_HEADS * HEAD_DIM
RW_N = 64
RW_HEADS = 32
RW_W = RW_HEADS * RW_N
LORA_W, LORA_A, LORA_G = 96, 96, 256
LORA_ALL = LORA_W + LORA_A + LORA_G
RW_FEAT = 3 * RW_W + LORA_ALL
RW_GN_EPS = 64e-5
RW_DECAY_SCALE = math.exp(-0.5)
COL_K = Q_W
COL_V = Q_W + KV_W
COL_RW = Q_W + 2 * KV_W
COL_LORA = COL_RW + 3 * RW_W
COL_GATE = COL_RW + RW_FEAT
MOE_GROUPS = 8
MOE_PER_GROUP = 8
N_EXPERTS = 64
EXPERT_FF = 1024

LANE = 128
VMEM_LIMIT_BYTES = 56 * 1024 * 1024
MM_TN = 512
LORA_PAD = 512
MOE_TM = 384
MOE_TF = 256
RW_CHUNK = 64
RW_GROUP = 4
NEG = -1e30


def _cparams(*sem):
    return pltpu.CompilerParams(dimension_semantics=sem, vmem_limit_bytes=VMEM_LIMIT_BYTES)


def _dot(a, b):
    return jnp.dot(a.astype(BF16), b.astype(BF16), preferred_element_type=F32)


def _dot_nt(a, b):
    return lax.dot_general(a.astype(BF16), b.astype(BF16), (((1,), (1,)), ((), ())),
                           preferred_element_type=F32)


def _dot_tn(a, b):
    return lax.dot_general(a.astype(BF16), b.astype(BF16), (((0,), (0,)), ((), ())),
                           preferred_element_type=F32)


def _dot_hi(a, b):
    return jnp.dot(a, b, preferred_element_type=F32, precision=HI)


def _bf16_terms(x, terms):
    parts = []
    for _ in range(terms):
        p = x.astype(BF16)
        parts.append(p)
        x = x - p.astype(F32)
    return parts


def _dot_sel(x, sel, terms):
    sel = sel.astype(BF16)
    return sum(jnp.dot(p, sel, preferred_element_type=F32) for p in _bf16_terms(x, terms))


def _sel_dot(sel, x, terms):
    sel = sel.astype(BF16)
    return sum(jnp.dot(sel, p, preferred_element_type=F32) for p in _bf16_terms(x, terms))


def _sigmoid(x):
    return 1.0 / (1.0 + jnp.exp(-x))


def _div_pow2(x, n):
    return x >> (n.bit_length() - 1)


def _mod_pow2(x, n):
    return x & (n - 1)


def _rmsnorm_kernel(x_ref, g_ref, o_ref):
    x = x_ref[...]
    ms = jnp.mean(x * x, axis=-1, keepdims=True)
    o_ref[...] = (x * lax.rsqrt(ms + NORM_EPS) * g_ref[...]).astype(o_ref.dtype)


def _rmsnorm(x, g, out_dtype, tm=512):
    m, d = x.shape
    return pl.pallas_call(
        _rmsnorm_kernel,
        grid=(m // tm,),
        in_specs=[pl.BlockSpec((tm, d), lambda i: (i, 0)), pl.BlockSpec((1, d), lambda i: (0, 0))],
        out_specs=pl.BlockSpec((tm, d), lambda i: (i, 0)),
        out_shape=jax.ShapeDtypeStruct((m, d), out_dtype),
        compiler_params=_cparams("parallel"),
        name="rmsnorm",
    )(x, g.reshape(1, d))


def _mm_kernel(a_ref, w_ref, *refs, n_extra, epilogue, w_transposed):
    extra = refs[:n_extra]
    outs = refs[n_extra:-1]
    wbf = refs[-1]

    @pl.when(pl.program_id(1) == 0)
    def _():
        w = w_ref[...]
        wbf[...] = (w.T if w_transposed else w).astype(BF16)

    acc = jnp.dot(a_ref[...], wbf[...], preferred_element_type=F32)
    epilogue(acc, extra, outs)


def _matmul(a, w, n_cols, col_off_tiles, epilogue, extras, out_dtypes, tm, name, w_transposed=False):
    m, k = a.shape
    tn = MM_TN
    if w_transposed:
        w_spec = pl.BlockSpec((tn, k), lambda j, i: (j + col_off_tiles, 0))
    else:
        w_spec = pl.BlockSpec((k, tn), lambda j, i: (0, j + col_off_tiles))
    in_specs = [pl.BlockSpec((tm, k), lambda j, i: (i, 0)), w_spec]
    args = [a, w]
    for arr, kind, off in extras:
        if kind == "row":
            in_specs.append(pl.BlockSpec((tm, arr.shape[1]), lambda j, i: (i, 0)))
        elif kind == "tile":
            in_specs.append(pl.BlockSpec((tm, tn), functools.partial(lambda j, i, o: (i, j + o), o=off)))
        else:
            in_specs.append(pl.BlockSpec((1, tn), functools.partial(lambda j, i, o: (0, j + o), o=off)))
        args.append(arr)
    out_specs = [pl.BlockSpec((tm, tn), lambda j, i: (i, j)) for _ in out_dtypes]
    out_shape = [jax.ShapeDtypeStruct((m, n_cols), dt) for dt in out_dtypes]
    res = pl.pallas_call(
        functools.partial(_mm_kernel, n_extra=len(extras), epilogue=epilogue, w_transposed=w_transposed),
        grid=(n_cols // tn, m // tm),
        in_specs=in_specs,
        out_specs=out_specs,
        out_shape=out_shape,
        scratch_shapes=[pltpu.VMEM((k, tn), BF16)],
        compiler_params=_cparams("parallel", "arbitrary"),
        name=name,
    )(*args)
    return res


def _rope_tile(acc, cos, sin_signed):
    parts = []
    for h in range(MM_TN // HEAD_DIM):
        x = acc[:, h * HEAD_DIM:(h + 1) * HEAD_DIM]
        parts.append(x * cos + pltpu.roll(x, HEAD_DIM // 2, axis=1) * sin_signed)
    return jnp.concatenate(parts, axis=1)


def _ep_rope_q(acc, extra, outs):
    outs[0][...] = (_rope_tile(acc, extra[0][...], extra[1][...]) * ATT_SCALE).astype(outs[0].dtype)


def _ep_rope_k(acc, extra, outs):
    outs[0][...] = _rope_tile(acc, extra[0][...], extra[1][...]).astype(outs[0].dtype)


def _ep_plain(acc, extra, outs):
    outs[0][...] = acc.astype(outs[0].dtype)


def _ep_sigmoid(acc, extra, outs):
    outs[0][...] = _sigmoid(acc).astype(outs[0].dtype)


def _ep_residual(acc, extra, outs):
    outs[0][...] = (extra[0][...] + acc).astype(outs[0].dtype)


def _attn_prompt_kernel(q_ref, kp_ref, kc_ref, vp_ref, vc_ref, o_ref, l_ref, *, tq):
    first_neg = jnp.where(pl.program_id(2) == 0, NEG, 0.0)
    qi = lax.broadcasted_iota(jnp.int32, (BACK, BACK), 0)
    kj = lax.broadcasted_iota(jnp.int32, (BACK, BACK), 1)
    prev_band = kj >= qi
    cur_band = kj <= qi
    for h in range(KV_HEADS):
        hs = slice(h * HEAD_DIM, (h + 1) * HEAD_DIM)
        for jb in range(tq // BACK):
            rows = slice(jb * BACK, (jb + 1) * BACK)
            qb = q_ref[0, rows, hs]
            if jb == 0:
                k_prev, v_prev = kp_ref[0, :, hs], vp_ref[0, :, hs]
                prev_neg = first_neg
            else:
                prow = slice((jb - 1) * BACK, jb * BACK)
                k_prev, v_prev = kc_ref[0, prow, hs], vc_ref[0, prow, hs]
                prev_neg = 0.0
            k_cur, v_cur = kc_ref[0, rows, hs], vc_ref[0, rows, hs]
            s_p = jnp.where(prev_band, _dot_nt(qb, k_prev), NEG) + prev_neg
            s_c = jnp.where(cur_band, _dot_nt(qb, k_cur), NEG)
            m = jnp.maximum(jnp.max(s_p, axis=-1, keepdims=True), jnp.max(s_c, axis=-1, keepdims=True))
            e_p = jnp.exp(s_p - m)
            e_c = jnp.exp(s_c - m)
            den = jnp.sum(e_p, axis=-1, keepdims=True) + jnp.sum(e_c, axis=-1, keepdims=True)
            o = (_dot(e_p, v_prev) + _dot(e_c, v_cur)) / den
            o_ref[0, rows, hs] = o
            l_ref[0, rows, hs] = jnp.broadcast_to(m + jnp.log(den), (BACK, HEAD_DIM))


def _attn_prompt_group(q, k, v, g, dil):
    b, s, _ = q.shape
    sub = s // dil
    tq = min(sub, 512)
    nq = sub // tq
    qv = q.reshape(b, sub, dil * Q_W)
    kv_ = k.reshape(b, sub, dil * KV_W)
    vv = v.reshape(b, sub, dil * KV_W)
    per = tq // BACK
    prev_map = lambda bi, r, i: (bi, jnp.maximum(i * per - 1, 0), r)
    cur_map = lambda bi, r, i: (bi, i, r)
    o, lse = pl.pallas_call(
        functools.partial(_attn_prompt_kernel, tq=tq),
        grid=(b, dil, nq),
        in_specs=[pl.BlockSpec((1, tq, KV_W), lambda bi, r, i: (bi, i, r * N_GROUPS + g)),
                  pl.BlockSpec((1, BACK, KV_W), prev_map),
                  pl.BlockSpec((1, tq, KV_W), cur_map),
                  pl.BlockSpec((1, BACK, KV_W), prev_map),
                  pl.BlockSpec((1, tq, KV_W), cur_map)],
        out_specs=[pl.BlockSpec((1, tq, KV_W), cur_map), pl.BlockSpec((1, tq, KV_W), cur_map)],
        out_shape=[jax.ShapeDtypeStruct((b, sub, dil * KV_W), F32)] * 2,
        compiler_params=_cparams("parallel", "parallel", "arbitrary"),
        name=f"attn_prompt_g{g}",
    )(qv, kv_, kv_, vv, vv)
    return o.reshape(b, s, KV_W), lse.reshape(b, s, KV_W)


def _attn_combine_kernel(o0, l0, o1, l1, o2, l2, out_ref):
    la, lb, lc = l0[...], l1[...], l2[...]
    m = jnp.maximum(jnp.maximum(la, lb), lc)
    wa, wb, wc = jnp.exp(la - m), jnp.exp(lb - m), jnp.exp(lc - m)
    out = (wa * o0[...] + wb * o1[...] + wc * o2[...]) / (wa + wb + wc)
    out_ref[...] = out.astype(out_ref.dtype)


def _attn_combine(parts, tm=512):
    m, w = parts[0].shape
    spec = pl.BlockSpec((tm, w), lambda i: (i, 0))
    return pl.pallas_call(
        _attn_combine_kernel,
        grid=(m // tm,),
        in_specs=[spec] * 6,
        out_specs=spec,
        out_shape=jax.ShapeDtypeStruct((m, w), BF16),
        compiler_params=_cparams("parallel"),
        name="attn_combine",
    )(*parts)


def _attn_sample_kernel(q_ref, kn_ref, vn_ref, k0_ref, v0_ref, k1_ref, v1_ref, k2_ref, v2_ref, o_ref, *, n_new):
    tok = lax.broadcasted_iota(jnp.int32, (BACK, KV_HEADS, 1), 0)
    kc_refs = (k0_ref, k1_ref, k2_ref)
    vc_refs = (v0_ref, v1_ref, v2_ref)
    for s in range(n_new):
        outs, lses = [], []
        for g in range(N_GROUPS):
            qh = q_ref[0, s, g * KV_HEADS:(g + 1) * KV_HEADS, :][None]
            if g == 0:
                kc, vc = kc_refs[g][0], vc_refs[g][0]
                new = slice(0, s + 1)
            else:
                kc, vc = kc_refs[g][0, :, s], vc_refs[g][0, :, s]
                new = slice(s, s + 1)
            sc = jnp.sum(kc * qh, axis=-1, keepdims=True)
            if g == 0:
                sc = jnp.where(tok >= s, sc, NEG)
            sn = jnp.sum(kn_ref[0, new] * qh, axis=-1, keepdims=True)
            m = jnp.maximum(jnp.max(sc, axis=0, keepdims=True), jnp.max(sn, axis=0, keepdims=True))
            p = jnp.exp(sc - m)
            pn = jnp.exp(sn - m)
            den = jnp.sum(p, axis=0, keepdims=True) + jnp.sum(pn, axis=0, keepdims=True)
            o = (jnp.sum(p * vc, axis=0, keepdims=True)
                 + jnp.sum(pn * vn_ref[0, new], axis=0, keepdims=True)) / den
            outs.append(o)
            lses.append(m + jnp.log(den))
        mm = jnp.maximum(jnp.maximum(lses[0], lses[1]), lses[2])
        ws = [jnp.exp(l - mm) for l in lses]
        comb = (ws[0] * outs[0] + ws[1] * outs[1] + ws[2] * outs[2]) / (ws[0] + ws[1] + ws[2])
        o_ref[0, s] = comb[0].astype(o_ref.dtype)


def _attn_sample(q, k_new, v_new, cache_k, cache_v):
    db, n_new = q.shape[0], q.shape[1]
    w_buf = cache_k.shape[1]
    assert w_buf == BACK * DILATIONS[-1] and n_new <= DILATIONS[1]
    new_spec = pl.BlockSpec((1, n_new, KV_HEADS, HEAD_DIM), lambda b: (b, 0, 0, 0))
    specs = [pl.BlockSpec((1, n_new, N_GROUPS * KV_HEADS, HEAD_DIM), lambda b: (b, 0, 0, 0)), new_spec, new_spec]
    args = [q, k_new, v_new]
    for dil in DILATIONS:
        sub = w_buf // dil
        last = sub // BACK - 1
        if dil == 1:
            shape = (db, sub, KV_HEADS, HEAD_DIM)
            spec = pl.BlockSpec((1, BACK, KV_HEADS, HEAD_DIM), functools.partial(lambda b, l: (b, l, 0, 0), l=last))
        else:
            shape = (db, sub, dil, KV_HEADS, HEAD_DIM)
            spec = pl.BlockSpec((1, BACK, n_new, KV_HEADS, HEAD_DIM),
                                functools.partial(lambda b, l: (b, l, 0, 0, 0), l=last))
        for c in (cache_k, cache_v):
            specs.append(spec)
            args.append(c.reshape(shape))
    return pl.pallas_call(
        functools.partial(_attn_sample_kernel, n_new=n_new),
        grid=(db,),
        in_specs=specs,
        out_specs=new_spec,
        out_shape=jax.ShapeDtypeStruct((db, n_new, KV_HEADS, HEAD_DIM), F32),
        compiler_params=_cparams("parallel"),
        name="attn_sample",
    )(*args)


def _head_indicator(width):
    l = lax.broadcasted_iota(jnp.int32, (width, LANE), 0)
    h = lax.broadcasted_iota(jnp.int32, (width, LANE), 1)
    return (_div_pow2(l, RW_N) == h).astype(F32)


def _head_indicator_t(width):
    h = lax.broadcasted_iota(jnp.int32, (LANE, width), 0)
    l = lax.broadcasted_iota(jnp.int32, (LANE, width), 1)
    return (_div_pow2(l, RW_N) == h).astype(F32)


def _rw_prep_kernel(ur, uk, uv, ul, fpr, fpk, fpv, fpl, fsr, fsk, fsv, fsl, mur, muk, muv, mul, ww, wa, wg, w0, a0,
                    kk_, ka, r_o, lw_o, k_o, v_o, kk_o, b_o, g_o, *, n_prompt_tiles, ds):
    is_sample = pl.program_id(0) >= n_prompt_tiles

    def mixed(u_ref, fp_ref, fs_ref, mu_ref):
        u = u_ref[...]
        row = lax.broadcasted_iota(jnp.int32, u.shape, 0)
        rolled = pltpu.roll(u, 1, axis=0)
        p_prompt = jnp.where(row == 0, fp_ref[0], rolled)
        p_sample = jnp.where(_mod_pow2(row, ds) == 0, fs_ref[...], rolled)
        prev = jnp.where(is_sample, p_sample, p_prompt)
        return u + (prev - u) * mu_ref[...]

    xr = mixed(ur, fpr, fsr, mur)
    xk = mixed(uk, fpk, fsk, muk)
    xv = mixed(uv, fpv, fsv, muv)
    xl = mixed(ul, fpl, fsl, mul)
    col = lax.broadcasted_iota(jnp.int32, xl.shape, 1)
    act = jnp.where(col < LORA_W, jnp.tanh(xl),
                    jnp.where(col < LORA_W + LORA_A, xl,
                              jnp.where(col < LORA_ALL, _sigmoid(xl), 0.0)))
    zw = _dot(act, ww[...])
    za = _dot(act, wa[...])
    zg = _dot(act, wg[...])
    lw = -RW_DECAY_SCALE * _sigmoid(w0[...] + zw)
    a = _sigmoid(a0[...] + za)
    kk = xk * kk_[...]
    width = kk.shape[1]
    ss = _dot_sel(kk * kk, _head_indicator(width), 2)
    inv = 1.0 / jnp.maximum(jnp.sqrt(ss), 1e-12)
    kkn = kk * _dot_sel(inv, _head_indicator_t(width), 2)
    r_o[...] = xr
    lw_o[...] = lw
    k_o[...] = xk * (1.0 + (a - 1.0) * ka[...])
    v_o[...] = xv
    kk_o[...] = kkn
    b_o[...] = kkn * a
    g_o[...] = zg


def _rw_prep(u_rkv, u_lora, shift, mp, sp, ds, mu, w_w, w_a, w_g, w0, a0, k_k, k_a, tm=512, tw=512):
    m = u_rkv.shape[0]
    nj = RW_W // tw
    n_p = mp // tm
    assert sp % tm == 0 and mp % tm == 0 and (m - mp) % tm == 0 and tm % ds == 0 and ds & (ds - 1) == 0
    mu_rkv = mu[:3 * RW_W].reshape(1, 3 * RW_W)
    lpad = LORA_PAD - LORA_ALL
    mu_l = jnp.pad(mu[3 * RW_W:], (0, lpad)).reshape(1, LORA_PAD)

    starts = jnp.arange(n_p, dtype=jnp.int32) * tm
    inside = ((starts % sp) != 0)[:, None]
    prev_idx = jnp.maximum(starts - 1, 0)

    def prompt_fix(u):
        return jnp.where(inside, jnp.take(u, prev_idx, axis=0), 0.0)[:, None, :]

    def sample_fix(first_rows):
        db, w = first_rows.shape
        return jnp.zeros((db, ds, w), F32).at[:, 0].set(first_rows).reshape(db * ds, w)

    fp_rkv, fp_l = prompt_fix(u_rkv), prompt_fix(u_lora)
    fs_rkv = sample_fix(shift[:, :3 * RW_W])
    fs_l = sample_fix(jnp.pad(shift[:, 3 * RW_W:], ((0, 0), (0, lpad))))

    def feat(off):
        return pl.BlockSpec((tm, tw), functools.partial(lambda i, j, o: (i, j + o), o=off))

    def fixp(off):
        return pl.BlockSpec((1, 1, tw), functools.partial(lambda i, j, o: (jnp.minimum(i, n_p - 1), 0, j + o), o=off))

    def fixs(off):
        return pl.BlockSpec((tm, tw), functools.partial(lambda i, j, o: (jnp.maximum(i - n_p, 0), j + o), o=off))

    def vec(off):
        return pl.BlockSpec((1, tw), functools.partial(lambda i, j, o: (0, j + o), o=off))

    lspec = pl.BlockSpec((tm, LORA_PAD), lambda i, j: (i, 0))
    lfixp = pl.BlockSpec((1, 1, LORA_PAD), lambda i, j: (jnp.minimum(i, n_p - 1), 0, 0))
    lfixs = pl.BlockSpec((tm, LORA_PAD), lambda i, j: (jnp.maximum(i - n_p, 0), 0))
    wspec = pl.BlockSpec((LORA_PAD, tw), lambda i, j: (0, j))
    in_specs = [feat(0), feat(nj), feat(2 * nj), lspec,
                fixp(0), fixp(nj), fixp(2 * nj), lfixp,
                fixs(0), fixs(nj), fixs(2 * nj), lfixs,
                vec(0), vec(nj), vec(2 * nj), pl.BlockSpec((1, LORA_PAD), lambda i, j: (0, 0)),
                wspec, wspec, wspec, vec(0), vec(0), vec(0), vec(0)]
    out_spec = pl.BlockSpec((tm, tw), lambda i, j: (i, j))
    outs = pl.pallas_call(
        functools.partial(_rw_prep_kernel, n_prompt_tiles=n_p, ds=ds),
        grid=(m // tm, nj),
        in_specs=in_specs,
        out_specs=[out_spec] * 7,
        out_shape=[jax.ShapeDtypeStruct((m, RW_W), F32)] * 7,
        compiler_params=_cparams("parallel", "arbitrary"),
        name="rw_prep",
    )(u_rkv, u_rkv, u_rkv, u_lora, fp_rkv, fp_rkv, fp_rkv, fp_l, fs_rkv, fs_rkv, fs_rkv, fs_l,
      mu_rkv, mu_rkv, mu_rkv, mu_l,
      w_w, w_a, w_g, w0.reshape(1, RW_W), a0.reshape(1, RW_W), k_k.reshape(1, RW_W), k_a.reshape(1, RW_W))
    return outs


def _rw_scan_kernel(r_ref, lw_ref, k_ref, v_ref, kk_ref, b_ref, s_in, y_ref, s_out, s_bd, *, chunk):
    t_id = pl.program_id(1)
    n_t = pl.num_programs(1)
    gw = RW_GROUP * RW_N
    n_groups = RW_W // gw
    rows = RW_GROUP * chunk

    ri = lax.broadcasted_iota(jnp.int32, (gw, gw), 0)
    ci = lax.broadcasted_iota(jnp.int32, (gw, gw), 1)
    state_mask = _div_pow2(ri, RW_N) == _div_pow2(ci, RW_N)

    @pl.when(t_id == 0)
    def _():
        kr = lax.broadcasted_iota(jnp.int32, (RW_N, gw), 0)
        kc = lax.broadcasted_iota(jnp.int32, (RW_N, gw), 1)
        spread = (_mod_pow2(kc, RW_N) == kr).astype(F32)
        for g in range(n_groups):
            tiled = _dot_sel(s_in[0, g * gw:(g + 1) * gw, :], spread, 3)
            s_bd[g] = jnp.where(state_mask, tiled, 0.0)

    ti = lax.broadcasted_iota(jnp.int32, (chunk, chunk), 0)
    tj = lax.broadcasted_iota(jnp.int32, (chunk, chunk), 1)
    lw = lw_ref[0]
    cum = _sel_dot((tj <= ti).astype(F32), lw, 3)
    e_pos = jnp.exp(cum)
    e_neg = jnp.exp(-cum)
    e_prev = jnp.exp(cum - lw)
    cum_last = cum[chunk - 1:chunk, :]
    e_rem = jnp.exp(cum_last - cum)
    c_all = jnp.exp(cum_last)
    kt = kk_ref[0] * e_prev
    bt = b_ref[0] * e_neg
    k2t = k_ref[0] * e_neg
    rt = r_ref[0] * e_pos
    btc = b_ref[0] * e_rem
    k2tc = k_ref[0] * e_rem
    vv = v_ref[0]

    sr = lax.broadcasted_iota(jnp.int32, (rows, gw), 0)
    sc = lax.broadcasted_iota(jnp.int32, (rows, gw), 1)
    stack_mask = _div_pow2(sr, chunk) == _div_pow2(sc, RW_N)
    ar = lax.broadcasted_iota(jnp.int32, (rows, rows), 0)
    ac = lax.broadcasted_iota(jnp.int32, (rows, rows), 1)
    strict = ar > ac
    incl = ar >= ac
    eye = (ar == ac).astype(F32)

    def stack(x):
        return jnp.where(stack_mask, jnp.concatenate([x] * RW_GROUP, axis=0), 0.0).astype(BF16)

    gs = range(n_groups)
    lanes = [slice(g * gw, (g + 1) * gw) for g in gs]
    kr_s = [jnp.concatenate([stack(kt[:, ls]), stack(rt[:, ls])], axis=0) for ls in lanes]
    bt_s = [stack(bt[:, ls]) for ls in lanes]
    k2t_s = [stack(k2t[:, ls]) for ls in lanes]
    v_s = [stack(vv[:, ls]) for ls in lanes]
    s0 = [s_bd[g] for g in gs]
    p_b = [_dot_nt(kr_s[g], bt_s[g]) for g in gs]
    p_k = [_dot_nt(kr_s[g], k2t_s[g]) for g in gs]
    p_s = [_dot_nt(kr_s[g], s0[g]) for g in gs]
    a_b = [jnp.where(strict, p_b[g][:rows], 0.0) for g in gs]
    r_b = [jnp.where(incl, p_b[g][rows:], 0.0) for g in gs]
    ar_k = [jnp.concatenate([jnp.where(strict, p_k[g][:rows], 0.0), jnp.where(incl, p_k[g][rows:], 0.0)], axis=0)
            for g in gs]
    p_v = [_dot(ar_k[g], v_s[g]) for g in gs]
    rhs = [p_s[g][:rows] + p_v[g][:rows] for g in gs]
    inv = [eye - a_b[g] for g in gs]
    pw = a_b
    n = 2
    while n < chunk:
        pw = [_dot(pw[g], pw[g]) for g in gs]
        inv = [inv[g] + _dot(inv[g], pw[g]) for g in gs]
        n *= 2
    u = [-_dot(inv[g], rhs[g]) for g in gs]
    y_bd = [p_s[g][rows:] + p_v[g][rows:] + _dot(r_b[g], u[g]) for g in gs]
    for g in gs:
        y = y_bd[g][0:chunk]
        for h in range(1, RW_GROUP):
            y = y + y_bd[g][h * chunk:(h + 1) * chunk]
        y_ref[0, :, lanes[g]] = y
    for g in gs:
        uv = jnp.concatenate([u[g].astype(BF16), v_s[g]], axis=0)
        bk = jnp.concatenate([stack(btc[:, lanes[g]]), stack(k2tc[:, lanes[g]])], axis=0)
        s_bd[g] = s0[g] * c_all[:, lanes[g]] + _dot_tn(uv, bk)

    @pl.when(t_id == n_t - 1)
    def _():
        gr = lax.broadcasted_iota(jnp.int32, (gw, RW_N), 0)
        gc = lax.broadcasted_iota(jnp.int32, (gw, RW_N), 1)
        gather = (_mod_pow2(gr, RW_N) == gc).astype(F32)
        for g in range(n_groups):
            s_out[0, g * gw:(g + 1) * gw, :] = _dot_sel(s_bd[g], gather, 3)


def _rw_scan(r, lw, k, v, kk, b, state, chunk, bn, t):
    gw = RW_GROUP * RW_N
    nt = t // chunk
    seq = pl.BlockSpec((1, chunk, RW_W), lambda bi, ti: (0, bi * nt + ti, 0))
    st = pl.BlockSpec((1, RW_W, RW_N), lambda bi, ti: (bi, 0, 0))
    y, s_new = pl.pallas_call(
        functools.partial(_rw_scan_kernel, chunk=chunk),
        grid=(bn, nt),
        in_specs=[seq] * 6 + [st],
        out_specs=[seq, st],
        out_shape=[jax.ShapeDtypeStruct((1, bn * t, RW_W), F32), jax.ShapeDtypeStruct((bn, RW_W, RW_N), F32)],
        scratch_shapes=[pltpu.VMEM((RW_W // gw, gw, gw), F32)],
        compiler_params=_cparams("parallel", "arbitrary"),
        name=f"rw_scan_c{chunk}",
    )(*(a[None] for a in (r, lw, k, v, kk, b)), state.reshape(bn, RW_W, RW_N))
    return y[0], s_new.reshape(bn, RW_HEADS, RW_N, RW_N)


def _rw_step_kernel(r_ref, lw_ref, k_ref, v_ref, kk_ref, b_ref, s_in, y_ref, s_out, *, n_steps):
    sub = 8
    for t in range(n_steps):
        w = jnp.exp(lw_ref[t])
        kap, bb, k2, r = kk_ref[t], b_ref[t], k_ref[t], r_ref[t]
        src = s_in if t == 0 else s_out

        def body(blk, c, t=t, w=w, kap=kap, bb=bb, k2=k2, r=r, src=src):
            base = pl.multiple_of(blk * sub, sub)
            v_rows = v_ref[t, pl.ds(base, sub), :]
            y_rows = []
            for j in range(sub):
                sv = src[0, base + j]
                sa = -jnp.sum(sv * kap, axis=0, keepdims=True)
                sn = sv * w + sa * bb + v_rows[j:j + 1] * k2
                s_out[0, base + j] = sn
                y_rows.append(jnp.sum(sn * r, axis=0, keepdims=True))
            y_ref[t, pl.ds(base, sub), :] = jnp.concatenate(y_rows, axis=0)
            return c

        lax.fori_loop(0, RW_N // sub, body, 0)


def _rw_steps(r, lw, k, v, kk, b, state):
    n_steps, _, nb = r.shape
    assert nb % LANE == 0
    seq = pl.BlockSpec((n_steps, RW_N, nb), lambda h: (0, h, 0))
    st = pl.BlockSpec((1, RW_N, RW_N, nb), lambda h: (h, 0, 0, 0))
    return pl.pallas_call(
        functools.partial(_rw_step_kernel, n_steps=n_steps),
        grid=(RW_HEADS,),
        in_specs=[seq] * 6 + [st],
        out_specs=[seq, st],
        out_shape=[jax.ShapeDtypeStruct((n_steps, RW_W, nb), F32),
                   jax.ShapeDtypeStruct((RW_HEADS, RW_N, RW_N, nb), F32)],
        compiler_params=_cparams("parallel"),
        name="rw_steps",
    )(r, lw, k, v, kk, b, state)


def _rw_post_kernel(y_ref, r_ref, k_ref, v_ref, g_ref, lnw, lnb, rk, o_ref):
    y = y_ref[...]
    width = y.shape[1]
    ind, ind_t = _head_indicator(width), _head_indicator_t(width)
    mean = _dot_sel(_dot_sel(y, ind, 2) * (1.0 / RW_N), ind_t, 2)
    d = y - mean
    var_h = _dot_sel(d * d, ind, 2) * (1.0 / RW_N)
    rstd = _dot_sel(lax.rsqrt(var_h + RW_GN_EPS), ind_t, 2)
    yn = d * rstd * lnw[...] + lnb[...]
    bonus = _dot_sel(_dot_sel(r_ref[...] * k_ref[...] * rk[...], ind, 2), ind_t, 2) * v_ref[...]
    o_ref[...] = ((yn + bonus) * g_ref[...]).astype(o_ref.dtype)


def _rw_post(y, r, k, v, g, ln_w, ln_b, r_k, tm=512, tw=512):
    m = y.shape[0]
    feat = pl.BlockSpec((tm, tw), lambda i, j: (i, j))
    vec = pl.BlockSpec((1, tw), lambda i, j: (0, j))
    return pl.pallas_call(
        _rw_post_kernel,
        grid=(m // tm, RW_W // tw),
        in_specs=[feat] * 5 + [vec] * 3,
        out_specs=feat,
        out_shape=jax.ShapeDtypeStruct((m, RW_W), BF16),
        compiler_params=_cparams("parallel", "parallel"),
        name="rw_post",
    )(y, r, k, v, g, ln_w.reshape(1, RW_W), ln_b.reshape(1, RW_W), r_k.reshape(1, RW_W))


def _merge_kernel(oa_ref, ob_ref, wa_ref, wb_ref, ga_ref, gb_ref, out_ref, wa_bf, wb_bf):
    @pl.when(pl.program_id(1) == 0)
    def _():
        wa_bf[...] = wa_ref[...].astype(BF16)
        wb_bf[...] = wb_ref[...].astype(BF16)

    ya = jnp.dot(oa_ref[...], wa_bf[...], preferred_element_type=F32)
    yb = jnp.dot(ob_ref[...], wb_bf[...], preferred_element_type=F32)
    out_ref[...] = (ga_ref[...].astype(F32) * ya + gb_ref[...].astype(F32) * yb).astype(out_ref.dtype)


def _merge(o_a, o_b, w_a, w_b, gates, tm):
    m = o_a.shape[0]
    tn = MM_TN
    nj = D_MODEL // tn
    return pl.pallas_call(
        _merge_kernel,
        grid=(nj, m // tm),
        in_specs=[pl.BlockSpec((tm, KV_W), lambda j, i: (i, 0)),
                  pl.BlockSpec((tm, RW_W), lambda j, i: (i, 0)),
                  pl.BlockSpec((KV_W, tn), lambda j, i: (0, j)),
                  pl.BlockSpec((RW_W, tn), lambda j, i: (0, j)),
                  pl.BlockSpec((tm, tn), lambda j, i: (i, j)),
                  pl.BlockSpec((tm, tn), lambda j, i: (i, j + nj))],
        out_specs=pl.BlockSpec((tm, tn), lambda j, i: (i, j)),
        out_shape=jax.ShapeDtypeStruct((m, D_MODEL), BF16),
        scratch_shapes=[pltpu.VMEM((KV_W, tn), BF16), pltpu.VMEM((RW_W, tn), BF16)],
        compiler_params=_cparams("parallel", "arbitrary"),
        name="merge",
    )(o_a, o_b, w_a, w_b, gates, gates)


def _router_kernel(x_ref, g_ref, wr_ref, br_ref, h_ref, id_ref, wt_ref):
    x = x_ref[...]
    ms = jnp.mean(x * x, axis=-1, keepdims=True)
    h = x * lax.rsqrt(ms + NORM_EPS) * g_ref[...]
    h_ref[...] = h.astype(h_ref.dtype)
    logits = _dot_hi(h, wr_ref[...]) + br_ref[...]
    lane = lax.broadcasted_iota(jnp.int32, logits.shape, 1)
    lane_f = lane.astype(F32)
    gmask = lane < MOE_GROUPS
    gl = jnp.where(gmask, logits, NEG)
    gm = jnp.max(gl, axis=-1, keepdims=True)
    gi = jnp.min(jnp.where(gl == gm, lane_f, float(LANE)), axis=-1, keepdims=True)
    g_prob = 1.0 / jnp.sum(jnp.where(gmask, jnp.exp(gl - gm), 0.0), axis=-1, keepdims=True)
    lo = MOE_GROUPS + gi * MOE_PER_GROUP
    emask = jnp.logical_and(lane_f >= lo, lane_f < lo + MOE_PER_GROUP)
    el = jnp.where(emask, logits, NEG)
    m1 = jnp.max(el, axis=-1, keepdims=True)
    i1 = jnp.min(jnp.where(el == m1, lane_f, float(LANE)), axis=-1, keepdims=True)
    el2 = jnp.where(lane_f == i1, NEG, el)
    m2 = jnp.max(el2, axis=-1, keepdims=True)
    i2 = jnp.min(jnp.where(el2 == m2, lane_f, float(LANE)), axis=-1, keepdims=True)
    t = jnp.exp(m2 - m1)
    w1 = g_prob / (1.0 + t)
    w2 = g_prob * t / (1.0 + t)
    ids = jnp.where(lane == 0, i1 - MOE_GROUPS, jnp.where(lane == 1, i2 - MOE_GROUPS, 0.0))
    id_ref[...] = ids.astype(jnp.int32)
    wt_ref[...] = jnp.where(lane == 0, w1, jnp.where(lane == 1, w2, 0.0))


def _router(x, g, w_router, b_router, tm=256):
    m, d = x.shape
    row = pl.BlockSpec((tm, d), lambda i: (i, 0))
    small = pl.BlockSpec((tm, LANE), lambda i: (i, 0))
    return pl.pallas_call(
        _router_kernel,
        grid=(m // tm,),
        in_specs=[row, pl.BlockSpec((1, d), lambda i: (0, 0)),
                  pl.BlockSpec((d, LANE), lambda i: (0, 0)), pl.BlockSpec((1, LANE), lambda i: (0, 0))],
        out_specs=[row, small, small],
        out_shape=[jax.ShapeDtypeStruct((m, d), F32), jax.ShapeDtypeStruct((m, LANE), jnp.int32),
                   jax.ShapeDtypeStruct((m, LANE), F32)],
        compiler_params=_cparams("parallel"),
        name="ffn_norm_router",
    )(x, g.reshape(1, d), w_router, b_router)


def _moe_kernel(te_ref, tv_ref, last_ref, x_ref, wg_ref, wu_ref, wd_ref, rw_ref, o_ref):
    i = pl.program_id(0)
    j = pl.program_id(1)
    valid = tv_ref[i] > 0

    @pl.when(j == 0)
    def _():
        o_ref[...] = jnp.zeros_like(o_ref)

    @pl.when(valid)
    def _():
        x = x_ref[...]
        gate = jnp.dot(x, wg_ref[0].astype(BF16), preferred_element_type=F32)
        up = jnp.dot(x, wu_ref[0].astype(BF16), preferred_element_type=F32)
        hidden = (gate * _sigmoid(gate) * up).astype(BF16)
        o_ref[...] += jnp.dot(hidden, wd_ref[0].astype(BF16), preferred_element_type=F32)

    @pl.when(jnp.logical_and(valid, j == pl.num_programs(1) - 1))
    def _():
        o_ref[...] = o_ref[...] * rw_ref[...]


def _moe_experts(xs, row_w, tile_expert, tile_valid, last_tile, w_gate, w_up, w_down):
    p, d = xs.shape
    n_tiles = p // MOE_TM
    nf = EXPERT_FF // MOE_TF

    def f_idx(i, j, tv):
        return jnp.where(tv[i] > 0, j, nf - 1)

    def rows(i, j, te, tv, last):
        return (jnp.minimum(i, last[0]), 0)

    grid_spec = pltpu.PrefetchScalarGridSpec(
        num_scalar_prefetch=3,
        grid=(n_tiles, nf),
        in_specs=[pl.BlockSpec((MOE_TM, d), rows),
                  pl.BlockSpec((1, d, MOE_TF), lambda i, j, te, tv, last: (te[i], 0, f_idx(i, j, tv))),
                  pl.BlockSpec((1, d, MOE_TF), lambda i, j, te, tv, last: (te[i], 0, f_idx(i, j, tv))),
                  pl.BlockSpec((1, MOE_TF, d), lambda i, j, te, tv, last: (te[i], f_idx(i, j, tv), 0)),
                  pl.BlockSpec((MOE_TM, 1), rows)],
        out_specs=pl.BlockSpec((MOE_TM, d), lambda i, j, te, tv, last: (i, 0)),
    )
    return pl.pallas_call(
        _moe_kernel,
        grid_spec=grid_spec,
        out_shape=jax.ShapeDtypeStruct((p, d), F32),
        compiler_params=_cparams("arbitrary", "arbitrary"),
        name="moe_experts",
    )(tile_expert, tile_valid, last_tile, xs, w_gate, w_up, w_down, row_w)


def _row_copy(src_hbm, row, dst_vmem, slot, sem):
    return pltpu.make_async_copy(src_hbm.at[pl.ds(row, 1)], dst_vmem.at[pl.ds(slot, 1)], sem)


def _dispatch_kernel(tok_ref, nv_ref, h_hbm, o_ref, buf, sem):
    i = pl.program_id(0)
    n = nv_ref[i]

    @pl.when(i == 0)
    def _():
        buf[...] = jnp.zeros_like(buf)

    def issue(r, c):
        _row_copy(h_hbm, tok_ref[i * MOE_TM + r], buf, r, sem).start()
        return c

    def wait(r, c):
        _row_copy(h_hbm, 0, buf, r, sem).wait()
        return c

    lax.fori_loop(0, n, issue, 0)
    lax.fori_loop(0, n, wait, 0)
    o_ref[...] = buf[...].astype(o_ref.dtype)


def _dispatch_rows(h, slot_tok, tile_rows):
    d = h.shape[1]
    p = slot_tok.shape[0]
    grid_spec = pltpu.PrefetchScalarGridSpec(
        num_scalar_prefetch=2,
        grid=(p // MOE_TM,),
        in_specs=[pl.BlockSpec(memory_space=pl.ANY)],
        out_specs=pl.BlockSpec((MOE_TM, d), lambda i, tok, nv: (i, 0)),
        scratch_shapes=[pltpu.VMEM((MOE_TM, d), F32), pltpu.SemaphoreType.DMA(())],
    )
    return pl.pallas_call(
        _dispatch_kernel,
        grid_spec=grid_spec,
        out_shape=jax.ShapeDtypeStruct((p, d), BF16),
        compiler_params=_cparams("arbitrary"),
        name="moe_dispatch",
    )(slot_tok, tile_rows, h)


def _dispatch_plan(ids, wts):
    t = ids.shape[0]
    a = t * 2
    flat_e = ids.reshape(a)
    onehot = (flat_e[:, None] == jnp.arange(N_EXPERTS, dtype=jnp.int32)[None, :]).astype(jnp.int32)
    csum = jnp.cumsum(onehot, axis=0)
    counts = csum[-1]
    rank = jnp.take_along_axis(csum, flat_e[:, None], axis=1)[:, 0] - 1
    tiles_per = (counts + MOE_TM - 1) // MOE_TM
    tile_end = jnp.cumsum(tiles_per)
    tile_start = tile_end - tiles_per
    dest = tile_start[flat_e] * MOE_TM + rank
    n_tiles = -(-a // MOE_TM) + N_EXPERTS
    total = tile_end[-1]
    slot_tok = jnp.zeros((n_tiles * MOE_TM,), jnp.int32).at[dest].set(jnp.arange(a, dtype=jnp.int32) // 2)
    slot_w = jnp.zeros((n_tiles * MOE_TM,), F32).at[dest].set(wts.reshape(a))
    tile_ids = jnp.arange(n_tiles, dtype=jnp.int32)
    tile_valid = (tile_ids < total).astype(jnp.int32)
    tile_expert = jnp.searchsorted(tile_end, jnp.minimum(tile_ids, total - 1), side="right").astype(jnp.int32)
    tile_expert = jnp.minimum(tile_expert, N_EXPERTS - 1)
    tile_pos = tile_ids - tile_start[tile_expert]
    tile_rows = jnp.where(tile_valid > 0, jnp.clip(counts[tile_expert] - tile_pos * MOE_TM, 0, MOE_TM), 0)
    last_tile = (total - 1).astype(jnp.int32).reshape(1)
    return slot_tok, slot_w, dest, tile_expert, tile_valid, tile_rows.astype(jnp.int32), last_tile


def _final_kernel(dest_ref, x_ref, y_hbm, g_ref, o_ref, buf, sem, *, row_off, tm):
    base = (row_off + pl.program_id(0) * tm) * 2

    def issue(r, c):
        _row_copy(y_hbm, dest_ref[base + r], buf, (r & 1) * tm + (r >> 1), sem).start()
        return c

    def wait(r, c):
        _row_copy(y_hbm, 0, buf, r, sem).wait()
        return c

    lax.fori_loop(0, 2 * tm, issue, 0)
    lax.fori_loop(0, 2 * tm, wait, 0)
    x = x_ref[...] + (buf[0:tm, :] + buf[tm:2 * tm, :])
    ms = jnp.mean(x * x, axis=-1, keepdims=True)
    o_ref[...] = x * lax.rsqrt(ms + NORM_EPS) * g_ref[...]


def _final(x, y_rows, dest, g, row_off, rows, tm=128):
    d = x.shape[1]
    off = row_off // tm
    grid_spec = pltpu.PrefetchScalarGridSpec(
        num_scalar_prefetch=1,
        grid=(rows // tm,),
        in_specs=[pl.BlockSpec((tm, d), lambda i, dst: (i + off, 0)),
                  pl.BlockSpec(memory_space=pl.ANY),
                  pl.BlockSpec((1, d), lambda i, dst: (0, 0))],
        out_specs=pl.BlockSpec((tm, d), lambda i, dst: (i, 0)),
        scratch_shapes=[pltpu.VMEM((2 * tm, d), F32), pltpu.SemaphoreType.DMA(())],
    )
    return pl.pallas_call(
        functools.partial(_final_kernel, row_off=row_off, tm=tm),
        grid_spec=grid_spec,
        out_shape=jax.ShapeDtypeStruct((rows, d), F32),
        compiler_params=_cparams("arbitrary"),
        name="final_norm",
    )(dest, x, y_rows, g.reshape(1, d))


def _rope_tables(pos):
    half = HEAD_DIM // 2
    inv_freq = ROPE_THETA ** (-jnp.arange(half, dtype=F32) / half)
    ang = pos.astype(F32)[:, None] * inv_freq[None, :]
    cos, sin = jnp.cos(ang), jnp.sin(ang)
    return jnp.concatenate([cos, cos], axis=-1), jnp.concatenate([-sin, sin], axis=-1)


def kernel(x_prompt, x_sample, cache_k, cache_v, state_shift, state_wkv, norm_mix_g, w_in, rw_mu, rw_w0, rw_w2,
           rw_a0, rw_a2, rw_g2, rw_k_k, rw_k_a, rw_r_k, rw_ln_w, rw_ln_b, w_branch_a, w_branch_b, w_out,
           norm_ffn_g, router_group_w, router_group_b, router_expert_w, router_expert_b, exp_gate, exp_up,
           exp_down, norm_final_g):
    assert w_in.shape[0] == 1, "single-layer trunk"
    bp, sp, d = x_prompt.shape
    db, ds, _ = x_sample.shape
    mp, ms_ = bp * sp, db * ds
    m = mp + ms_
    past = cache_k.shape[2]
    tm_mm = m // 8

    x_all = jnp.concatenate([x_prompt.reshape(mp, d), x_sample.reshape(ms_, d)], axis=0)
    h = _rmsnorm(x_all, norm_mix_g[0], BF16)

    w_in_t = jnp.swapaxes(w_in, 1, 2).reshape(w_in.shape[2], d)
    cos_p, sin_p = _rope_tables(jnp.arange(sp, dtype=jnp.int32))
    cos_s, sin_s = _rope_tables(past + jnp.arange(ds, dtype=jnp.int32))
    cos = jnp.concatenate([jnp.tile(cos_p, (bp, 1)), jnp.tile(cos_s, (db, 1))], axis=0)
    sin = jnp.concatenate([jnp.tile(sin_p, (bp, 1)), jnp.tile(sin_s, (db, 1))], axis=0)
    rope_extras = [(cos, "row", 0), (sin, "row", 0)]
    def inproj(w, n_cols, col, epilogue, extras, dtype, name):
        return _matmul(h, w, n_cols, col // MM_TN, epilogue, extras, [dtype], tm_mm, name, w_transposed=True)[0]

    q = inproj(w_in_t, Q_W, 0, _ep_rope_q, rope_extras, BF16, "inproj_q")
    k = inproj(w_in_t, KV_W, COL_K, _ep_rope_k, rope_extras, F32, "inproj_k")
    v = inproj(w_in_t, KV_W, COL_V, _ep_plain, [], F32, "inproj_v")
    u_rkv = inproj(w_in_t, 3 * RW_W, COL_RW, _ep_plain, [], F32, "inproj_rkv")
    u_lora = inproj(w_in_t, LORA_PAD, COL_LORA, _ep_plain, [], F32, "inproj_lora")
    gates = inproj(w_in_t[COL_GATE:], 2 * d, 0, _ep_sigmoid, [], BF16, "inproj_gates")

    qp = q[:mp].reshape(bp, sp, Q_W)
    kp = k[:mp].reshape(bp, sp, KV_W)
    vp = v[:mp].reshape(bp, sp, KV_W)
    parts = []
    for g, dil in enumerate(DILATIONS):
        o_g, l_g = _attn_prompt_group(qp, kp, vp, g, dil)
        parts += [o_g.reshape(mp, KV_W), l_g.reshape(mp, KV_W)]
    oa_p = _attn_combine(parts)
    ks = k[mp:].reshape(db, ds, KV_HEADS, HEAD_DIM)
    vs = v[mp:].reshape(db, ds, KV_HEADS, HEAD_DIM)
    oa_s = _attn_sample(q[mp:].astype(F32).reshape(db, ds, N_GROUPS * KV_HEADS, HEAD_DIM), ks, vs,
                        cache_k.reshape(db, past, KV_HEADS, HEAD_DIM), cache_v.reshape(db, past, KV_HEADS, HEAD_DIM))
    o_a = jnp.concatenate([oa_p, oa_s.reshape(ms_, KV_W).astype(BF16)], axis=0)

    zl = functools.partial(jnp.zeros, dtype=F32)
    w_w = zl((LORA_PAD, RW_W)).at[:LORA_W].set(rw_w2[0])
    w_a = zl((LORA_PAD, RW_W)).at[LORA_W:LORA_W + LORA_A].set(rw_a2[0])
    w_g = zl((LORA_PAD, RW_W)).at[LORA_W + LORA_A:LORA_ALL].set(rw_g2[0])
    r_, lw_, k2_, v_, kk_, b_, g_ = _rw_prep(u_rkv, u_lora, state_shift[0], mp, sp, ds, rw_mu[0], w_w, w_a, w_g,
                                             rw_w0[0], rw_a0[0], rw_k_k[0], rw_k_a[0])

    scan_in = (r_, lw_, k2_, v_, kk_, b_)
    y_p, wkv_p = _rw_scan(*scan_in, jnp.zeros((bp, RW_HEADS, RW_N, RW_N), F32), RW_CHUNK, bp, sp)
    samp = [a[mp:].reshape(db, ds, RW_W).transpose(1, 2, 0) for a in scan_in]
    y_s, wkv_s = _rw_steps(*samp, jnp.transpose(state_wkv[0], (1, 2, 3, 0)))
    wkv_s = jnp.transpose(wkv_s, (3, 0, 1, 2))
    y_rw = jnp.concatenate([y_p, y_s.transpose(2, 0, 1).reshape(ms_, RW_W)], axis=0)
    o_b = _rw_post(y_rw, r_, k2_, v_, g_, rw_ln_w[0], rw_ln_b[0], rw_r_k[0])

    merged = _merge(o_a, o_b, w_branch_a[0], w_branch_b[0], gates, tm_mm)
    (x1,) = _matmul(merged, w_out.reshape(d, d), d, 0, _ep_residual, [(x_all, "tile", 0)], [F32], tm_mm, "out_proj")

    w_router = jnp.concatenate([router_group_w[0], router_expert_w[0],
                                jnp.zeros((d, LANE - MOE_GROUPS - N_EXPERTS), F32)], axis=1)
    b_router = jnp.concatenate([router_group_b[0], router_expert_b[0],
                                jnp.zeros((LANE - MOE_GROUPS - N_EXPERTS,), F32)]).reshape(1, LANE)
    h2, ids, wts = _router(x1, norm_ffn_g[0], w_router, b_router)
    slot_tok, slot_w, dest, tile_expert, tile_valid, tile_rows, last_tile = _dispatch_plan(ids[:, :2], wts[:, :2])
    xs = _dispatch_rows(h2, slot_tok, tile_rows)
    yb = _moe_experts(xs, slot_w[:, None], tile_expert, tile_valid, last_tile,
                      exp_gate[0], exp_up[0], exp_down[0])
    y_prompt = _final(x1, yb, dest, norm_final_g, 0, mp).reshape(bp, sp, d)
    y_sample = _final(x1, yb, dest, norm_final_g, mp, ms_).reshape(db, ds, d)

    keep = min(BACK * DILATIONS[-1], sp)
    k_prompt = k[:mp].reshape(1, bp, sp, KV_HEADS, HEAD_DIM)[:, :, sp - keep:]
    v_prompt = v[:mp].reshape(1, bp, sp, KV_HEADS, HEAD_DIM)[:, :, sp - keep:]
    k_sample = ks.reshape(1, db, ds, KV_HEADS, HEAD_DIM)
    v_sample = vs.reshape(1, db, ds, KV_HEADS, HEAD_DIM)

    def last_rows(lo, nb, t):
        rows = lo + t - 1 + t * jnp.arange(nb, dtype=jnp.int32)
        a = jnp.take(u_rkv, rows, axis=0)
        b = jnp.take(u_lora, rows, axis=0)[:, :LORA_ALL]
        return jnp.concatenate([a, b], axis=-1)[None]

    return (y_prompt, y_sample, k_prompt, v_prompt, k_sample, v_sample,
            last_rows(0, bp, sp), last_rows(mp, db, ds), wkv_p[None], wkv_s[None])
```

```python
import functools
import math

import jax
import jax.numpy as jnp
from jax import lax
from jax.experimental import pallas as pl
from jax.experimental.pallas import tpu as pltpu

F32 = jnp.float32
BF16 = jnp.bfloat16
HI = lax.Precision.HIGHEST

D_MODEL = 4096
NORM_EPS = 1e-6
HEAD_DIM = 128
KV_HEADS = 8
DILATIONS = (1, 4, 16)
BACK = 128
N_GROUPS = 3
ROPE_THETA = 10000.0
ATT_SCALE = HEAD_DIM ** -0.5
Q_W = N_GROUPS * KV_HEADS * HEAD_DIM
KV_W = KV_HEADS * HEAD_DIM
RW_N = 64
RW_HEADS = 32
RW_W = RW_HEADS * RW_N
LORA_W, LORA_A, LORA_G = 96, 96, 256
LORA_ALL = LORA_W + LORA_A + LORA_G
RW_FEAT = 3 * RW_W + LORA_ALL
RW_GN_EPS = 64e-5
RW_DECAY_SCALE = math.exp(-0.5)
COL_K = Q_W
COL_V = Q_W + KV_W
COL_RW = Q_W + 2 * KV_W
COL_LORA = COL_RW + 3 * RW_W
COL_GATE = COL_RW + RW_FEAT
MOE_GROUPS = 8
MOE_PER_GROUP = 8
N_EXPERTS = 64
EXPERT_FF = 1024

LANE = 128
VMEM_LIMIT_BYTES = 56 * 1024 * 1024
MM_TN = 512
LORA_PAD = 512
MOE_TM = 384
MOE_TK = 1024
MOE_TN = 1024
MOE_NK = D_MODEL // MOE_TK
MOE_NN = D_MODEL // MOE_TN
RW_CHUNK = 64
RW_GROUP = 4
NEG = -1e30


def _cparams(*sem):
    return pltpu.CompilerParams(dimension_semantics=sem, vmem_limit_bytes=VMEM_LIMIT_BYTES)


def _dot(a, b):
    return jnp.dot(a.astype(BF16), b.astype(BF16), preferred_element_type=F32)


def _dot_nt(a, b):
    return lax.dot_general(a.astype(BF16), b.astype(BF16), (((1,), (1,)), ((), ())),
                           preferred_element_type=F32)


def _dot_tn(a, b):
    return lax.dot_general(a.astype(BF16), b.astype(BF16), (((0,), (0,)), ((), ())),
                           preferred_element_type=F32)


def _dot_hi(a, b):
    return jnp.dot(a, b, preferred_element_type=F32, precision=HI)


def _bf16_terms(x, terms):
    parts = []
    for _ in range(terms):
        p = x.astype(BF16)
        parts.append(p)
        x = x - p.astype(F32)
    return parts


def _dot_sel(x, sel, terms):
    sel = sel.astype(BF16)
    return sum(jnp.dot(p, sel, preferred_element_type=F32) for p in _bf16_terms(x, terms))


def _sel_dot(sel, x, terms):
    sel = sel.astype(BF16)
    return sum(jnp.dot(sel, p, preferred_element_type=F32) for p in _bf16_terms(x, terms))


def _sigmoid(x):
    return 1.0 / (1.0 + jnp.exp(-x))


def _div_pow2(x, n):
    return x >> (n.bit_length() - 1)


def _mod_pow2(x, n):
    return x & (n - 1)


def _rmsnorm_kernel(x_ref, g_ref, o_ref):
    x = x_ref[...]
    ms = jnp.mean(x * x, axis=-1, keepdims=True)
    o_ref[...] = (x * lax.rsqrt(ms + NORM_EPS) * g_ref[...]).astype(o_ref.dtype)


def _rmsnorm(x, g, out_dtype, tm=512):
    m, d = x.shape
    return pl.pallas_call(
        _rmsnorm_kernel,
        grid=(m // tm,),
        in_specs=[pl.BlockSpec((tm, d), lambda i: (i, 0)), pl.BlockSpec((1, d), lambda i: (0, 0))],
        out_specs=pl.BlockSpec((tm, d), lambda i: (i, 0)),
        out_shape=jax.ShapeDtypeStruct((m, d), out_dtype),
        compiler_params=_cparams("parallel"),
        name="rmsnorm",
    )(x, g.reshape(1, d))


def _mm_kernel(a_ref, w_ref, *refs, n_extra, epilogue, w_transposed):
    extra = refs[:n_extra]
    outs = refs[n_extra:-1]
    wbf = refs[-1]

    @pl.when(pl.program_id(1) == 0)
    def _():
        w = w_ref[...]
        wbf[...] = (w.T if w_transposed else w).astype(BF16)

    acc = jnp.dot(a_ref[...], wbf[...], preferred_element_type=F32)
    epilogue(acc, extra, outs)


def _matmul(a, w, n_cols, col_off_tiles, epilogue, extras, out_dtypes, tm, name, w_transposed=False):
    m, k = a.shape
    tn = MM_TN
    if w_transposed:
        w_spec = pl.BlockSpec((tn, k), lambda j, i: (j + col_off_tiles, 0))
    else:
        w_spec = pl.BlockSpec((k, tn), lambda j, i: (0, j + col_off_tiles))
    in_specs = [pl.BlockSpec((tm, k), lambda j, i: (i, 0)), w_spec]
    args = [a, w]
    for arr, kind, off in extras:
        if kind == "row":
            in_specs.append(pl.BlockSpec((tm, arr.shape[1]), lambda j, i: (i, 0)))
        elif kind == "tile":
            in_specs.append(pl.BlockSpec((tm, tn), functools.partial(lambda j, i, o: (i, j + o), o=off)))
        else:
            in_specs.append(pl.BlockSpec((1, tn), functools.partial(lambda j, i, o: (0, j + o), o=off)))
        args.append(arr)
    out_specs = [pl.BlockSpec((tm, tn), lambda j, i: (i, j)) for _ in out_dtypes]
    out_shape = [jax.ShapeDtypeStruct((m, n_cols), dt) for dt in out_dtypes]
    res = pl.pallas_call(
        functools.partial(_mm_kernel, n_extra=len(extras), epilogue=epilogue, w_transposed=w_transposed),
        grid=(n_cols // tn, m // tm),
        in_specs=in_specs,
        out_specs=out_specs,
        out_shape=out_shape,
        scratch_shapes=[pltpu.VMEM((k, tn), BF16)],
        compiler_params=_cparams("parallel", "arbitrary"),
        name=name,
    )(*args)
    return res


def _rope_tile(acc, cos, sin_signed):
    parts = []
    for h in range(MM_TN // HEAD_DIM):
        x = acc[:, h * HEAD_DIM:(h + 1) * HEAD_DIM]
        parts.append(x * cos + pltpu.roll(x, HEAD_DIM // 2, axis=1) * sin_signed)
    return jnp.concatenate(parts, axis=1)


def _ep_rope_q(acc, extra, outs):
    outs[0][...] = (_rope_tile(acc, extra[0][...], extra[1][...]) * ATT_SCALE).astype(outs[0].dtype)


def _ep_rope_k(acc, extra, outs):
    outs[0][...] = _rope_tile(acc, extra[0][...], extra[1][...]).astype(outs[0].dtype)


def _ep_plain(acc, extra, outs):
    outs[0][...] = acc.astype(outs[0].dtype)


def _ep_sigmoid(acc, extra, outs):
    outs[0][...] = _sigmoid(acc).astype(outs[0].dtype)


def _ep_residual(acc, extra, outs):
    outs[0][...] = (extra[0][...] + acc).astype(outs[0].dtype)


def _attn_prompt_kernel(q_ref, kp_ref, kc_ref, vp_ref, vc_ref, o_ref, l_ref, *, tq):
    first_neg = jnp.where(pl.program_id(2) == 0, NEG, 0.0)
    qi = lax.broadcasted_iota(jnp.int32, (BACK, BACK), 0)
    kj = lax.broadcasted_iota(jnp.int32, (BACK, BACK), 1)
    prev_band = kj >= qi
    cur_band = kj <= qi
    for h in range(KV_HEADS):
        hs = slice(h * HEAD_DIM, (h + 1) * HEAD_DIM)
        for jb in range(tq // BACK):
            rows = slice(jb * BACK, (jb + 1) * BACK)
            qb = q_ref[0, rows, hs]
            if jb == 0:
                k_prev, v_prev = kp_ref[0, :, hs], vp_ref[0, :, hs]
                prev_neg = first_neg
            else:
                prow = slice((jb - 1) * BACK, jb * BACK)
                k_prev, v_prev = kc_ref[0, prow, hs], vc_ref[0, prow, hs]
                prev_neg = 0.0
            k_cur, v_cur = kc_ref[0, rows, hs], vc_ref[0, rows, hs]
            s_p = jnp.where(prev_band, _dot_nt(qb, k_prev), NEG) + prev_neg
            s_c = jnp.where(cur_band, _dot_nt(qb, k_cur), NEG)
            m = jnp.maximum(jnp.max(s_p, axis=-1, keepdims=True), jnp.max(s_c, axis=-1, keepdims=True))
            e_p = jnp.exp(s_p - m)
            e_c = jnp.exp(s_c - m)
            den = jnp.sum(e_p, axis=-1, keepdims=True) + jnp.sum(e_c, axis=-1, keepdims=True)
            o = (_dot(e_p, v_prev) + _dot(e_c, v_cur)) / den
            o_ref[0, rows, hs] = o
            l_ref[0, rows, hs] = jnp.broadcast_to(m + jnp.log(den), (BACK, HEAD_DIM))


def _attn_prompt_group(q, k, v, g, dil):
    b, s, _ = q.shape
    sub = s // dil
    tq = min(sub, 512)
    nq = sub // tq
    qv = q.reshape(b, sub, dil * Q_W)
    kv_ = k.reshape(b, sub, dil * KV_W)
    vv = v.reshape(b, sub, dil * KV_W)
    per = tq // BACK
    prev_map = lambda bi, r, i: (bi, jnp.maximum(i * per - 1, 0), r)
    cur_map = lambda bi, r, i: (bi, i, r)
    o, lse = pl.pallas_call(
        functools.partial(_attn_prompt_kernel, tq=tq),
        grid=(b, dil, nq),
        in_specs=[pl.BlockSpec((1, tq, KV_W), lambda bi, r, i: (bi, i, r * N_GROUPS + g)),
                  pl.BlockSpec((1, BACK, KV_W), prev_map),
                  pl.BlockSpec((1, tq, KV_W), cur_map),
                  pl.BlockSpec((1, BACK, KV_W), prev_map),
                  pl.BlockSpec((1, tq, KV_W), cur_map)],
        out_specs=[pl.BlockSpec((1, tq, KV_W), cur_map), pl.BlockSpec((1, tq, KV_W), cur_map)],
        out_shape=[jax.ShapeDtypeStruct((b, sub, dil * KV_W), F32)] * 2,
        compiler_params=_cparams("parallel", "parallel", "arbitrary"),
        name=f"attn_prompt_g{g}",
    )(qv, kv_, kv_, vv, vv)
    return o.reshape(b, s, KV_W), lse.reshape(b, s, KV_W)


def _attn_combine_kernel(o0, l0, o1, l1, o2, l2, out_ref):
    la, lb, lc = l0[...], l1[...], l2[...]
    m = jnp.maximum(jnp.maximum(la, lb), lc)
    wa, wb, wc = jnp.exp(la - m), jnp.exp(lb - m), jnp.exp(lc - m)
    out = (wa * o0[...] + wb * o1[...] + wc * o2[...]) / (wa + wb + wc)
    out_ref[...] = out.astype(out_ref.dtype)


def _attn_combine(parts, tm=512):
    m, w = parts[0].shape
    spec = pl.BlockSpec((tm, w), lambda i: (i, 0))
    return pl.pallas_call(
        _attn_combine_kernel,
        grid=(m // tm,),
        in_specs=[spec] * 6,
        out_specs=spec,
        out_shape=jax.ShapeDtypeStruct((m, w), BF16),
        compiler_params=_cparams("parallel"),
        name="attn_combine",
    )(*parts)


def _attn_sample_kernel(q_ref, kn_ref, vn_ref, k0_ref, v0_ref, k1_ref, v1_ref, k2_ref, v2_ref, o_ref, *, n_new):
    tok = lax.broadcasted_iota(jnp.int32, (BACK, KV_HEADS, 1), 0)
    kc_refs = (k0_ref, k1_ref, k2_ref)
    vc_refs = (v0_ref, v1_ref, v2_ref)
    for s in range(n_new):
        outs, lses = [], []
        for g in range(N_GROUPS):
            qh = q_ref[0, s, g * KV_HEADS:(g + 1) * KV_HEADS, :][None]
            if g == 0:
                kc, vc = kc_refs[g][0], vc_refs[g][0]
                new = slice(0, s + 1)
            else:
                kc, vc = kc_refs[g][0, :, s], vc_refs[g][0, :, s]
                new = slice(s, s + 1)
            sc = jnp.sum(kc * qh, axis=-1, keepdims=True)
            if g == 0:
                sc = jnp.where(tok >= s, sc, NEG)
            sn = jnp.sum(kn_ref[0, new] * qh, axis=-1, keepdims=True)
            m = jnp.maximum(jnp.max(sc, axis=0, keepdims=True), jnp.max(sn, axis=0, keepdims=True))
            p = jnp.exp(sc - m)
            pn = jnp.exp(sn - m)
            den = jnp.sum(p, axis=0, keepdims=True) + jnp.sum(pn, axis=0, keepdims=True)
            o = (jnp.sum(p * vc, axis=0, keepdims=True)
                 + jnp.sum(pn * vn_ref[0, new], axis=0, keepdims=True)) / den
            outs.append(o)
            lses.append(m + jnp.log(den))
        mm = jnp.maximum(jnp.maximum(lses[0], lses[1]), lses[2])
        ws = [jnp.exp(l - mm) for l in lses]
        comb = (ws[0] * outs[0] + ws[1] * outs[1] + ws[2] * outs[2]) / (ws[0] + ws[1] + ws[2])
        o_ref[0, s] = comb[0].astype(o_ref.dtype)


def _attn_sample(q, k_new, v_new, cache_k, cache_v):
    db, n_new = q.shape[0], q.shape[1]
    w_buf = cache_k.shape[1]
    assert w_buf == BACK * DILATIONS[-1] and n_new <= DILATIONS[1]
    new_spec = pl.BlockSpec((1, n_new, KV_HEADS, HEAD_DIM), lambda b: (b, 0, 0, 0))
    specs = [pl.BlockSpec((1, n_new, N_GROUPS * KV_HEADS, HEAD_DIM), lambda b: (b, 0, 0, 0)), new_spec, new_spec]
    args = [q, k_new, v_new]
    for dil in DILATIONS:
        sub = w_buf // dil
        last = sub // BACK - 1
        if dil == 1:
            shape = (db, sub, KV_HEADS, HEAD_DIM)
            spec = pl.BlockSpec((1, BACK, KV_HEADS, HEAD_DIM), functools.partial(lambda b, l: (b, l, 0, 0), l=last))
        else:
            shape = (db, sub, dil, KV_HEADS, HEAD_DIM)
            spec = pl.BlockSpec((1, BACK, n_new, KV_HEADS, HEAD_DIM),
                                functools.partial(lambda b, l: (b, l, 0, 0, 0), l=last))
        for c in (cache_k, cache_v):
            specs.append(spec)
            args.append(c.reshape(shape))
    return pl.pallas_call(
        functools.partial(_attn_sample_kernel, n_new=n_new),
        grid=(db,),
        in_specs=specs,
        out_specs=new_spec,
        out_shape=jax.ShapeDtypeStruct((db, n_new, KV_HEADS, HEAD_DIM), F32),
        compiler_params=_cparams("parallel"),
        name="attn_sample",
    )(*args)


def _head_indicator(width):
    l = lax.broadcasted_iota(jnp.int32, (width, LANE), 0)
    h = lax.broadcasted_iota(jnp.int32, (width, LANE), 1)
    return (_div_pow2(l, RW_N) == h).astype(F32)


def _head_indicator_t(width):
    h = lax.broadcasted_iota(jnp.int32, (LANE, width), 0)
    l = lax.broadcasted_iota(jnp.int32, (LANE, width), 1)
    return (_div_pow2(l, RW_N) == h).astype(F32)


def _rw_prep_kernel(ur, uk, uv, ul, fpr, fpk, fpv, fpl, fsr, fsk, fsv, fsl, mur, muk, muv, mul, ww, wa, wg, w0, a0,
                    kk_, ka, r_o, lw_o, k_o, v_o, kk_o, b_o, g_o, *, n_prompt_tiles, ds):
    is_sample = pl.program_id(0) >= n_prompt_tiles

    def mixed(u_ref, fp_ref, fs_ref, mu_ref):
        u = u_ref[...]
        row = lax.broadcasted_iota(jnp.int32, u.shape, 0)
        rolled = pltpu.roll(u, 1, axis=0)
        p_prompt = jnp.where(row == 0, fp_ref[0], rolled)
        p_sample = jnp.where(_mod_pow2(row, ds) == 0, fs_ref[...], rolled)
        prev = jnp.where(is_sample, p_sample, p_prompt)
        return u + (prev - u) * mu_ref[...]

    xr = mixed(ur, fpr, fsr, mur)
    xk = mixed(uk, fpk, fsk, muk)
    xv = mixed(uv, fpv, fsv, muv)
    xl = mixed(ul, fpl, fsl, mul)
    col = lax.broadcasted_iota(jnp.int32, xl.shape, 1)
    act = jnp.where(col < LORA_W, jnp.tanh(xl),
                    jnp.where(col < LORA_W + LORA_A, xl,
                              jnp.where(col < LORA_ALL, _sigmoid(xl), 0.0)))
    zw = _dot(act, ww[...])
    za = _dot(act, wa[...])
    zg = _dot(act, wg[...])
    lw = -RW_DECAY_SCALE * _sigmoid(w0[...] + zw)
    a = _sigmoid(a0[...] + za)
    kk = xk * kk_[...]
    width = kk.shape[1]
    ss = _dot_sel(kk * kk, _head_indicator(width), 2)
    inv = 1.0 / jnp.maximum(jnp.sqrt(ss), 1e-12)
    kkn = kk * _dot_sel(inv, _head_indicator_t(width), 2)
    r_o[...] = xr
    lw_o[...] = lw
    k_o[...] = xk * (1.0 + (a - 1.0) * ka[...])
    v_o[...] = xv
    kk_o[...] = kkn
    b_o[...] = kkn * a
    g_o[...] = zg


def _rw_prep(u_rkv, u_lora, shift, mp, sp, ds, mu, w_w, w_a, w_g, w0, a0, k_k, k_a, tm=512, tw=512):
    m = u_rkv.shape[0]
    nj = RW_W // tw
    n_p = mp // tm
    assert sp % tm == 0 and mp % tm == 0 and (m - mp) % tm == 0 and tm % ds == 0 and ds & (ds - 1) == 0
    mu_rkv = mu[:3 * RW_W].reshape(1, 3 * RW_W)
    lpad = LORA_PAD - LORA_ALL
    mu_l = jnp.pad(mu[3 * RW_W:], (0, lpad)).reshape(1, LORA_PAD)

    starts = jnp.arange(n_p, dtype=jnp.int32) * tm
    inside = ((starts % sp) != 0)[:, None]
    prev_idx = jnp.maximum(starts - 1, 0)

    def prompt_fix(u):
        return jnp.where(inside, jnp.take(u, prev_idx, axis=0), 0.0)[:, None, :]

    def sample_fix(first_rows):
        db, w = first_rows.shape
        return jnp.zeros((db, ds, w), F32).at[:, 0].set(first_rows).reshape(db * ds, w)

    fp_rkv, fp_l = prompt_fix(u_rkv), prompt_fix(u_lora)
    fs_rkv = sample_fix(shift[:, :3 * RW_W])
    fs_l = sample_fix(jnp.pad(shift[:, 3 * RW_W:], ((0, 0), (0, lpad))))

    def feat(off):
        return pl.BlockSpec((tm, tw), functools.partial(lambda i, j, o: (i, j + o), o=off))

    def fixp(off):
        return pl.BlockSpec((1, 1, tw), functools.partial(lambda i, j, o: (jnp.minimum(i, n_p - 1), 0, j + o), o=off))

    def fixs(off):
        return pl.BlockSpec((tm, tw), functools.partial(lambda i, j, o: (jnp.maximum(i - n_p, 0), j + o), o=off))

    def vec(off):
        return pl.BlockSpec((1, tw), functools.partial(lambda i, j, o: (0, j + o), o=off))

    lspec = pl.BlockSpec((tm, LORA_PAD), lambda i, j: (i, 0))
    lfixp = pl.BlockSpec((1, 1, LORA_PAD), lambda i, j: (jnp.minimum(i, n_p - 1), 0, 0))
    lfixs = pl.BlockSpec((tm, LORA_PAD), lambda i, j: (jnp.maximum(i - n_p, 0), 0))
    wspec = pl.BlockSpec((LORA_PAD, tw), lambda i, j: (0, j))
    in_specs = [feat(0), feat(nj), feat(2 * nj), lspec,
                fixp(0), fixp(nj), fixp(2 * nj), lfixp,
                fixs(0), fixs(nj), fixs(2 * nj), lfixs,
                vec(0), vec(nj), vec(2 * nj), pl.BlockSpec((1, LORA_PAD), lambda i, j: (0, 0)),
                wspec, wspec, wspec, vec(0), vec(0), vec(0), vec(0)]
    out_spec = pl.BlockSpec((tm, tw), lambda i, j: (i, j))
    outs = pl.pallas_call(
        functools.partial(_rw_prep_kernel, n_prompt_tiles=n_p, ds=ds),
        grid=(m // tm, nj),
        in_specs=in_specs,
        out_specs=[out_spec] * 7,
        out_shape=[jax.ShapeDtypeStruct((m, RW_W), F32)] * 7,
        compiler_params=_cparams("parallel", "arbitrary"),
        name="rw_prep",
    )(u_rkv, u_rkv, u_rkv, u_lora, fp_rkv, fp_rkv, fp_rkv, fp_l, fs_rkv, fs_rkv, fs_rkv, fs_l,
      mu_rkv, mu_rkv, mu_rkv, mu_l,
      w_w, w_a, w_g, w0.reshape(1, RW_W), a0.reshape(1, RW_W), k_k.reshape(1, RW_W), k_a.reshape(1, RW_W))
    return outs


def _rw_scan_kernel(r_ref, lw_ref, k_ref, v_ref, kk_ref, b_ref, s_in, y_ref, s_out, s_bd, *, chunk):
    t_id = pl.program_id(1)
    n_t = pl.num_programs(1)
    gw = RW_GROUP * RW_N
    n_groups = RW_W // gw
    rows = RW_GROUP * chunk

    ri = lax.broadcasted_iota(jnp.int32, (gw, gw), 0)
    ci = lax.broadcasted_iota(jnp.int32, (gw, gw), 1)
    state_mask = _div_pow2(ri, RW_N) == _div_pow2(ci, RW_N)

    @pl.when(t_id == 0)
    def _():
        kr = lax.broadcasted_iota(jnp.int32, (RW_N, gw), 0)
        kc = lax.broadcasted_iota(jnp.int32, (RW_N, gw), 1)
        spread = (_mod_pow2(kc, RW_N) == kr).astype(F32)
        for g in range(n_groups):
            tiled = _dot_sel(s_in[0, g * gw:(g + 1) * gw, :], spread, 3)
            s_bd[g] = jnp.where(state_mask, tiled, 0.0)

    ti = lax.broadcasted_iota(jnp.int32, (chunk, chunk), 0)
    tj = lax.broadcasted_iota(jnp.int32, (chunk, chunk), 1)
    lw = lw_ref[0]
    cum = _sel_dot((tj <= ti).astype(F32), lw, 3)
    e_pos = jnp.exp(cum)
    e_neg = jnp.exp(-cum)
    e_prev = jnp.exp(cum - lw)
    cum_last = cum[chunk - 1:chunk, :]
    e_rem = jnp.exp(cum_last - cum)
    c_all = jnp.exp(cum_last)
    kt = kk_ref[0] * e_prev
    bt = b_ref[0] * e_neg
    k2t = k_ref[0] * e_neg
    rt = r_ref[0] * e_pos
    btc = b_ref[0] * e_rem
    k2tc = k_ref[0] * e_rem
    vv = v_ref[0]

    sr = lax.broadcasted_iota(jnp.int32, (rows, gw), 0)
    sc = lax.broadcasted_iota(jnp.int32, (rows, gw), 1)
    stack_mask = _div_pow2(sr, chunk) == _div_pow2(sc, RW_N)
    ar = lax.broadcasted_iota(jnp.int32, (rows, rows), 0)
    ac = lax.broadcasted_iota(jnp.int32, (rows, rows), 1)
    strict = ar > ac
    incl = ar >= ac
    eye = (ar == ac).astype(F32)

    def stack(x):
        return jnp.where(stack_mask, jnp.concatenate([x] * RW_GROUP, axis=0), 0.0).astype(BF16)

    gs = range(n_groups)
    lanes = [slice(g * gw, (g + 1) * gw) for g in gs]
    kr_s = [jnp.concatenate([stack(kt[:, ls]), stack(rt[:, ls])], axis=0) for ls in lanes]
    bt_s = [stack(bt[:, ls]) for ls in lanes]
    k2t_s = [stack(k2t[:, ls]) for ls in lanes]
    v_s = [stack(vv[:, ls]) for ls in lanes]
    s0 = [s_bd[g] for g in gs]
    p_b = [_dot_nt(kr_s[g], bt_s[g]) for g in gs]
    p_k = [_dot_nt(kr_s[g], k2t_s[g]) for g in gs]
    p_s = [_dot_nt(kr_s[g], s0[g]) for g in gs]
    a_b = [jnp.where(strict, p_b[g][:rows], 0.0) for g in gs]
    r_b = [jnp.where(incl, p_b[g][rows:], 0.0) for g in gs]
    ar_k = [jnp.concatenate([jnp.where(strict, p_k[g][:rows], 0.0), jnp.where(incl, p_k[g][rows:], 0.0)], axis=0)
            for g in gs]
    p_v = [_dot(ar_k[g], v_s[g]) for g in gs]
    rhs = [p_s[g][:rows] + p_v[g][:rows] for g in gs]
    inv = [eye - a_b[g] for g in gs]
    pw = a_b
    n = 2
    while n < chunk:
        pw = [_dot(pw[g], pw[g]) for g in gs]
        inv = [inv[g] + _dot(inv[g], pw[g]) for g in gs]
        n *= 2
    u = [-_dot(inv[g], rhs[g]) for g in gs]
    y_bd = [p_s[g][rows:] + p_v[g][rows:] + _dot(r_b[g], u[g]) for g in gs]
    for g in gs:
        y = y_bd[g][0:chunk]
        for h in range(1, RW_GROUP):
            y = y + y_bd[g][h * chunk:(h + 1) * chunk]
        y_ref[0, :, lanes[g]] = y
    for g in gs:
        uv = jnp.concatenate([u[g].astype(BF16), v_s[g]], axis=0)
        bk = jnp.concatenate([stack(btc[:, lanes[g]]), stack(k2tc[:, lanes[g]])], axis=0)
        s_bd[g] = s0[g] * c_all[:, lanes[g]] + _dot_tn(uv, bk)

    @pl.when(t_id == n_t - 1)
    def _():
        gr = lax.broadcasted_iota(jnp.int32, (gw, RW_N), 0)
        gc = lax.broadcasted_iota(jnp.int32, (gw, RW_N), 1)
        gather = (_mod_pow2(gr, RW_N) == gc).astype(F32)
        for g in range(n_groups):
            s_out[0, g * gw:(g + 1) * gw, :] = _dot_sel(s_bd[g], gather, 3)


def _rw_scan(r, lw, k, v, kk, b, state, chunk, bn, t):
    gw = RW_GROUP * RW_N
    nt = t // chunk
    seq = pl.BlockSpec((1, chunk, RW_W), lambda bi, ti: (0, bi * nt + ti, 0))
    st = pl.BlockSpec((1, RW_W, RW_N), lambda bi, ti: (bi, 0, 0))
    y, s_new = pl.pallas_call(
        functools.partial(_rw_scan_kernel, chunk=chunk),
        grid=(bn, nt),
        in_specs=[seq] * 6 + [st],
        out_specs=[seq, st],
        out_shape=[jax.ShapeDtypeStruct((1, bn * t, RW_W), F32), jax.ShapeDtypeStruct((bn, RW_W, RW_N), F32)],
        scratch_shapes=[pltpu.VMEM((RW_W // gw, gw, gw), F32)],
        compiler_params=_cparams("parallel", "arbitrary"),
        name=f"rw_scan_c{chunk}",
    )(*(a[None] for a in (r, lw, k, v, kk, b)), state.reshape(bn, RW_W, RW_N))
    return y[0], s_new.reshape(bn, RW_HEADS, RW_N, RW_N)


def _rw_step_kernel(r_ref, lw_ref, k_ref, v_ref, kk_ref, b_ref, s_in, y_ref, s_out, *, n_steps):
    sub = 8
    for t in range(n_steps):
        w = jnp.exp(lw_ref[t])
        kap, bb, k2, r = kk_ref[t], b_ref[t], k_ref[t], r_ref[t]
        src = s_in if t == 0 else s_out

        def body(blk, c, t=t, w=w, kap=kap, bb=bb, k2=k2, r=r, src=src):
            base = pl.multiple_of(blk * sub, sub)
            v_rows = v_ref[t, pl.ds(base, sub), :]
            y_rows = []
            for j in range(sub):
                sv = src[0, base + j]
                sa = -jnp.sum(sv * kap, axis=0, keepdims=True)
                sn = sv * w + sa * bb + v_rows[j:j + 1] * k2
                s_out[0, base + j] = sn
                y_rows.append(jnp.sum(sn * r, axis=0, keepdims=True))
            y_ref[t, pl.ds(base, sub), :] = jnp.concatenate(y_rows, axis=0)
            return c

        lax.fori_loop(0, RW_N // sub, body, 0)


def _rw_steps(r, lw, k, v, kk, b, state):
    n_steps, _, nb = r.shape
    assert nb % LANE == 0
    seq = pl.BlockSpec((n_steps, RW_N, nb), lambda h: (0, h, 0))
    st = pl.BlockSpec((1, RW_N, RW_N, nb), lambda h: (h, 0, 0, 0))
    return pl.pallas_call(
        functools.partial(_rw_step_kernel, n_steps=n_steps),
        grid=(RW_HEADS,),
        in_specs=[seq] * 6 + [st],
        out_specs=[seq, st],
        out_shape=[jax.ShapeDtypeStruct((n_steps, RW_W, nb), F32),
                   jax.ShapeDtypeStruct((RW_HEADS, RW_N, RW_N, nb), F32)],
        compiler_params=_cparams("parallel"),
        name="rw_steps",
    )(r, lw, k, v, kk, b, state)


def _rw_post_kernel(y_ref, r_ref, k_ref, v_ref, g_ref, lnw, lnb, rk, o_ref):
    y = y_ref[...]
    width = y.shape[1]
    ind, ind_t = _head_indicator(width), _head_indicator_t(width)
    mean = _dot_sel(_dot_sel(y, ind, 2) * (1.0 / RW_N), ind_t, 2)
    d = y - mean
    var_h = _dot_sel(d * d, ind, 2) * (1.0 / RW_N)
    rstd = _dot_sel(lax.rsqrt(var_h + RW_GN_EPS), ind_t, 2)
    yn = d * rstd * lnw[...] + lnb[...]
    bonus = _dot_sel(_dot_sel(r_ref[...] * k_ref[...] * rk[...], ind, 2), ind_t, 2) * v_ref[...]
    o_ref[...] = ((yn + bonus) * g_ref[...]).astype(o_ref.dtype)


def _rw_post(y, r, k, v, g, ln_w, ln_b, r_k, tm=512, tw=512):
    m = y.shape[0]
    feat = pl.BlockSpec((tm, tw), lambda i, j: (i, j))
    vec = pl.BlockSpec((1, tw), lambda i, j: (0, j))
    return pl.pallas_call(
        _rw_post_kernel,
        grid=(m // tm, RW_W // tw),
        in_specs=[feat] * 5 + [vec] * 3,
        out_specs=feat,
        out_shape=jax.ShapeDtypeStruct((m, RW_W), BF16),
        compiler_params=_cparams("parallel", "parallel"),
        name="rw_post",
    )(y, r, k, v, g, ln_w.reshape(1, RW_W), ln_b.reshape(1, RW_W), r_k.reshape(1, RW_W))


def _merge_kernel(oa_ref, ob_ref, wa_ref, wb_ref, ga_ref, gb_ref, out_ref, wa_bf, wb_bf):
    @pl.when(pl.program_id(1) == 0)
    def _():
        wa_bf[...] = wa_ref[...].astype(BF16)
        wb_bf[...] = wb_ref[...].astype(BF16)

    ya = jnp.dot(oa_ref[...], wa_bf[...], preferred_element_type=F32)
    yb = jnp.dot(ob_ref[...], wb_bf[...], preferred_element_type=F32)
    out_ref[...] = (ga_ref[...].astype(F32) * ya + gb_ref[...].astype(F32) * yb).astype(out_ref.dtype)


def _merge(o_a, o_b, w_a, w_b, gates, tm):
    m = o_a.shape[0]
    tn = MM_TN
    nj = D_MODEL // tn
    return pl.pallas_call(
        _merge_kernel,
        grid=(nj, m // tm),
        in_specs=[pl.BlockSpec((tm, KV_W), lambda j, i: (i, 0)),
                  pl.BlockSpec((tm, RW_W), lambda j, i: (i, 0)),
                  pl.BlockSpec((KV_W, tn), lambda j, i: (0, j)),
                  pl.BlockSpec((RW_W, tn), lambda j, i: (0, j)),
                  pl.BlockSpec((tm, tn), lambda j, i: (i, j)),
                  pl.BlockSpec((tm, tn), lambda j, i: (i, j + nj))],
        out_specs=pl.BlockSpec((tm, tn), lambda j, i: (i, j)),
        out_shape=jax.ShapeDtypeStruct((m, D_MODEL), BF16),
        scratch_shapes=[pltpu.VMEM((KV_W, tn), BF16), pltpu.VMEM((RW_W, tn), BF16)],
        compiler_params=_cparams("parallel", "arbitrary"),
        name="merge",
    )(o_a, o_b, w_a, w_b, gates, gates)


def _router_kernel(x_ref, g_ref, wr_ref, br_ref, h_ref, id_ref, wt_ref):
    x = x_ref[...]
    ms = jnp.mean(x * x, axis=-1, keepdims=True)
    h = x * lax.rsqrt(ms + NORM_EPS) * g_ref[...]
    h_ref[...] = h.astype(h_ref.dtype)
    logits = _dot_hi(h, wr_ref[...]) + br_ref[...]
    lane = lax.broadcasted_iota(jnp.int32, logits.shape, 1)
    lane_f = lane.astype(F32)
    gmask = lane < MOE_GROUPS
    gl = jnp.where(gmask, logits, NEG)
    gm = jnp.max(gl, axis=-1, keepdims=True)
    gi = jnp.min(jnp.where(gl == gm, lane_f, float(LANE)), axis=-1, keepdims=True)
    g_prob = 1.0 / jnp.sum(jnp.where(gmask, jnp.exp(gl - gm), 0.0), axis=-1, keepdims=True)
    lo = MOE_GROUPS + gi * MOE_PER_GROUP
    emask = jnp.logical_and(lane_f >= lo, lane_f < lo + MOE_PER_GROUP)
    el = jnp.where(emask, logits, NEG)
    m1 = jnp.max(el, axis=-1, keepdims=True)
    i1 = jnp.min(jnp.where(el == m1, lane_f, float(LANE)), axis=-1, keepdims=True)
    el2 = jnp.where(lane_f == i1, NEG, el)
    m2 = jnp.max(el2, axis=-1, keepdims=True)
    i2 = jnp.min(jnp.where(el2 == m2, lane_f, float(LANE)), axis=-1, keepdims=True)
    t = jnp.exp(m2 - m1)
    w1 = g_prob / (1.0 + t)
    w2 = g_prob * t / (1.0 + t)
    ids = jnp.where(lane == 0, i1 - MOE_GROUPS, jnp.where(lane == 1, i2 - MOE_GROUPS, 0.0))
    id_ref[...] = ids.astype(jnp.int32)
    wt_ref[...] = jnp.where(lane == 0, w1, jnp.where(lane == 1, w2, 0.0))


def _router(x, g, w_router, b_router, tm=256):
    m, d = x.shape
    row = pl.BlockSpec((tm, d), lambda i: (i, 0))
    small = pl.BlockSpec((tm, LANE), lambda i: (i, 0))
    return pl.pallas_call(
        _router_kernel,
        grid=(m // tm,),
        in_specs=[row, pl.BlockSpec((1, d), lambda i: (0, 0)),
                  pl.BlockSpec((d, LANE), lambda i: (0, 0)), pl.BlockSpec((1, LANE), lambda i: (0, 0))],
        out_specs=[row, small, small],
        out_shape=[jax.ShapeDtypeStruct((m, d), F32), jax.ShapeDtypeStruct((m, LANE), jnp.int32),
                   jax.ShapeDtypeStruct((m, LANE), F32)],
        compiler_params=_cparams("parallel"),
        name="ffn_norm_router",
    )(x, g.reshape(1, d), w_router, b_router)


def _row_copy(src_hbm, row, dst_vmem, slot, sem):
    return pltpu.make_async_copy(src_hbm.at[pl.ds(row, 1)], dst_vmem.at[pl.ds(slot, 1)], sem)


def _moe_kernel(te_ref, tv_ref, last_ref, tok_ref, nv_ref, h_hbm, wg_ref, wu_ref, wd_ref, rw_ref, o_ref,
                xbuf, gacc, uacc, hid, sems):
    i = pl.program_id(0)
    j = pl.program_id(1)
    valid = tv_ref[i] > 0
    slot = i & 1

    def gather(tile, slot_, start):
        def body(r, c):
            row = tok_ref[tile * MOE_TM + r] if start else 0
            cp = _row_copy(h_hbm, row, xbuf.at[slot_], r, sems.at[slot_])
            if start:
                cp.start()
            else:
                cp.wait()
            return c

        lax.fori_loop(0, nv_ref[tile], body, 0)

    @pl.when(j == 0)
    def _():
        @pl.when(i == 0)
        def _():
            xbuf[...] = jnp.zeros_like(xbuf)
            gather(0, 0, True)

        gather(i, slot, False)

        @pl.when(i + 1 < pl.num_programs(0))
        def _():
            gather(i + 1, 1 - slot, True)

    for kt in range(MOE_NK):
        @pl.when(jnp.logical_and(valid, j == kt))
        def _(kt=kt):
            x = xbuf[slot, :, kt * MOE_TK:(kt + 1) * MOE_TK].astype(BF16)
            g = jnp.dot(x, wg_ref[0].astype(BF16), preferred_element_type=F32)
            u = jnp.dot(x, wu_ref[0].astype(BF16), preferred_element_type=F32)
            if kt > 0:
                g = g + gacc[...]
                u = u + uacc[...]
            if kt < MOE_NK - 1:
                gacc[...] = g
                uacc[...] = u
            else:
                hid[...] = (g * _sigmoid(g) * u).astype(BF16)

    @pl.when(j >= MOE_NK)
    def _():
        @pl.when(valid)
        def _():
            o_ref[...] = jnp.dot(hid[...], wd_ref[0].astype(BF16), preferred_element_type=F32) * rw_ref[...]

        @pl.when(jnp.logical_not(valid))
        def _():
            o_ref[...] = jnp.zeros_like(o_ref)


def _moe_experts(h, slot_tok, tile_rows, row_w, tile_expert, tile_valid, last_tile, w_gate, w_up, w_down):
    d = h.shape[1]
    p = slot_tok.shape[0]
    n_tiles = p // MOE_TM

    def k_idx(i, j, tv):
        return jnp.where(tv[i] > 0, jnp.minimum(j, MOE_NK - 1), MOE_NK - 1)

    def n_idx(i, j, tv):
        return jnp.where(tv[i] > 0, jnp.maximum(j - MOE_NK, 0), MOE_NN - 1)

    grid_spec = pltpu.PrefetchScalarGridSpec(
        num_scalar_prefetch=5,
        grid=(n_tiles, MOE_NK + MOE_NN),
        in_specs=[pl.BlockSpec(memory_space=pl.ANY),
                  pl.BlockSpec((1, MOE_TK, EXPERT_FF), lambda i, j, te, tv, last, tok, nv: (te[i], k_idx(i, j, tv), 0)),
                  pl.BlockSpec((1, MOE_TK, EXPERT_FF), lambda i, j, te, tv, last, tok, nv: (te[i], k_idx(i, j, tv), 0)),
                  pl.BlockSpec((1, EXPERT_FF, MOE_TN), lambda i, j, te, tv, last, tok, nv: (te[i], 0, n_idx(i, j, tv))),
                  pl.BlockSpec((MOE_TM, 1), lambda i, j, te, tv, last, tok, nv: (jnp.minimum(i, last[0]), 0))],
        out_specs=pl.BlockSpec((MOE_TM, MOE_TN),
                               lambda i, j, te, tv, last, tok, nv: (i, jnp.maximum(j - MOE_NK, 0))),
        scratch_shapes=[pltpu.VMEM((2, MOE_TM, d), F32),
                        pltpu.VMEM((MOE_TM, EXPERT_FF), F32), pltpu.VMEM((MOE_TM, EXPERT_FF), F32),
                        pltpu.VMEM((MOE_TM, EXPERT_FF), BF16),
                        pltpu.SemaphoreType.DMA((2,))],
    )
    return pl.pallas_call(
        _moe_kernel,
        grid_spec=grid_spec,
        out_shape=jax.ShapeDtypeStruct((p, d), F32),
        compiler_params=_cparams("arbitrary", "arbitrary"),
        name="moe_experts",
    )(tile_expert, tile_valid, last_tile, slot_tok, tile_rows, h, w_gate, w_up, w_down, row_w)


def _dispatch_plan(ids, wts):
    t = ids.shape[0]
    a = t * 2
    flat_e = ids.reshape(a)
    onehot = (flat_e[:, None] == jnp.arange(N_EXPERTS, dtype=jnp.int32)[None, :]).astype(jnp.int32)
    csum = jnp.cumsum(onehot, axis=0)
    counts = csum[-1]
    rank = jnp.take_along_axis(csum, flat_e[:, None], axis=1)[:, 0] - 1
    tiles_per = (counts + MOE_TM - 1) // MOE_TM
    tile_end = jnp.cumsum(tiles_per)
    tile_start = tile_end - tiles_per
    dest = tile_start[flat_e] * MOE_TM + rank
    n_tiles = -(-a // MOE_TM) + N_EXPERTS
    total = tile_end[-1]
    slot_tok = jnp.zeros((n_tiles * MOE_TM,), jnp.int32).at[dest].set(jnp.arange(a, dtype=jnp.int32) // 2)
    slot_w = jnp.zeros((n_tiles * MOE_TM,), F32).at[dest].set(wts.reshape(a))
    tile_ids = jnp.arange(n_tiles, dtype=jnp.int32)
    tile_valid = (tile_ids < total).astype(jnp.int32)
    tile_expert = jnp.searchsorted(tile_end, jnp.minimum(tile_ids, total - 1), side="right").astype(jnp.int32)
    tile_expert = jnp.minimum(tile_expert, N_EXPERTS - 1)
    tile_pos = tile_ids - tile_start[tile_expert]
    tile_rows = jnp.where(tile_valid > 0, jnp.clip(counts[tile_expert] - tile_pos * MOE_TM, 0, MOE_TM), 0)
    last_tile = (total - 1).astype(jnp.int32).reshape(1)
    return slot_tok, slot_w, dest, tile_expert, tile_valid, tile_rows.astype(jnp.int32), last_tile


def _final_kernel(dest_ref, x_ref, y_hbm, g_ref, o_ref, buf, sems, *, row_off, tm):
    i = pl.program_id(0)
    slot = i & 1

    def gather(tile, slot_, start):
        base = (row_off + tile * tm) * 2

        def body(r, c):
            row = dest_ref[base + r] if start else 0
            cp = _row_copy(y_hbm, row, buf.at[slot_], (r & 1) * tm + (r >> 1), sems.at[slot_])
            if start:
                cp.start()
            else:
                cp.wait()
            return c

        lax.fori_loop(0, 2 * tm, body, 0)

    @pl.when(i == 0)
    def _():
        gather(0, 0, True)

    gather(i, slot, False)

    @pl.when(i + 1 < pl.num_programs(0))
    def _():
        gather(i + 1, 1 - slot, True)

    x = x_ref[...] + (buf[slot, 0:tm, :] + buf[slot, tm:2 * tm, :])
    ms = jnp.mean(x * x, axis=-1, keepdims=True)
    o_ref[...] = x * lax.rsqrt(ms + NORM_EPS) * g_ref[...]


def _final(x, y_rows, dest, g, row_off, rows, tm=128):
    d = x.shape[1]
    off = row_off // tm
    grid_spec = pltpu.PrefetchScalarGridSpec(
        num_scalar_prefetch=1,
        grid=(rows // tm,),
        in_specs=[pl.BlockSpec((tm, d), lambda i, dst: (i + off, 0)),
                  pl.BlockSpec(memory_space=pl.ANY),
                  pl.BlockSpec((1, d), lambda i, dst: (0, 0))],
        out_specs=pl.BlockSpec((tm, d), lambda i, dst: (i, 0)),
        scratch_shapes=[pltpu.VMEM((2, 2 * tm, d), F32), pltpu.SemaphoreType.DMA((2,))],
    )
    return pl.pallas_call(
        functools.partial(_final_kernel, row_off=row_off, tm=tm),
        grid_spec=grid_spec,
        out_shape=jax.ShapeDtypeStruct((rows, d), F32),
        compiler_params=_cparams("arbitrary"),
        name="final_norm",
    )(dest, x, y_rows, g.reshape(1, d))


def _rope_tables(pos):
    half = HEAD_DIM // 2
    inv_freq = ROPE_THETA ** (-jnp.arange(half, dtype=F32) / half)
    ang = pos.astype(F32)[:, None] * inv_freq[None, :]
    cos, sin = jnp.cos(ang), jnp.sin(ang)
    return jnp.concatenate([cos, cos], axis=-1), jnp.concatenate([-sin, sin], axis=-1)


def kernel(x_prompt, x_sample, cache_k, cache_v, state_shift, state_wkv, norm_mix_g, w_in, rw_mu, rw_w0, rw_w2,
           rw_a0, rw_a2, rw_g2, rw_k_k, rw_k_a, rw_r_k, rw_ln_w, rw_ln_b, w_branch_a, w_branch_b, w_out,
           norm_ffn_g, router_group_w, router_group_b, router_expert_w, router_expert_b, exp_gate, exp_up,
           exp_down, norm_final_g):
    assert w_in.shape[0] == 1, "single-layer trunk"
    bp, sp, d = x_prompt.shape
    db, ds, _ = x_sample.shape
    mp, ms_ = bp * sp, db * ds
    m = mp + ms_
    past = cache_k.shape[2]
    tm_mm = m // 8

    x_all = jnp.concatenate([x_prompt.reshape(mp, d), x_sample.reshape(ms_, d)], axis=0)
    h = _rmsnorm(x_all, norm_mix_g[0], BF16)

    w_in_t = jnp.swapaxes(w_in, 1, 2).reshape(w_in.shape[2], d)
    cos_p, sin_p = _rope_tables(jnp.arange(sp, dtype=jnp.int32))
    cos_s, sin_s = _rope_tables(past + jnp.arange(ds, dtype=jnp.int32))
    cos = jnp.concatenate([jnp.tile(cos_p, (bp, 1)), jnp.tile(cos_s, (db, 1))], axis=0)
    sin = jnp.concatenate([jnp.tile(sin_p, (bp, 1)), jnp.tile(sin_s, (db, 1))], axis=0)
    rope_extras = [(cos, "row", 0), (sin, "row", 0)]
    def inproj(w, n_cols, col, epilogue, extras, dtype, name):
        return _matmul(h, w, n_cols, col // MM_TN, epilogue, extras, [dtype], tm_mm, name, w_transposed=True)[0]

    q = inproj(w_in_t, Q_W, 0, _ep_rope_q, rope_extras, BF16, "inproj_q")
    k = inproj(w_in_t, KV_W, COL_K, _ep_rope_k, rope_extras, F32, "inproj_k")
    v = inproj(w_in_t, KV_W, COL_V, _ep_plain, [], F32, "inproj_v")
    u_rkv = inproj(w_in_t, 3 * RW_W, COL_RW, _ep_plain, [], F32, "inproj_rkv")
    u_lora = inproj(w_in_t, LORA_PAD, COL_LORA, _ep_plain, [], F32, "inproj_lora")
    gates = inproj(w_in_t[COL_GATE:], 2 * d, 0, _ep_sigmoid, [], BF16, "inproj_gates")

    qp = q[:mp].reshape(bp, sp, Q_W)
    kp = k[:mp].reshape(bp, sp, KV_W)
    vp = v[:mp].reshape(bp, sp, KV_W)
    parts = []
    for g, dil in enumerate(DILATIONS):
        o_g, l_g = _attn_prompt_group(qp, kp, vp, g, dil)
        parts += [o_g.reshape(mp, KV_W), l_g.reshape(mp, KV_W)]
    oa_p = _attn_combine(parts)
    ks = k[mp:].reshape(db, ds, KV_HEADS, HEAD_DIM)
    vs = v[mp:].reshape(db, ds, KV_HEADS, HEAD_DIM)
    oa_s = _attn_sample(q[mp:].astype(F32).reshape(db, ds, N_GROUPS * KV_HEADS, HEAD_DIM), ks, vs,
                        cache_k.reshape(db, past, KV_HEADS, HEAD_DIM), cache_v.reshape(db, past, KV_HEADS, HEAD_DIM))
    o_a = jnp.concatenate([oa_p, oa_s.reshape(ms_, KV_W).astype(BF16)], axis=0)

    zl = functools.partial(jnp.zeros, dtype=F32)
    w_w = zl((LORA_PAD, RW_W)).at[:LORA_W].set(rw_w2[0])
    w_a = zl((LORA_PAD, RW_W)).at[LORA_W:LORA_W + LORA_A].set(rw_a2[0])
    w_g = zl((LORA_PAD, RW_W)).at[LORA_W + LORA_A:LORA_ALL].set(rw_g2[0])
    r_, lw_, k2_, v_, kk_, b_, g_ = _rw_prep(u_rkv, u_lora, state_shift[0], mp, sp, ds, rw_mu[0], w_w, w_a, w_g,
                                             rw_w0[0], rw_a0[0], rw_k_k[0], rw_k_a[0])

    scan_in = (r_, lw_, k2_, v_, kk_, b_)
    y_p, wkv_p = _rw_scan(*scan_in, jnp.zeros((bp, RW_HEADS, RW_N, RW_N), F32), RW_CHUNK, bp, sp)
    samp = [a[mp:].reshape(db, ds, RW_W).transpose(1, 2, 0) for a in scan_in]
    y_s, wkv_s = _rw_steps(*samp, jnp.transpose(state_wkv[0], (1, 2, 3, 0)))
    wkv_s = jnp.transpose(wkv_s, (3, 0, 1, 2))
    y_rw = jnp.concatenate([y_p, y_s.transpose(2, 0, 1).reshape(ms_, RW_W)], axis=0)
    o_b = _rw_post(y_rw, r_, k2_, v_, g_, rw_ln_w[0], rw_ln_b[0], rw_r_k[0])

    merged = _merge(o_a, o_b, w_branch_a[0], w_branch_b[0], gates, tm_mm)
    (x1,) = _matmul(merged, w_out.reshape(d, d), d, 0, _ep_residual, [(x_all, "tile", 0)], [F32], tm_mm, "out_proj")

    w_router = jnp.concatenate([router_group_w[0], router_expert_w[0],
                                jnp.zeros((d, LANE - MOE_GROUPS - N_EXPERTS), F32)], axis=1)
    b_router = jnp.concatenate([router_group_b[0], router_expert_b[0],
                                jnp.zeros((LANE - MOE_GROUPS - N_EXPERTS,), F32)]).reshape(1, LANE)
    h2, ids, wts = _router(x1, norm_ffn_g[0], w_router, b_router)
    slot_tok, slot_w, dest, tile_expert, tile_valid, tile_rows, last_tile = _dispatch_plan(ids[:, :2], wts[:, :2])
    yb = _moe_experts(h2, slot_tok, tile_rows, slot_w[:, None], tile_expert, tile_valid, last_tile,
                      exp_gate[0], exp_up[0], exp_down[0])
    y_prompt = _final(x1, yb, dest, norm_final_g, 0, mp).reshape(bp, sp, d)
    y_sample = _final(x1, yb, dest, norm_final_g, mp, ms_).reshape(db, ds, d)

    keep = min(BACK * DILATIONS[-1], sp)
    k_prompt = k[:mp].reshape(1, bp, sp, KV_HEADS, HEAD_DIM)[:, :, sp - keep:]
    v_prompt = v[:mp].reshape(1, bp, sp, KV_HEADS, HEAD_DIM)[:, :, sp - keep:]
    k_sample = ks.reshape(1, db, ds, KV_HEADS, HEAD_DIM)
    v_sample = vs.reshape(1, db, ds, KV_HEADS, HEAD_DIM)

    def last_rows(lo, nb, t):
        rows = lo + t - 1 + t * jnp.arange(nb, dtype=jnp.int32)
        a = jnp.take(u_rkv, rows, axis=0)
        b = jnp.take(u_lora, rows, axis=0)[:, :LORA_ALL]
        return jnp.concatenate([a, b], axis=-1)[None]

    return (y_prompt, y_sample, k_prompt, v_prompt, k_sample, v_sample,
            last_rows(0, bp, sp), last_rows(mp, db, ds), wkv_p[None], wkv_s[None])
```

```python
import functools
import math

import jax
import jax.numpy as jnp
from jax import lax
from jax.experimental import pallas as pl
from jax.experimental.pallas import tpu as pltpu

F32 = jnp.float32
BF16 = jnp.bfloat16
HI = lax.Precision.HIGHEST

D_MODEL = 4096
NORM_EPS = 1e-6
HEAD_DIM = 128
KV_HEADS = 8
DILATIONS = (1, 4, 16)
BACK = 128
N_GROUPS = 3
ROPE_THETA = 10000.0
ATT_SCALE = HEAD_DIM ** -0.5
Q_W = N_GROUPS * KV_HEADS * HEAD_DIM
KV_W = KV_HEADS * HEAD_DIM
RW_N = 64
RW_HEADS = 32
RW_W = RW_HEADS * RW_N
LORA_W, LORA_A, LORA_G = 96, 96, 256
LORA_ALL = LORA_W + LORA_A + LORA_G
RW_FEAT = 3 * RW_W + LORA_ALL
RW_GN_EPS = 64e-5
RW_DECAY_SCALE = math.exp(-0.5)
COL_K = Q_W
COL_V = Q_W + KV_W
COL_RW = Q_W + 2 * KV_W
COL_LORA = COL_RW + 3 * RW_W
COL_GATE = COL_RW + RW_FEAT
MOE_GROUPS = 8
MOE_PER_GROUP = 8
N_EXPERTS = 64
EXPERT_FF = 1024

LANE = 128
VMEM_LIMIT_BYTES = 56 * 1024 * 1024
MM_TN = 512
LORA_PAD = 512
MOE_TM = 384
MOE_TF = 256
RW_CHUNK = 64
RW_GROUP = 4
NEG = -1e30


def _cparams(*sem):
    return pltpu.CompilerParams(dimension_semantics=sem, vmem_limit_bytes=VMEM_LIMIT_BYTES)


def _dot(a, b):
    return jnp.dot(a.astype(BF16), b.astype(BF16), preferred_element_type=F32)


def _dot_nt(a, b):
    return lax.dot_general(a.astype(BF16), b.astype(BF16), (((1,), (1,)), ((), ())),
                           preferred_element_type=F32)


def _dot_tn(a, b):
    return lax.dot_general(a.astype(BF16), b.astype(BF16), (((0,), (0,)), ((), ())),
                           preferred_element_type=F32)


def _dot_hi(a, b):
    return jnp.dot(a, b, preferred_element_type=F32, precision=HI)


def _bf16_terms(x, terms):
    parts = []
    for _ in range(terms):
        p = x.astype(BF16)
        parts.append(p)
        x = x - p.astype(F32)
    return parts


def _dot_sel(x, sel, terms):
    sel = sel.astype(BF16)
    return sum(jnp.dot(p, sel, preferred_element_type=F32) for p in _bf16_terms(x, terms))


def _sel_dot(sel, x, terms):
    sel = sel.astype(BF16)
    return sum(jnp.dot(sel, p, preferred_element_type=F32) for p in _bf16_terms(x, terms))


def _sigmoid(x):
    return 1.0 / (1.0 + jnp.exp(-x))


def _div_pow2(x, n):
    return x >> (n.bit_length() - 1)


def _mod_pow2(x, n):
    return x & (n - 1)


def _rmsnorm_kernel(x_ref, g_ref, o_ref):
    x = x_ref[...]
    ms = jnp.mean(x * x, axis=-1, keepdims=True)
    o_ref[...] = (x * lax.rsqrt(ms + NORM_EPS) * g_ref[...]).astype(o_ref.dtype)


def _rmsnorm(x, g, out_dtype, tm=512):
    m, d = x.shape
    return pl.pallas_call(
        _rmsnorm_kernel,
        grid=(m // tm,),
        in_specs=[pl.BlockSpec((tm, d), lambda i: (i, 0)), pl.BlockSpec((1, d), lambda i: (0, 0))],
        out_specs=pl.BlockSpec((tm, d), lambda i: (i, 0)),
        out_shape=jax.ShapeDtypeStruct((m, d), out_dtype),
        compiler_params=_cparams("parallel"),
        name="rmsnorm",
    )(x, g.reshape(1, d))


def _mm_kernel(a_ref, w_ref, *refs, n_extra, epilogue, w_transposed):
    extra = refs[:n_extra]
    outs = refs[n_extra:-1]
    wbf = refs[-1]

    @pl.when(pl.program_id(1) == 0)
    def _():
        w = w_ref[...]
        wbf[...] = (w.T if w_transposed else w).astype(BF16)

    acc = jnp.dot(a_ref[...], wbf[...], preferred_element_type=F32)
    epilogue(acc, extra, outs)


def _matmul(a, w, n_cols, col_off_tiles, epilogue, extras, out_dtypes, tm, name, w_transposed=False):
    m, k = a.shape
    tn = MM_TN
    if w_transposed:
        w_spec = pl.BlockSpec((tn, k), lambda j, i: (j + col_off_tiles, 0))
    else:
        w_spec = pl.BlockSpec((k, tn), lambda j, i: (0, j + col_off_tiles))
    in_specs = [pl.BlockSpec((tm, k), lambda j, i: (i, 0)), w_spec]
    args = [a, w]
    for arr, kind, off in extras:
        if kind == "row":
            in_specs.append(pl.BlockSpec((tm, arr.shape[1]), lambda j, i: (i, 0)))
        elif kind == "tile":
            in_specs.append(pl.BlockSpec((tm, tn), functools.partial(lambda j, i, o: (i, j + o), o=off)))
        else:
            in_specs.append(pl.BlockSpec((1, tn), functools.partial(lambda j, i, o: (0, j + o), o=off)))
        args.append(arr)
    out_specs = [pl.BlockSpec((tm, tn), lambda j, i: (i, j)) for _ in out_dtypes]
    out_shape = [jax.ShapeDtypeStruct((m, n_cols), dt) for dt in out_dtypes]
    res = pl.pallas_call(
        functools.partial(_mm_kernel, n_extra=len(extras), epilogue=epilogue, w_transposed=w_transposed),
        grid=(n_cols // tn, m // tm),
        in_specs=in_specs,
        out_specs=out_specs,
        out_shape=out_shape,
        scratch_shapes=[pltpu.VMEM((k, tn), BF16)],
        compiler_params=_cparams("parallel", "arbitrary"),
        name=name,
    )(*args)
    return res


def _rope_tile(acc, cos, sin_signed):
    parts = []
    for h in range(MM_TN // HEAD_DIM):
        x = acc[:, h * HEAD_DIM:(h + 1) * HEAD_DIM]
        parts.append(x * cos + pltpu.roll(x, HEAD_DIM // 2, axis=1) * sin_signed)
    return jnp.concatenate(parts, axis=1)


def _ep_rope_q(acc, extra, outs):
    outs[0][...] = (_rope_tile(acc, extra[0][...], extra[1][...]) * ATT_SCALE).astype(outs[0].dtype)


def _ep_rope_k(acc, extra, outs):
    outs[0][...] = _rope_tile(acc, extra[0][...], extra[1][...]).astype(outs[0].dtype)


def _ep_plain(acc, extra, outs):
    outs[0][...] = acc.astype(outs[0].dtype)


def _ep_sigmoid(acc, extra, outs):
    outs[0][...] = _sigmoid(acc).astype(outs[0].dtype)


def _ep_residual(acc, extra, outs):
    outs[0][...] = (extra[0][...] + acc).astype(outs[0].dtype)


def _attn_prompt_kernel(q_ref, kp_ref, kc_ref, vp_ref, vc_ref, o_ref, l_ref, *, tq):
    first_neg = jnp.where(pl.program_id(2) == 0, NEG, 0.0)
    qi = lax.broadcasted_iota(jnp.int32, (BACK, BACK), 0)
    kj = lax.broadcasted_iota(jnp.int32, (BACK, BACK), 1)
    prev_band = kj >= qi
    cur_band = kj <= qi
    hs = [slice(h * HEAD_DIM, (h + 1) * HEAD_DIM) for h in range(KV_HEADS)]
    for jb in range(tq // BACK):
        rows = slice(jb * BACK, (jb + 1) * BACK)
        if jb == 0:
            k_prev = [kp_ref[0, :, s] for s in hs]
            v_prev = [vp_ref[0, :, s] for s in hs]
            prev_neg = first_neg
        else:
            prow = slice((jb - 1) * BACK, jb * BACK)
            k_prev = [kc_ref[0, prow, s] for s in hs]
            v_prev = [vc_ref[0, prow, s] for s in hs]
            prev_neg = 0.0
        qb = [q_ref[0, rows, s] for s in hs]
        s_p = [jnp.where(prev_band, _dot_nt(qb[h], k_prev[h]), NEG) + prev_neg for h in range(KV_HEADS)]
        s_c = [jnp.where(cur_band, _dot_nt(qb[h], kc_ref[0, rows, hs[h]]), NEG) for h in range(KV_HEADS)]
        m = [jnp.maximum(jnp.max(s_p[h], axis=-1, keepdims=True), jnp.max(s_c[h], axis=-1, keepdims=True))
             for h in range(KV_HEADS)]
        e_p = [jnp.exp(s_p[h] - m[h]) for h in range(KV_HEADS)]
        e_c = [jnp.exp(s_c[h] - m[h]) for h in range(KV_HEADS)]
        den = [jnp.sum(e_p[h], axis=-1, keepdims=True) + jnp.sum(e_c[h], axis=-1, keepdims=True)
               for h in range(KV_HEADS)]
        o = [(_dot(e_p[h], v_prev[h]) + _dot(e_c[h], vc_ref[0, rows, hs[h]])) / den[h] for h in range(KV_HEADS)]
        for h in range(KV_HEADS):
            o_ref[0, rows, hs[h]] = o[h]
            l_ref[0, rows, hs[h]] = jnp.broadcast_to(m[h] + jnp.log(den[h]), (BACK, HEAD_DIM))


def _attn_prompt_group(q, k, v, g, dil):
    b, s, _ = q.shape
    sub = s // dil
    tq = min(sub, 512)
    nq = sub // tq
    qv = q.reshape(b, sub, dil * Q_W)
    kv_ = k.reshape(b, sub, dil * KV_W)
    vv = v.reshape(b, sub, dil * KV_W)
    per = tq // BACK
    prev_map = lambda bi, r, i: (bi, jnp.maximum(i * per - 1, 0), r)
    cur_map = lambda bi, r, i: (bi, i, r)
    o, lse = pl.pallas_call(
        functools.partial(_attn_prompt_kernel, tq=tq),
        grid=(b, dil, nq),
        in_specs=[pl.BlockSpec((1, tq, KV_W), lambda bi, r, i: (bi, i, r * N_GROUPS + g)),
                  pl.BlockSpec((1, BACK, KV_W), prev_map),
                  pl.BlockSpec((1, tq, KV_W), cur_map),
                  pl.BlockSpec((1, BACK, KV_W), prev_map),
                  pl.BlockSpec((1, tq, KV_W), cur_map)],
        out_specs=[pl.BlockSpec((1, tq, KV_W), cur_map), pl.BlockSpec((1, tq, KV_W), cur_map)],
        out_shape=[jax.ShapeDtypeStruct((b, sub, dil * KV_W), F32)] * 2,
        compiler_params=_cparams("parallel", "parallel", "arbitrary"),
        name=f"attn_prompt_g{g}",
    )(qv, kv_, kv_, vv, vv)
    return o.reshape(b, s, KV_W), lse.reshape(b, s, KV_W)


def _attn_combine_kernel(o0, l0, o1, l1, o2, l2, out_ref):
    la, lb, lc = l0[...], l1[...], l2[...]
    m = jnp.maximum(jnp.maximum(la, lb), lc)
    wa, wb, wc = jnp.exp(la - m), jnp.exp(lb - m), jnp.exp(lc - m)
    out = (wa * o0[...] + wb * o1[...] + wc * o2[...]) / (wa + wb + wc)
    out_ref[...] = out.astype(out_ref.dtype)


def _attn_combine(parts, tm=512):
    m, w = parts[0].shape
    spec = pl.BlockSpec((tm, w), lambda i: (i, 0))
    return pl.pallas_call(
        _attn_combine_kernel,
        grid=(m // tm,),
        in_specs=[spec] * 6,
        out_specs=spec,
        out_shape=jax.ShapeDtypeStruct((m, w), BF16),
        compiler_params=_cparams("parallel"),
        name="attn_combine",
    )(*parts)


def _attn_sample_kernel(q_ref, kn_ref, vn_ref, k0_ref, v0_ref, k1_ref, v1_ref, k2_ref, v2_ref, o_ref, *, n_new):
    tok = lax.broadcasted_iota(jnp.int32, (BACK, KV_HEADS, 1), 0)
    kc_refs = (k0_ref, k1_ref, k2_ref)
    vc_refs = (v0_ref, v1_ref, v2_ref)
    for s in range(n_new):
        outs, lses = [], []
        for g in range(N_GROUPS):
            qh = q_ref[0, s, g * KV_HEADS:(g + 1) * KV_HEADS, :][None]
            if g == 0:
                kc, vc = kc_refs[g][0], vc_refs[g][0]
                new = slice(0, s + 1)
            else:
                kc, vc = kc_refs[g][0, :, s], vc_refs[g][0, :, s]
                new = slice(s, s + 1)
            sc = jnp.sum(kc * qh, axis=-1, keepdims=True)
            if g == 0:
                sc = jnp.where(tok >= s, sc, NEG)
            sn = jnp.sum(kn_ref[0, new] * qh, axis=-1, keepdims=True)
            m = jnp.maximum(jnp.max(sc, axis=0, keepdims=True), jnp.max(sn, axis=0, keepdims=True))
            p = jnp.exp(sc - m)
            pn = jnp.exp(sn - m)
            den = jnp.sum(p, axis=0, keepdims=True) + jnp.sum(pn, axis=0, keepdims=True)
            o = (jnp.sum(p * vc, axis=0, keepdims=True)
                 + jnp.sum(pn * vn_ref[0, new], axis=0, keepdims=True)) / den
            outs.append(o)
            lses.append(m + jnp.log(den))
        mm = jnp.maximum(jnp.maximum(lses[0], lses[1]), lses[2])
        ws = [jnp.exp(l - mm) for l in lses]
        comb = (ws[0] * outs[0] + ws[1] * outs[1] + ws[2] * outs[2]) / (ws[0] + ws[1] + ws[2])
        o_ref[0, s] = comb[0].astype(o_ref.dtype)


def _attn_sample(q, k_new, v_new, cache_k, cache_v):
    db, n_new = q.shape[0], q.shape[1]
    w_buf = cache_k.shape[1]
    assert w_buf == BACK * DILATIONS[-1] and n_new <= DILATIONS[1]
    new_spec = pl.BlockSpec((1, n_new, KV_HEADS, HEAD_DIM), lambda b: (b, 0, 0, 0))
    specs = [pl.BlockSpec((1, n_new, N_GROUPS * KV_HEADS, HEAD_DIM), lambda b: (b, 0, 0, 0)), new_spec, new_spec]
    args = [q, k_new, v_new]
    for dil in DILATIONS:
        sub = w_buf // dil
        last = sub // BACK - 1
        if dil == 1:
            shape = (db, sub, KV_HEADS, HEAD_DIM)
            spec = pl.BlockSpec((1, BACK, KV_HEADS, HEAD_DIM), functools.partial(lambda b, l: (b, l, 0, 0), l=last))
        else:
            shape = (db, sub, dil, KV_HEADS, HEAD_DIM)
            spec = pl.BlockSpec((1, BACK, n_new, KV_HEADS, HEAD_DIM),
                                functools.partial(lambda b, l: (b, l, 0, 0, 0), l=last))
        for c in (cache_k, cache_v):
            specs.append(spec)
            args.append(c.reshape(shape))
    return pl.pallas_call(
        functools.partial(_attn_sample_kernel, n_new=n_new),
        grid=(db,),
        in_specs=specs,
        out_specs=new_spec,
        out_shape=jax.ShapeDtypeStruct((db, n_new, KV_HEADS, HEAD_DIM), F32),
        compiler_params=_cparams("parallel"),
        name="attn_sample",
    )(*args)


def _head_indicator(width):
    l = lax.broadcasted_iota(jnp.int32, (width, LANE), 0)
    h = lax.broadcasted_iota(jnp.int32, (width, LANE), 1)
    return (_div_pow2(l, RW_N) == h).astype(F32)


def _head_indicator_t(width):
    h = lax.broadcasted_iota(jnp.int32, (LANE, width), 0)
    l = lax.broadcasted_iota(jnp.int32, (LANE, width), 1)
    return (_div_pow2(l, RW_N) == h).astype(F32)


def _rw_prep_kernel(ur, uk, uv, ul, fpr, fpk, fpv, fpl, fsr, fsk, fsv, fsl, mur, muk, muv, mul, ww, wa, wg, w0, a0,
                    kk_, ka, r_o, lw_o, k_o, v_o, kk_o, b_o, g_o, *, n_prompt_tiles, ds):
    is_sample = pl.program_id(0) >= n_prompt_tiles

    def mixed(u_ref, fp_ref, fs_ref, mu_ref):
        u = u_ref[...]
        row = lax.broadcasted_iota(jnp.int32, u.shape, 0)
        rolled = pltpu.roll(u, 1, axis=0)
        p_prompt = jnp.where(row == 0, fp_ref[0], rolled)
        p_sample = jnp.where(_mod_pow2(row, ds) == 0, fs_ref[...], rolled)
        prev = jnp.where(is_sample, p_sample, p_prompt)
        return u + (prev - u) * mu_ref[...]

    xr = mixed(ur, fpr, fsr, mur)
    xk = mixed(uk, fpk, fsk, muk)
    xv = mixed(uv, fpv, fsv, muv)
    xl = mixed(ul, fpl, fsl, mul)
    col = lax.broadcasted_iota(jnp.int32, xl.shape, 1)
    act = jnp.where(col < LORA_W, jnp.tanh(xl),
                    jnp.where(col < LORA_W + LORA_A, xl,
                              jnp.where(col < LORA_ALL, _sigmoid(xl), 0.0)))
    zw = _dot(act, ww[...])
    za = _dot(act, wa[...])
    zg = _dot(act, wg[...])
    lw = -RW_DECAY_SCALE * _sigmoid(w0[...] + zw)
    a = _sigmoid(a0[...] + za)
    kk = xk * kk_[...]
    width = kk.shape[1]
    ss = _dot_sel(kk * kk, _head_indicator(width), 2)
    inv = 1.0 / jnp.maximum(jnp.sqrt(ss), 1e-12)
    kkn = kk * _dot_sel(inv, _head_indicator_t(width), 2)
    r_o[...] = xr.astype(r_o.dtype)
    lw_o[...] = lw
    k_o[...] = (xk * (1.0 + (a - 1.0) * ka[...])).astype(k_o.dtype)
    v_o[...] = xv.astype(v_o.dtype)
    kk_o[...] = kkn.astype(kk_o.dtype)
    b_o[...] = (kkn * a).astype(b_o.dtype)
    g_o[...] = zg.astype(g_o.dtype)


def _rw_prep(u_rkv, u_lora, shift, mp, sp, ds, mu, w_w, w_a, w_g, w0, a0, k_k, k_a, tm=512, tw=512):
    m = u_rkv.shape[0]
    nj = RW_W // tw
    n_p = mp // tm
    assert sp % tm == 0 and mp % tm == 0 and (m - mp) % tm == 0 and tm % ds == 0 and ds & (ds - 1) == 0
    mu_rkv = mu[:3 * RW_W].reshape(1, 3 * RW_W)
    lpad = LORA_PAD - LORA_ALL
    mu_l = jnp.pad(mu[3 * RW_W:], (0, lpad)).reshape(1, LORA_PAD)

    starts = jnp.arange(n_p, dtype=jnp.int32) * tm
    inside = ((starts % sp) != 0)[:, None]
    prev_idx = jnp.maximum(starts - 1, 0)

    def prompt_fix(u):
        return jnp.where(inside, jnp.take(u, prev_idx, axis=0), 0.0)[:, None, :]

    def sample_fix(first_rows):
        db, w = first_rows.shape
        return jnp.zeros((db, ds, w), F32).at[:, 0].set(first_rows).reshape(db * ds, w)

    fp_rkv, fp_l = prompt_fix(u_rkv), prompt_fix(u_lora)
    fs_rkv = sample_fix(shift[:, :3 * RW_W])
    fs_l = sample_fix(jnp.pad(shift[:, 3 * RW_W:], ((0, 0), (0, lpad))))

    def feat(off):
        return pl.BlockSpec((tm, tw), functools.partial(lambda i, j, o: (i, j + o), o=off))

    def fixp(off):
        return pl.BlockSpec((1, 1, tw), functools.partial(lambda i, j, o: (jnp.minimum(i, n_p - 1), 0, j + o), o=off))

    def fixs(off):
        return pl.BlockSpec((tm, tw), functools.partial(lambda i, j, o: (jnp.maximum(i - n_p, 0), j + o), o=off))

    def vec(off):
        return pl.BlockSpec((1, tw), functools.partial(lambda i, j, o: (0, j + o), o=off))

    lspec = pl.BlockSpec((tm, LORA_PAD), lambda i, j: (i, 0))
    lfixp = pl.BlockSpec((1, 1, LORA_PAD), lambda i, j: (jnp.minimum(i, n_p - 1), 0, 0))
    lfixs = pl.BlockSpec((tm, LORA_PAD), lambda i, j: (jnp.maximum(i - n_p, 0), 0))
    wspec = pl.BlockSpec((LORA_PAD, tw), lambda i, j: (0, j))
    in_specs = [feat(0), feat(nj), feat(2 * nj), lspec,
                fixp(0), fixp(nj), fixp(2 * nj), lfixp,
                fixs(0), fixs(nj), fixs(2 * nj), lfixs,
                vec(0), vec(nj), vec(2 * nj), pl.BlockSpec((1, LORA_PAD), lambda i, j: (0, 0)),
                wspec, wspec, wspec, vec(0), vec(0), vec(0), vec(0)]
    out_spec = pl.BlockSpec((tm, tw), lambda i, j: (i, j))
    outs = pl.pallas_call(
        functools.partial(_rw_prep_kernel, n_prompt_tiles=n_p, ds=ds),
        grid=(m // tm, nj),
        in_specs=in_specs,
        out_specs=[out_spec] * 7,
        out_shape=[jax.ShapeDtypeStruct((m, RW_W), F32 if i == 1 else BF16) for i in range(7)],
        compiler_params=_cparams("parallel", "arbitrary"),
        name="rw_prep",
    )(u_rkv, u_rkv, u_rkv, u_lora, fp_rkv, fp_rkv, fp_rkv, fp_l, fs_rkv, fs_rkv, fs_rkv, fs_l,
      mu_rkv, mu_rkv, mu_rkv, mu_l,
      w_w, w_a, w_g, w0.reshape(1, RW_W), a0.reshape(1, RW_W), k_k.reshape(1, RW_W), k_a.reshape(1, RW_W))
    return outs


def _rw_head_out(y, r, k, v, g, ln_w, ln_b, r_k):
    width = y.shape[1]
    ind, ind_t = _head_indicator(width).astype(BF16), _head_indicator_t(width).astype(BF16)
    mean = _dot_sel(_dot_sel(y, ind, 2) * (1.0 / RW_N), ind_t, 2)
    d = y - mean
    var_h = _dot_sel(d * d, ind, 2) * (1.0 / RW_N)
    rstd = _dot_sel(lax.rsqrt(var_h + RW_GN_EPS), ind_t, 2)
    yn = d * rstd * ln_w + ln_b
    bonus = _dot_sel(_dot_sel(r * k * r_k, ind, 2), ind_t, 2) * v
    return (yn + bonus) * g


def _rw_scan_kernel(r_ref, lw_ref, k_ref, v_ref, kk_ref, b_ref, g_ref, lnw_ref, lnb_ref, rk_ref, s_in,
                    o_ref, s_out, s_bd, *, chunk):
    t_id = pl.program_id(1)
    n_t = pl.num_programs(1)
    gw = RW_GROUP * RW_N
    n_groups = RW_W // gw
    rows = RW_GROUP * chunk

    ri = lax.broadcasted_iota(jnp.int32, (gw, gw), 0)
    ci = lax.broadcasted_iota(jnp.int32, (gw, gw), 1)
    state_mask = _div_pow2(ri, RW_N) == _div_pow2(ci, RW_N)

    @pl.when(t_id == 0)
    def _():
        kr = lax.broadcasted_iota(jnp.int32, (RW_N, gw), 0)
        kc = lax.broadcasted_iota(jnp.int32, (RW_N, gw), 1)
        spread = (_mod_pow2(kc, RW_N) == kr).astype(F32)
        for g in range(n_groups):
            tiled = _dot_sel(s_in[0, g * gw:(g + 1) * gw, :], spread, 3)
            s_bd[g] = jnp.where(state_mask, tiled, 0.0)

    ti = lax.broadcasted_iota(jnp.int32, (chunk, chunk), 0)
    tj = lax.broadcasted_iota(jnp.int32, (chunk, chunk), 1)
    lw = lw_ref[0]
    cum = _sel_dot((tj <= ti).astype(F32), lw, 3)
    e_pos = jnp.exp(cum)
    e_neg = jnp.exp(-cum)
    e_prev = jnp.exp(cum - lw)
    cum_last = cum[chunk - 1:chunk, :]
    e_rem = jnp.exp(cum_last - cum)
    c_all = jnp.exp(cum_last)
    r_in, k_in, b_in = r_ref[0].astype(F32), k_ref[0].astype(F32), b_ref[0].astype(F32)
    kt = kk_ref[0].astype(F32) * e_prev
    bt = b_in * e_neg
    k2t = k_in * e_neg
    rt = r_in * e_pos
    btc = b_in * e_rem
    k2tc = k_in * e_rem
    vv = v_ref[0].astype(F32)

    sr = lax.broadcasted_iota(jnp.int32, (rows, gw), 0)
    sc = lax.broadcasted_iota(jnp.int32, (rows, gw), 1)
    stack_mask = _div_pow2(sr, chunk) == _div_pow2(sc, RW_N)
    ar = lax.broadcasted_iota(jnp.int32, (rows, rows), 0)
    ac = lax.broadcasted_iota(jnp.int32, (rows, rows), 1)
    strict = ar > ac
    incl = ar >= ac
    eye = (ar == ac).astype(F32)

    def stack(x):
        return jnp.where(stack_mask, jnp.concatenate([x] * RW_GROUP, axis=0), 0.0).astype(BF16)

    gs = range(n_groups)
    lanes = [slice(g * gw, (g + 1) * gw) for g in gs]
    kr_s = [jnp.concatenate([stack(kt[:, ls]), stack(rt[:, ls])], axis=0) for ls in lanes]
    bt_s = [stack(bt[:, ls]) for ls in lanes]
    k2t_s = [stack(k2t[:, ls]) for ls in lanes]
    v_s = [stack(vv[:, ls]) for ls in lanes]
    s0 = [s_bd[g] for g in gs]
    p_b = [_dot_nt(kr_s[g], bt_s[g]) for g in gs]
    p_k = [_dot_nt(kr_s[g], k2t_s[g]) for g in gs]
    p_s = [_dot_nt(kr_s[g], s0[g]) for g in gs]
    a_b = [jnp.where(strict, p_b[g][:rows], 0.0) for g in gs]
    r_b = [jnp.where(incl, p_b[g][rows:], 0.0) for g in gs]
    ar_k = [jnp.concatenate([jnp.where(strict, p_k[g][:rows], 0.0), jnp.where(incl, p_k[g][rows:], 0.0)], axis=0)
            for g in gs]
    p_v = [_dot(ar_k[g], v_s[g]) for g in gs]
    rhs = [p_s[g][:rows] + p_v[g][:rows] for g in gs]
    inv = [eye - a_b[g] for g in gs]
    pw = a_b
    n = 2
    while n < chunk:
        pw = [_dot(pw[g], pw[g]) for g in gs]
        inv = [inv[g] + _dot(inv[g], pw[g]) for g in gs]
        n *= 2
    u = [-_dot(inv[g], rhs[g]) for g in gs]
    y_bd = [p_s[g][rows:] + p_v[g][rows:] + _dot(r_b[g], u[g]) for g in gs]
    ys = []
    for g in gs:
        y = y_bd[g][0:chunk]
        for h in range(1, RW_GROUP):
            y = y + y_bd[g][h * chunk:(h + 1) * chunk]
        ys.append(y)
    o_ref[0] = _rw_head_out(jnp.concatenate(ys, axis=1), r_in, k_in, vv, g_ref[0].astype(F32),
                            lnw_ref[...], lnb_ref[...], rk_ref[...]).astype(o_ref.dtype)
    for g in gs:
        uv = jnp.concatenate([u[g].astype(BF16), v_s[g]], axis=0)
        bk = jnp.concatenate([stack(btc[:, lanes[g]]), stack(k2tc[:, lanes[g]])], axis=0)
        s_bd[g] = s0[g] * c_all[:, lanes[g]] + _dot_tn(uv, bk)

    @pl.when(t_id == n_t - 1)
    def _():
        gr = lax.broadcasted_iota(jnp.int32, (gw, RW_N), 0)
        gc = lax.broadcasted_iota(jnp.int32, (gw, RW_N), 1)
        gather = (_mod_pow2(gr, RW_N) == gc).astype(F32)
        for g in range(n_groups):
            s_out[0, g * gw:(g + 1) * gw, :] = _dot_sel(s_bd[g], gather, 3)


def _rw_scan(r, lw, k, v, kk, b, g, ln_w, ln_b, r_k, state, chunk, bn, t):
    gw = RW_GROUP * RW_N
    nt = t // chunk
    seq = pl.BlockSpec((1, chunk, RW_W), lambda bi, ti: (0, bi * nt + ti, 0))
    vec = pl.BlockSpec((1, RW_W), lambda bi, ti: (0, 0))
    st = pl.BlockSpec((1, RW_W, RW_N), lambda bi, ti: (bi, 0, 0))
    o, s_new = pl.pallas_call(
        functools.partial(_rw_scan_kernel, chunk=chunk),
        grid=(bn, nt),
        in_specs=[seq] * 7 + [vec] * 3 + [st],
        out_specs=[seq, st],
        out_shape=[jax.ShapeDtypeStruct((1, bn * t, RW_W), BF16), jax.ShapeDtypeStruct((bn, RW_W, RW_N), F32)],
        scratch_shapes=[pltpu.VMEM((RW_W // gw, gw, gw), F32)],
        compiler_params=_cparams("parallel", "arbitrary"),
        name=f"rw_scan_c{chunk}",
    )(*(a[None] for a in (r, lw, k, v, kk, b, g)), ln_w.reshape(1, RW_W), ln_b.reshape(1, RW_W),
      r_k.reshape(1, RW_W), state.reshape(bn, RW_W, RW_N))
    return o[0], s_new.reshape(bn, RW_HEADS, RW_N, RW_N)


def _rw_step_kernel(r_ref, lw_ref, k_ref, v_ref, kk_ref, b_ref, s_in, y_ref, s_out, *, n_steps):
    sub = 8
    for t in range(n_steps):
        w = jnp.exp(lw_ref[t])
        kap, bb, k2, r = kk_ref[t], b_ref[t], k_ref[t], r_ref[t]
        src = s_in if t == 0 else s_out

        def body(blk, c, t=t, w=w, kap=kap, bb=bb, k2=k2, r=r, src=src):
            base = pl.multiple_of(blk * sub, sub)
            v_rows = v_ref[t, pl.ds(base, sub), :]
            y_rows = []
            for j in range(sub):
                sv = src[0, base + j]
                sa = -jnp.sum(sv * kap, axis=0, keepdims=True)
                sn = sv * w + sa * bb + v_rows[j:j + 1] * k2
                s_out[0, base + j] = sn
                y_rows.append(jnp.sum(sn * r, axis=0, keepdims=True))
            y_ref[t, pl.ds(base, sub), :] = jnp.concatenate(y_rows, axis=0)
            return c

        lax.fori_loop(0, RW_N // sub, body, 0)


def _rw_steps(r, lw, k, v, kk, b, state):
    n_steps, _, nb = r.shape
    assert nb % LANE == 0
    seq = pl.BlockSpec((n_steps, RW_N, nb), lambda h: (0, h, 0))
    st = pl.BlockSpec((1, RW_N, RW_N, nb), lambda h: (h, 0, 0, 0))
    return pl.pallas_call(
        functools.partial(_rw_step_kernel, n_steps=n_steps),
        grid=(RW_HEADS,),
        in_specs=[seq] * 6 + [st],
        out_specs=[seq, st],
        out_shape=[jax.ShapeDtypeStruct((n_steps, RW_W, nb), F32),
                   jax.ShapeDtypeStruct((RW_HEADS, RW_N, RW_N, nb), F32)],
        compiler_params=_cparams("parallel"),
        name="rw_steps",
    )(r, lw, k, v, kk, b, state)


def _rw_post_kernel(y_ref, r_ref, k_ref, v_ref, g_ref, lnw, lnb, rk, o_ref):
    f32 = lambda ref: ref[...].astype(F32)
    o_ref[...] = _rw_head_out(y_ref[...], f32(r_ref), f32(k_ref), f32(v_ref), f32(g_ref),
                              lnw[...], lnb[...], rk[...]).astype(o_ref.dtype)


def _rw_post(y, r, k, v, g, ln_w, ln_b, r_k, row_off, tm=512, tw=512):
    m = y.shape[0]
    off = row_off // tm
    own = pl.BlockSpec((tm, tw), lambda i, j: (i, j))
    feat = pl.BlockSpec((tm, tw), lambda i, j: (i + off, j))
    vec = pl.BlockSpec((1, tw), lambda i, j: (0, j))
    return pl.pallas_call(
        _rw_post_kernel,
        grid=(m // tm, RW_W // tw),
        in_specs=[own] + [feat] * 4 + [vec] * 3,
        out_specs=own,
        out_shape=jax.ShapeDtypeStruct((m, RW_W), BF16),
        compiler_params=_cparams("parallel", "parallel"),
        name="rw_post",
    )(y, r, k, v, g, ln_w.reshape(1, RW_W), ln_b.reshape(1, RW_W), r_k.reshape(1, RW_W))


def _merge_kernel(oa_ref, ob_ref, wa_ref, wb_ref, ga_ref, gb_ref, out_ref, wa_bf, wb_bf):
    @pl.when(pl.program_id(1) == 0)
    def _():
        wa_bf[...] = wa_ref[...].astype(BF16)
        wb_bf[...] = wb_ref[...].astype(BF16)

    ya = jnp.dot(oa_ref[...], wa_bf[...], preferred_element_type=F32)
    yb = jnp.dot(ob_ref[...], wb_bf[...], preferred_element_type=F32)
    out_ref[...] = (ga_ref[...].astype(F32) * ya + gb_ref[...].astype(F32) * yb).astype(out_ref.dtype)


def _merge(o_a, o_b, w_a, w_b, gates, tm):
    m = o_a.shape[0]
    tn = MM_TN
    nj = D_MODEL // tn
    return pl.pallas_call(
        _merge_kernel,
        grid=(nj, m // tm),
        in_specs=[pl.BlockSpec((tm, KV_W), lambda j, i: (i, 0)),
                  pl.BlockSpec((tm, RW_W), lambda j, i: (i, 0)),
                  pl.BlockSpec((KV_W, tn), lambda j, i: (0, j)),
                  pl.BlockSpec((RW_W, tn), lambda j, i: (0, j)),
                  pl.BlockSpec((tm, tn), lambda j, i: (i, j)),
                  pl.BlockSpec((tm, tn), lambda j, i: (i, j + nj))],
        out_specs=pl.BlockSpec((tm, tn), lambda j, i: (i, j)),
        out_shape=jax.ShapeDtypeStruct((m, D_MODEL), BF16),
        scratch_shapes=[pltpu.VMEM((KV_W, tn), BF16), pltpu.VMEM((RW_W, tn), BF16)],
        compiler_params=_cparams("parallel", "arbitrary"),
        name="merge",
    )(o_a, o_b, w_a, w_b, gates, gates)


def _router_kernel(x_ref, g_ref, wr_ref, br_ref, h_ref, id_ref, wt_ref):
    x = x_ref[...]
    ms = jnp.mean(x * x, axis=-1, keepdims=True)
    h = x * lax.rsqrt(ms + NORM_EPS) * g_ref[...]
    h_ref[...] = h.astype(h_ref.dtype)
    logits = _dot_hi(h, wr_ref[...]) + br_ref[...]
    lane = lax.broadcasted_iota(jnp.int32, logits.shape, 1)
    lane_f = lane.astype(F32)
    gmask = lane < MOE_GROUPS
    gl = jnp.where(gmask, logits, NEG)
    gm = jnp.max(gl, axis=-1, keepdims=True)
    gi = jnp.min(jnp.where(gl == gm, lane_f, float(LANE)), axis=-1, keepdims=True)
    g_prob = 1.0 / jnp.sum(jnp.where(gmask, jnp.exp(gl - gm), 0.0), axis=-1, keepdims=True)
    lo = MOE_GROUPS + gi * MOE_PER_GROUP
    emask = jnp.logical_and(lane_f >= lo, lane_f < lo + MOE_PER_GROUP)
    el = jnp.where(emask, logits, NEG)
    m1 = jnp.max(el, axis=-1, keepdims=True)
    i1 = jnp.min(jnp.where(el == m1, lane_f, float(LANE)), axis=-1, keepdims=True)
    el2 = jnp.where(lane_f == i1, NEG, el)
    m2 = jnp.max(el2, axis=-1, keepdims=True)
    i2 = jnp.min(jnp.where(el2 == m2, lane_f, float(LANE)), axis=-1, keepdims=True)
    t = jnp.exp(m2 - m1)
    w1 = g_prob / (1.0 + t)
    w2 = g_prob * t / (1.0 + t)
    ids = jnp.where(lane == 0, i1 - MOE_GROUPS, jnp.where(lane == 1, i2 - MOE_GROUPS, 0.0))
    id_ref[...] = ids.astype(jnp.int32)
    wt_ref[...] = jnp.where(lane == 0, w1, jnp.where(lane == 1, w2, 0.0))


def _router(x, g, w_router, b_router, tm=256):
    m, d = x.shape
    row = pl.BlockSpec((tm, d), lambda i: (i, 0))
    small = pl.BlockSpec((tm, LANE), lambda i: (i, 0))
    return pl.pallas_call(
        _router_kernel,
        grid=(m // tm,),
        in_specs=[row, pl.BlockSpec((1, d), lambda i: (0, 0)),
                  pl.BlockSpec((d, LANE), lambda i: (0, 0)), pl.BlockSpec((1, LANE), lambda i: (0, 0))],
        out_specs=[row, small, small],
        out_shape=[jax.ShapeDtypeStruct((m, d), F32), jax.ShapeDtypeStruct((m, LANE), jnp.int32),
                   jax.ShapeDtypeStruct((m, LANE), F32)],
        compiler_params=_cparams("parallel"),
        name="ffn_norm_router",
    )(x, g.reshape(1, d), w_router, b_router)


def _moe_kernel(te_ref, tv_ref, last_ref, x_ref, wg_ref, wu_ref, wd_ref, rw_ref, o_ref):
    i = pl.program_id(0)
    j = pl.program_id(1)
    valid = tv_ref[i] > 0

    @pl.when(j == 0)
    def _():
        o_ref[...] = jnp.zeros_like(o_ref)

    @pl.when(valid)
    def _():
        x = x_ref[...]
        gate = jnp.dot(x, wg_ref[0].astype(BF16), preferred_element_type=F32)
        up = jnp.dot(x, wu_ref[0].astype(BF16), preferred_element_type=F32)
        hidden = (gate * _sigmoid(gate) * up).astype(BF16)
        o_ref[...] += jnp.dot(hidden, wd_ref[0].astype(BF16), preferred_element_type=F32)

    @pl.when(jnp.logical_and(valid, j == pl.num_programs(1) - 1))
    def _():
        o_ref[...] = o_ref[...] * rw_ref[...]


def _moe_experts(xs, row_w, tile_expert, tile_valid, last_tile, w_gate, w_up, w_down):
    p, d = xs.shape
    n_tiles = p // MOE_TM
    nf = EXPERT_FF // MOE_TF

    def f_idx(i, j, tv):
        return jnp.where(tv[i] > 0, j, nf - 1)

    def rows(i, j, te, tv, last):
        return (jnp.minimum(i, last[0]), 0)

    grid_spec = pltpu.PrefetchScalarGridSpec(
        num_scalar_prefetch=3,
        grid=(n_tiles, nf),
        in_specs=[pl.BlockSpec((MOE_TM, d), rows),
                  pl.BlockSpec((1, d, MOE_TF), lambda i, j, te, tv, last: (te[i], 0, f_idx(i, j, tv))),
                  pl.BlockSpec((1, d, MOE_TF), lambda i, j, te, tv, last: (te[i], 0, f_idx(i, j, tv))),
                  pl.BlockSpec((1, MOE_TF, d), lambda i, j, te, tv, last: (te[i], f_idx(i, j, tv), 0)),
                  pl.BlockSpec((MOE_TM, 1), rows)],
        out_specs=pl.BlockSpec((MOE_TM, d), lambda i, j, te, tv, last: (i, 0)),
    )
    return pl.pallas_call(
        _moe_kernel,
        grid_spec=grid_spec,
        out_shape=jax.ShapeDtypeStruct((p, d), F32),
        compiler_params=_cparams("arbitrary", "arbitrary"),
        name="moe_experts",
    )(tile_expert, tile_valid, last_tile, xs, w_gate, w_up, w_down, row_w)


def _row_copy(src_hbm, row, dst_vmem, slot, sem):
    return pltpu.make_async_copy(src_hbm.at[pl.ds(row, 1)], dst_vmem.at[pl.ds(slot, 1)], sem)


def _dispatch_kernel(tok_ref, nv_ref, h_hbm, o_ref, buf, sem):
    i = pl.program_id(0)
    n = nv_ref[i]

    @pl.when(i == 0)
    def _():
        buf[...] = jnp.zeros_like(buf)

    def issue(r, c):
        _row_copy(h_hbm, tok_ref[i * MOE_TM + r], buf, r, sem).start()
        return c

    def wait(r, c):
        _row_copy(h_hbm, 0, buf, r, sem).wait()
        return c

    lax.fori_loop(0, n, issue, 0)
    lax.fori_loop(0, n, wait, 0)
    o_ref[...] = buf[...].astype(o_ref.dtype)


def _dispatch_rows(h, slot_tok, tile_rows):
    d = h.shape[1]
    p = slot_tok.shape[0]
    grid_spec = pltpu.PrefetchScalarGridSpec(
        num_scalar_prefetch=2,
        grid=(p // MOE_TM,),
        in_specs=[pl.BlockSpec(memory_space=pl.ANY)],
        out_specs=pl.BlockSpec((MOE_TM, d), lambda i, tok, nv: (i, 0)),
        scratch_shapes=[pltpu.VMEM((MOE_TM, d), F32), pltpu.SemaphoreType.DMA(())],
    )
    return pl.pallas_call(
        _dispatch_kernel,
        grid_spec=grid_spec,
        out_shape=jax.ShapeDtypeStruct((p, d), BF16),
        compiler_params=_cparams("arbitrary"),
        name="moe_dispatch",
    )(slot_tok, tile_rows, h)


def _dispatch_plan(ids, wts):
    t = ids.shape[0]
    a = t * 2
    flat_e = ids.reshape(a)
    onehot = (flat_e[:, None] == jnp.arange(N_EXPERTS, dtype=jnp.int32)[None, :]).astype(jnp.int32)
    csum = jnp.cumsum(onehot, axis=0)
    counts = csum[-1]
    rank = jnp.take_along_axis(csum, flat_e[:, None], axis=1)[:, 0] - 1
    tiles_per = (counts + MOE_TM - 1) // MOE_TM
    tile_end = jnp.cumsum(tiles_per)
    tile_start = tile_end - tiles_per
    dest = tile_start[flat_e] * MOE_TM + rank
    n_tiles = -(-a // MOE_TM) + N_EXPERTS
    total = tile_end[-1]
    slot_tok = jnp.zeros((n_tiles * MOE_TM,), jnp.int32).at[dest].set(jnp.arange(a, dtype=jnp.int32) // 2)
    slot_w = jnp.zeros((n_tiles * MOE_TM,), F32).at[dest].set(wts.reshape(a))
    tile_ids = jnp.arange(n_tiles, dtype=jnp.int32)
    tile_valid = (tile_ids < total).astype(jnp.int32)
    tile_expert = jnp.searchsorted(tile_end, jnp.minimum(tile_ids, total - 1), side="right").astype(jnp.int32)
    tile_expert = jnp.minimum(tile_expert, N_EXPERTS - 1)
    tile_pos = tile_ids - tile_start[tile_expert]
    tile_rows = jnp.where(tile_valid > 0, jnp.clip(counts[tile_expert] - tile_pos * MOE_TM, 0, MOE_TM), 0)
    last_tile = (total - 1).astype(jnp.int32).reshape(1)
    return slot_tok, slot_w, dest, tile_expert, tile_valid, tile_rows.astype(jnp.int32), last_tile


def _final_kernel(dest_ref, x_ref, y_hbm, g_ref, o_ref, buf, sem, *, row_off, tm):
    base = (row_off + pl.program_id(0) * tm) * 2

    def issue(r, c):
        _row_copy(y_hbm, dest_ref[base + r], buf, (r & 1) * tm + (r >> 1), sem).start()
        return c

    def wait(r, c):
        _row_copy(y_hbm, 0, buf, r, sem).wait()
        return c

    lax.fori_loop(0, 2 * tm, issue, 0)
    lax.fori_loop(0, 2 * tm, wait, 0)
    x = x_ref[...] + (buf[0:tm, :] + buf[tm:2 * tm, :])
    ms = jnp.mean(x * x, axis=-1, keepdims=True)
    o_ref[...] = x * lax.rsqrt(ms + NORM_EPS) * g_ref[...]


def _final(x, y_rows, dest, g, row_off, rows, tm=128):
    d = x.shape[1]
    off = row_off // tm
    grid_spec = pltpu.PrefetchScalarGridSpec(
        num_scalar_prefetch=1,
        grid=(rows // tm,),
        in_specs=[pl.BlockSpec((tm, d), lambda i, dst: (i + off, 0)),
                  pl.BlockSpec(memory_space=pl.ANY),
                  pl.BlockSpec((1, d), lambda i, dst: (0, 0))],
        out_specs=pl.BlockSpec((tm, d), lambda i, dst: (i, 0)),
        scratch_shapes=[pltpu.VMEM((2 * tm, d), F32), pltpu.SemaphoreType.DMA(())],
    )
    return pl.pallas_call(
        functools.partial(_final_kernel, row_off=row_off, tm=tm),
        grid_spec=grid_spec,
        out_shape=jax.ShapeDtypeStruct((rows, d), F32),
        compiler_params=_cparams("arbitrary"),
        name="final_norm",
    )(dest, x, y_rows, g.reshape(1, d))


def _rope_tables(pos):
    half = HEAD_DIM // 2
    inv_freq = ROPE_THETA ** (-jnp.arange(half, dtype=F32) / half)
    ang = pos.astype(F32)[:, None] * inv_freq[None, :]
    cos, sin = jnp.cos(ang), jnp.sin(ang)
    return jnp.concatenate([cos, cos], axis=-1), jnp.concatenate([-sin, sin], axis=-1)


def kernel(x_prompt, x_sample, cache_k, cache_v, state_shift, state_wkv, norm_mix_g, w_in, rw_mu, rw_w0, rw_w2,
           rw_a0, rw_a2, rw_g2, rw_k_k, rw_k_a, rw_r_k, rw_ln_w, rw_ln_b, w_branch_a, w_branch_b, w_out,
           norm_ffn_g, router_group_w, router_group_b, router_expert_w, router_expert_b, exp_gate, exp_up,
           exp_down, norm_final_g):
    assert w_in.shape[0] == 1, "single-layer trunk"
    bp, sp, d = x_prompt.shape
    db, ds, _ = x_sample.shape
    mp, ms_ = bp * sp, db * ds
    m = mp + ms_
    past = cache_k.shape[2]
    tm_mm = m // 8

    x_all = jnp.concatenate([x_prompt.reshape(mp, d), x_sample.reshape(ms_, d)], axis=0)
    h = _rmsnorm(x_all, norm_mix_g[0], BF16)

    w_in_t = jnp.swapaxes(w_in, 1, 2).reshape(w_in.shape[2], d)
    cos_p, sin_p = _rope_tables(jnp.arange(sp, dtype=jnp.int32))
    cos_s, sin_s = _rope_tables(past + jnp.arange(ds, dtype=jnp.int32))
    cos = jnp.concatenate([jnp.tile(cos_p, (bp, 1)), jnp.tile(cos_s, (db, 1))], axis=0)
    sin = jnp.concatenate([jnp.tile(sin_p, (bp, 1)), jnp.tile(sin_s, (db, 1))], axis=0)
    rope_extras = [(cos, "row", 0), (sin, "row", 0)]
    def inproj(w, n_cols, col, epilogue, extras, dtype, name):
        return _matmul(h, w, n_cols, col // MM_TN, epilogue, extras, [dtype], tm_mm, name, w_transposed=True)[0]

    q = inproj(w_in_t, Q_W, 0, _ep_rope_q, rope_extras, BF16, "inproj_q")
    k = inproj(w_in_t, KV_W, COL_K, _ep_rope_k, rope_extras, F32, "inproj_k")
    v = inproj(w_in_t, KV_W, COL_V, _ep_plain, [], F32, "inproj_v")
    u_rkv = inproj(w_in_t, 3 * RW_W, COL_RW, _ep_plain, [], F32, "inproj_rkv")
    u_lora = inproj(w_in_t, LORA_PAD, COL_LORA, _ep_plain, [], F32, "inproj_lora")
    gates = inproj(w_in_t[COL_GATE:], 2 * d, 0, _ep_sigmoid, [], BF16, "inproj_gates")

    qp = q[:mp].reshape(bp, sp, Q_W)
    kp = k[:mp].reshape(bp, sp, KV_W)
    vp = v[:mp].reshape(bp, sp, KV_W)
    parts = []
    for g, dil in enumerate(DILATIONS):
        o_g, l_g = _attn_prompt_group(qp, kp, vp, g, dil)
        parts += [o_g.reshape(mp, KV_W), l_g.reshape(mp, KV_W)]
    oa_p = _attn_combine(parts)
    ks = k[mp:].reshape(db, ds, KV_HEADS, HEAD_DIM)
    vs = v[mp:].reshape(db, ds, KV_HEADS, HEAD_DIM)
    oa_s = _attn_sample(q[mp:].astype(F32).reshape(db, ds, N_GROUPS * KV_HEADS, HEAD_DIM), ks, vs,
                        cache_k.reshape(db, past, KV_HEADS, HEAD_DIM), cache_v.reshape(db, past, KV_HEADS, HEAD_DIM))
    o_a = jnp.concatenate([oa_p, oa_s.reshape(ms_, KV_W).astype(BF16)], axis=0)

    zl = functools.partial(jnp.zeros, dtype=F32)
    w_w = zl((LORA_PAD, RW_W)).at[:LORA_W].set(rw_w2[0])
    w_a = zl((LORA_PAD, RW_W)).at[LORA_W:LORA_W + LORA_A].set(rw_a2[0])
    w_g = zl((LORA_PAD, RW_W)).at[LORA_W + LORA_A:LORA_ALL].set(rw_g2[0])
    r_, lw_, k2_, v_, kk_, b_, g_ = _rw_prep(u_rkv, u_lora, state_shift[0], mp, sp, ds, rw_mu[0], w_w, w_a, w_g,
                                             rw_w0[0], rw_a0[0], rw_k_k[0], rw_k_a[0])

    scan_in = (r_, lw_, k2_, v_, kk_, b_)
    head_params = (rw_ln_w[0], rw_ln_b[0], rw_r_k[0])
    ob_p, wkv_p = _rw_scan(*scan_in, g_, *head_params, jnp.zeros((bp, RW_HEADS, RW_N, RW_N), F32),
                           RW_CHUNK, bp, sp)
    samp = [a[mp:].astype(F32).reshape(db, ds, RW_W).transpose(1, 2, 0) for a in scan_in]
    y_s, wkv_s = _rw_steps(*samp, jnp.transpose(state_wkv[0], (1, 2, 3, 0)))
    wkv_s = jnp.transpose(wkv_s, (3, 0, 1, 2))
    ob_s = _rw_post(y_s.transpose(2, 0, 1).reshape(ms_, RW_W), r_, k2_, v_, g_, *head_params, mp)
    o_b = jnp.concatenate([ob_p, ob_s], axis=0)

    merged = _merge(o_a, o_b, w_branch_a[0], w_branch_b[0], gates, tm_mm)
    (x1,) = _matmul(merged, w_out.reshape(d, d), d, 0, _ep_residual, [(x_all, "tile", 0)], [F32], tm_mm, "out_proj")

    w_router = jnp.concatenate([router_group_w[0], router_expert_w[0],
                                jnp.zeros((d, LANE - MOE_GROUPS - N_EXPERTS), F32)], axis=1)
    b_router = jnp.concatenate([router_group_b[0], router_expert_b[0],
                                jnp.zeros((LANE - MOE_GROUPS - N_EXPERTS,), F32)]).reshape(1, LANE)
    h2, ids, wts = _router(x1, norm_ffn_g[0], w_router, b_router)
    slot_tok, slot_w, dest, tile_expert, tile_valid, tile_rows, last_tile = _dispatch_plan(ids[:, :2], wts[:, :2])
    xs = _dispatch_rows(h2, slot_tok, tile_rows)
    yb = _moe_experts(xs, slot_w[:, None], tile_expert, tile_valid, last_tile,
                      exp_gate[0], exp_up[0], exp_down[0])
    y_prompt = _final(x1, yb, dest, norm_final_g, 0, mp).reshape(bp, sp, d)
    y_sample = _final(x1, yb, dest, norm_final_g, mp, ms_).reshape(db, ds, d)

    keep = min(BACK * DILATIONS[-1], sp)
    k_prompt = k[:mp].reshape(1, bp, sp, KV_HEADS, HEAD_DIM)[:, :, sp - keep:]
    v_prompt = v[:mp].reshape(1, bp, sp, KV_HEADS, HEAD_DIM)[:, :, sp - keep:]
    k_sample = ks.reshape(1, db, ds, KV_HEADS, HEAD_DIM)
    v_sample = vs.reshape(1, db, ds, KV_HEADS, HEAD_DIM)

    def last_rows(lo, nb, t):
        rows = lo + t - 1 + t * jnp.arange(nb, dtype=jnp.int32)
        a = jnp.take(u_rkv, rows, axis=0)
        b = jnp.take(u_lora, rows, axis=0)[:, :LORA_ALL]
        return jnp.concatenate([a, b], axis=-1)[None]

    return (y_prompt, y_sample, k_prompt, v_prompt, k_sample, v_sample,
            last_rows(0, bp, sp), last_rows(mp, db, ds), wkv_p[None], wkv_s[None])
```

```python
import functools
import math

import jax
import jax.numpy as jnp
from jax import lax
from jax.experimental import pallas as pl
from jax.experimental.pallas import tpu as pltpu

F32 = jnp.float32
BF16 = jnp.bfloat16
HI = lax.Precision.HIGHEST

D_MODEL = 4096
NORM_EPS = 1e-6
HEAD_DIM = 128
KV_HEADS = 8
DILATIONS = (1, 4, 16)
BACK = 128
N_GROUPS = 3
ROPE_THETA = 10000.0
ATT_SCALE = HEAD_DIM ** -0.5
Q_W = N_GROUPS * KV_HEADS * HEAD_DIM
KV_W = KV_HEADS * HEAD_DIM
RW_N = 64
RW_HEADS = 32
RW_W = RW_HEADS * RW_N
LORA_W, LORA_A, LORA_G = 96, 96, 256
LORA_ALL = LORA_W + LORA_A + LORA_G
RW_FEAT = 3 * RW_W + LORA_ALL
RW_GN_EPS = 64e-5
RW_DECAY_SCALE = math.exp(-0.5)
COL_K = Q_W
COL_V = Q_W + KV_W
COL_RW = Q_W + 2 * KV_W
COL_LORA = COL_RW + 3 * RW_W
COL_GATE = COL_RW + RW_FEAT
MOE_GROUPS = 8
MOE_PER_GROUP = 8
N_EXPERTS = 64
EXPERT_FF = 1024

LANE = 128
VMEM_LIMIT_BYTES = 56 * 1024 * 1024
MM_TN = 512
LORA_PAD = 512
MOE_TM = 384
MOE_TF = 256
RW_CHUNK = 64
RW_GROUP = 4
NEG = -1e30


def _cparams(*sem):
    return pltpu.CompilerParams(dimension_semantics=sem, vmem_limit_bytes=VMEM_LIMIT_BYTES)


def _dot(a, b):
    return jnp.dot(a.astype(BF16), b.astype(BF16), preferred_element_type=F32)


def _dot_nt(a, b):
    return lax.dot_general(a.astype(BF16), b.astype(BF16), (((1,), (1,)), ((), ())),
                           preferred_element_type=F32)


def _dot_tn(a, b):
    return lax.dot_general(a.astype(BF16), b.astype(BF16), (((0,), (0,)), ((), ())),
                           preferred_element_type=F32)


def _dot_hi(a, b):
    return jnp.dot(a, b, preferred_element_type=F32, precision=HI)


def _bf16_terms(x, terms):
    parts = []
    for _ in range(terms):
        p = x.astype(BF16)
        parts.append(p)
        x = x - p.astype(F32)
    return parts


def _dot_sel(x, sel, terms):
    sel = sel.astype(BF16)
    return sum(jnp.dot(p, sel, preferred_element_type=F32) for p in _bf16_terms(x, terms))


def _sel_dot(sel, x, terms):
    sel = sel.astype(BF16)
    return sum(jnp.dot(sel, p, preferred_element_type=F32) for p in _bf16_terms(x, terms))


def _sigmoid(x):
    return 1.0 / (1.0 + jnp.exp(-x))


def _div_pow2(x, n):
    return x >> (n.bit_length() - 1)


def _mod_pow2(x, n):
    return x & (n - 1)


def _rmsnorm_kernel(x_ref, g_ref, o_ref):
    x = x_ref[...]
    ms = jnp.mean(x * x, axis=-1, keepdims=True)
    o_ref[...] = (x * lax.rsqrt(ms + NORM_EPS) * g_ref[...]).astype(o_ref.dtype)


def _rmsnorm(x, g, out_dtype, tm=512):
    m, d = x.shape
    return pl.pallas_call(
        _rmsnorm_kernel,
        grid=(m // tm,),
        in_specs=[pl.BlockSpec((tm, d), lambda i: (i, 0)), pl.BlockSpec((1, d), lambda i: (0, 0))],
        out_specs=pl.BlockSpec((tm, d), lambda i: (i, 0)),
        out_shape=jax.ShapeDtypeStruct((m, d), out_dtype),
        compiler_params=_cparams("parallel"),
        name="rmsnorm",
    )(x, g.reshape(1, d))


def _mm_kernel(a_ref, w_ref, *refs, n_extra, epilogue, w_transposed):
    extra = refs[:n_extra]
    outs = refs[n_extra:-1]
    wbf = refs[-1]

    @pl.when(pl.program_id(1) == 0)
    def _():
        w = w_ref[...]
        wbf[...] = (w.T if w_transposed else w).astype(BF16)

    acc = jnp.dot(a_ref[...], wbf[...], preferred_element_type=F32)
    epilogue(acc, extra, outs)


def _matmul(a, w, n_cols, col_off_tiles, epilogue, extras, out_dtypes, tm, name, w_transposed=False):
    m, k = a.shape
    tn = MM_TN
    if w_transposed:
        w_spec = pl.BlockSpec((tn, k), lambda j, i: (j + col_off_tiles, 0))
    else:
        w_spec = pl.BlockSpec((k, tn), lambda j, i: (0, j + col_off_tiles))
    in_specs = [pl.BlockSpec((tm, k), lambda j, i: (i, 0)), w_spec]
    args = [a, w]
    for arr, kind, off in extras:
        if kind == "row":
            in_specs.append(pl.BlockSpec((tm, arr.shape[1]), lambda j, i: (i, 0)))
        elif kind == "tile":
            in_specs.append(pl.BlockSpec((tm, tn), functools.partial(lambda j, i, o: (i, j + o), o=off)))
        else:
            in_specs.append(pl.BlockSpec((1, tn), functools.partial(lambda j, i, o: (0, j + o), o=off)))
        args.append(arr)
    out_specs = [pl.BlockSpec((tm, tn), lambda j, i: (i, j)) for _ in out_dtypes]
    out_shape = [jax.ShapeDtypeStruct((m, n_cols), dt) for dt in out_dtypes]
    res = pl.pallas_call(
        functools.partial(_mm_kernel, n_extra=len(extras), epilogue=epilogue, w_transposed=w_transposed),
        grid=(n_cols // tn, m // tm),
        in_specs=in_specs,
        out_specs=out_specs,
        out_shape=out_shape,
        scratch_shapes=[pltpu.VMEM((k, tn), BF16)],
        compiler_params=_cparams("parallel", "arbitrary"),
        name=name,
    )(*args)
    return res


def _rope_tile(acc, cos, sin_signed):
    parts = []
    for h in range(MM_TN // HEAD_DIM):
        x = acc[:, h * HEAD_DIM:(h + 1) * HEAD_DIM]
        parts.append(x * cos + pltpu.roll(x, HEAD_DIM // 2, axis=1) * sin_signed)
    return jnp.concatenate(parts, axis=1)


def _ep_rope_q(acc, extra, outs):
    outs[0][...] = (_rope_tile(acc, extra[0][...], extra[1][...]) * ATT_SCALE).astype(outs[0].dtype)


def _ep_rope_k(acc, extra, outs):
    outs[0][...] = _rope_tile(acc, extra[0][...], extra[1][...]).astype(outs[0].dtype)


def _ep_plain(acc, extra, outs):
    outs[0][...] = acc.astype(outs[0].dtype)


def _ep_sigmoid(acc, extra, outs):
    outs[0][...] = _sigmoid(acc).astype(outs[0].dtype)


def _ep_residual(acc, extra, outs):
    outs[0][...] = (extra[0][...] + acc).astype(outs[0].dtype)


def _attn_prompt_kernel(q_ref, kp_ref, kc_ref, vp_ref, vc_ref, o_ref, l_ref, *, tq):
    first_neg = jnp.where(pl.program_id(2) == 0, NEG, 0.0)
    qi = lax.broadcasted_iota(jnp.int32, (BACK, BACK), 0)
    kj = lax.broadcasted_iota(jnp.int32, (BACK, BACK), 1)
    prev_band = kj >= qi
    cur_band = kj <= qi
    hs = [slice(h * HEAD_DIM, (h + 1) * HEAD_DIM) for h in range(KV_HEADS)]
    for jb in range(tq // BACK):
        rows = slice(jb * BACK, (jb + 1) * BACK)
        if jb == 0:
            k_prev = [kp_ref[0, :, s] for s in hs]
            v_prev = [vp_ref[0, :, s] for s in hs]
            prev_neg = first_neg
        else:
            prow = slice((jb - 1) * BACK, jb * BACK)
            k_prev = [kc_ref[0, prow, s] for s in hs]
            v_prev = [vc_ref[0, prow, s] for s in hs]
            prev_neg = 0.0
        qb = [q_ref[0, rows, s] for s in hs]
        s_p = [jnp.where(prev_band, _dot_nt(qb[h], k_prev[h]), NEG) + prev_neg for h in range(KV_HEADS)]
        s_c = [jnp.where(cur_band, _dot_nt(qb[h], kc_ref[0, rows, hs[h]]), NEG) for h in range(KV_HEADS)]
        m = [jnp.maximum(jnp.max(s_p[h], axis=-1, keepdims=True), jnp.max(s_c[h], axis=-1, keepdims=True))
             for h in range(KV_HEADS)]
        e_p = [jnp.exp(s_p[h] - m[h]) for h in range(KV_HEADS)]
        e_c = [jnp.exp(s_c[h] - m[h]) for h in range(KV_HEADS)]
        den = [jnp.sum(e_p[h], axis=-1, keepdims=True) + jnp.sum(e_c[h], axis=-1, keepdims=True)
               for h in range(KV_HEADS)]
        o = [(_dot(e_p[h], v_prev[h]) + _dot(e_c[h], vc_ref[0, rows, hs[h]])) / den[h] for h in range(KV_HEADS)]
        for h in range(KV_HEADS):
            o_ref[0, rows, hs[h]] = o[h]
            l_ref[0, rows, hs[h]] = jnp.broadcast_to(m[h] + jnp.log(den[h]), (BACK, HEAD_DIM))


def _attn_prompt_group(q, k, v, g, dil):
    b, s, _ = q.shape
    sub = s // dil
    tq = min(sub, 512)
    nq = sub // tq
    qv = q.reshape(b, sub, dil * Q_W)
    kv_ = k.reshape(b, sub, dil * KV_W)
    vv = v.reshape(b, sub, dil * KV_W)
    per = tq // BACK
    prev_map = lambda bi, r, i: (bi, jnp.maximum(i * per - 1, 0), r)
    cur_map = lambda bi, r, i: (bi, i, r)
    o, lse = pl.pallas_call(
        functools.partial(_attn_prompt_kernel, tq=tq),
        grid=(b, dil, nq),
        in_specs=[pl.BlockSpec((1, tq, KV_W), lambda bi, r, i: (bi, i, r * N_GROUPS + g)),
                  pl.BlockSpec((1, BACK, KV_W), prev_map),
                  pl.BlockSpec((1, tq, KV_W), cur_map),
                  pl.BlockSpec((1, BACK, KV_W), prev_map),
                  pl.BlockSpec((1, tq, KV_W), cur_map)],
        out_specs=[pl.BlockSpec((1, tq, KV_W), cur_map), pl.BlockSpec((1, tq, KV_W), cur_map)],
        out_shape=[jax.ShapeDtypeStruct((b, sub, dil * KV_W), F32)] * 2,
        compiler_params=_cparams("parallel", "parallel", "arbitrary"),
        name=f"attn_prompt_g{g}",
    )(qv, kv_, kv_, vv, vv)
    return o.reshape(b, s, KV_W), lse.reshape(b, s, KV_W)


def _attn_combine_kernel(o0, l0, o1, l1, o2, l2, out_ref):
    la, lb, lc = l0[...], l1[...], l2[...]
    m = jnp.maximum(jnp.maximum(la, lb), lc)
    wa, wb, wc = jnp.exp(la - m), jnp.exp(lb - m), jnp.exp(lc - m)
    out = (wa * o0[...] + wb * o1[...] + wc * o2[...]) / (wa + wb + wc)
    out_ref[...] = out.astype(out_ref.dtype)


def _attn_combine(parts, tm=512):
    m, w = parts[0].shape
    spec = pl.BlockSpec((tm, w), lambda i: (i, 0))
    return pl.pallas_call(
        _attn_combine_kernel,
        grid=(m // tm,),
        in_specs=[spec] * 6,
        out_specs=spec,
        out_shape=jax.ShapeDtypeStruct((m, w), BF16),
        compiler_params=_cparams("parallel"),
        name="attn_combine",
    )(*parts)


def _attn_sample_kernel(q_ref, kn_ref, vn_ref, k0_ref, v0_ref, k1_ref, v1_ref, k2_ref, v2_ref, o_ref, *, n_new):
    tok = lax.broadcasted_iota(jnp.int32, (BACK, KV_HEADS, 1), 0)
    kc_refs = (k0_ref, k1_ref, k2_ref)
    vc_refs = (v0_ref, v1_ref, v2_ref)
    for s in range(n_new):
        outs, lses = [], []
        for g in range(N_GROUPS):
            qh = q_ref[0, s, g * KV_HEADS:(g + 1) * KV_HEADS, :][None]
            if g == 0:
                kc, vc = kc_refs[g][0], vc_refs[g][0]
                new = slice(0, s + 1)
            else:
                kc, vc = kc_refs[g][0, :, s], vc_refs[g][0, :, s]
                new = slice(s, s + 1)
            sc = jnp.sum(kc * qh, axis=-1, keepdims=True)
            if g == 0:
                sc = jnp.where(tok >= s, sc, NEG)
            sn = jnp.sum(kn_ref[0, new] * qh, axis=-1, keepdims=True)
            m = jnp.maximum(jnp.max(sc, axis=0, keepdims=True), jnp.max(sn, axis=0, keepdims=True))
            p = jnp.exp(sc - m)
            pn = jnp.exp(sn - m)
            den = jnp.sum(p, axis=0, keepdims=True) + jnp.sum(pn, axis=0, keepdims=True)
            o = (jnp.sum(p * vc, axis=0, keepdims=True)
                 + jnp.sum(pn * vn_ref[0, new], axis=0, keepdims=True)) / den
            outs.append(o)
            lses.append(m + jnp.log(den))
        mm = jnp.maximum(jnp.maximum(lses[0], lses[1]), lses[2])
        ws = [jnp.exp(l - mm) for l in lses]
        comb = (ws[0] * outs[0] + ws[1] * outs[1] + ws[2] * outs[2]) / (ws[0] + ws[1] + ws[2])
        o_ref[0, s] = comb[0].astype(o_ref.dtype)


def _attn_sample(q, k_new, v_new, cache_k, cache_v):
    db, n_new = q.shape[0], q.shape[1]
    w_buf = cache_k.shape[1]
    assert w_buf == BACK * DILATIONS[-1] and n_new <= DILATIONS[1]
    new_spec = pl.BlockSpec((1, n_new, KV_HEADS, HEAD_DIM), lambda b: (b, 0, 0, 0))
    specs = [pl.BlockSpec((1, n_new, N_GROUPS * KV_HEADS, HEAD_DIM), lambda b: (b, 0, 0, 0)), new_spec, new_spec]
    args = [q, k_new, v_new]
    for dil in DILATIONS:
        sub = w_buf // dil
        last = sub // BACK - 1
        if dil == 1:
            shape = (db, sub, KV_HEADS, HEAD_DIM)
            spec = pl.BlockSpec((1, BACK, KV_HEADS, HEAD_DIM), functools.partial(lambda b, l: (b, l, 0, 0), l=last))
        else:
            shape = (db, sub, dil, KV_HEADS, HEAD_DIM)
            spec = pl.BlockSpec((1, BACK, n_new, KV_HEADS, HEAD_DIM),
                                functools.partial(lambda b, l: (b, l, 0, 0, 0), l=last))
        for c in (cache_k, cache_v):
            specs.append(spec)
            args.append(c.reshape(shape))
    return pl.pallas_call(
        functools.partial(_attn_sample_kernel, n_new=n_new),
        grid=(db,),
        in_specs=specs,
        out_specs=new_spec,
        out_shape=jax.ShapeDtypeStruct((db, n_new, KV_HEADS, HEAD_DIM), F32),
        compiler_params=_cparams("parallel"),
        name="attn_sample",
    )(*args)


def _head_indicator(width):
    l = lax.broadcasted_iota(jnp.int32, (width, LANE), 0)
    h = lax.broadcasted_iota(jnp.int32, (width, LANE), 1)
    return (_div_pow2(l, RW_N) == h).astype(F32)


def _head_indicator_t(width):
    h = lax.broadcasted_iota(jnp.int32, (LANE, width), 0)
    l = lax.broadcasted_iota(jnp.int32, (LANE, width), 1)
    return (_div_pow2(l, RW_N) == h).astype(F32)


def _rw_prep_kernel(ur, uk, uv, ul, fpr, fpk, fpv, fpl, fsr, fsk, fsv, fsl, mur, muk, muv, mul, ww, wa, wg, w0, a0,
                    kk_, ka, r_o, lw_o, k_o, v_o, kk_o, b_o, g_o, *, n_prompt_tiles, ds):
    is_sample = pl.program_id(0) >= n_prompt_tiles

    def mixed(u_ref, fp_ref, fs_ref, mu_ref):
        u = u_ref[...]
        row = lax.broadcasted_iota(jnp.int32, u.shape, 0)
        rolled = pltpu.roll(u, 1, axis=0)
        p_prompt = jnp.where(row == 0, fp_ref[0], rolled)
        p_sample = jnp.where(_mod_pow2(row, ds) == 0, fs_ref[...], rolled)
        prev = jnp.where(is_sample, p_sample, p_prompt)
        return u + (prev - u) * mu_ref[...]

    xr = mixed(ur, fpr, fsr, mur)
    xk = mixed(uk, fpk, fsk, muk)
    xv = mixed(uv, fpv, fsv, muv)
    xl = mixed(ul, fpl, fsl, mul)
    col = lax.broadcasted_iota(jnp.int32, xl.shape, 1)
    act = jnp.where(col < LORA_W, jnp.tanh(xl),
                    jnp.where(col < LORA_W + LORA_A, xl,
                              jnp.where(col < LORA_ALL, _sigmoid(xl), 0.0)))
    zw = _dot(act, ww[...])
    za = _dot(act, wa[...])
    zg = _dot(act, wg[...])
    lw = -RW_DECAY_SCALE * _sigmoid(w0[...] + zw)
    a = _sigmoid(a0[...] + za)
    kk = xk * kk_[...]
    width = kk.shape[1]
    ss = _dot_sel(kk * kk, _head_indicator(width), 2)
    inv = 1.0 / jnp.maximum(jnp.sqrt(ss), 1e-12)
    kkn = kk * _dot_sel(inv, _head_indicator_t(width), 2)
    r_o[...] = xr.astype(r_o.dtype)
    lw_o[...] = lw
    k_o[...] = (xk * (1.0 + (a - 1.0) * ka[...])).astype(k_o.dtype)
    v_o[...] = xv.astype(v_o.dtype)
    kk_o[...] = kkn.astype(kk_o.dtype)
    b_o[...] = (kkn * a).astype(b_o.dtype)
    g_o[...] = zg.astype(g_o.dtype)


def _rw_prep(u_rkv, u_lora, shift, mp, sp, ds, mu, w_w, w_a, w_g, w0, a0, k_k, k_a, tm=512, tw=512):
    m = u_rkv.shape[0]
    nj = RW_W // tw
    n_p = mp // tm
    assert sp % tm == 0 and mp % tm == 0 and (m - mp) % tm == 0 and tm % ds == 0 and ds & (ds - 1) == 0
    mu_rkv = mu[:3 * RW_W].reshape(1, 3 * RW_W)
    lpad = LORA_PAD - LORA_ALL
    mu_l = jnp.pad(mu[3 * RW_W:], (0, lpad)).reshape(1, LORA_PAD)

    starts = jnp.arange(n_p, dtype=jnp.int32) * tm
    inside = ((starts % sp) != 0)[:, None]
    prev_idx = jnp.maximum(starts - 1, 0)

    def prompt_fix(u):
        return jnp.where(inside, jnp.take(u, prev_idx, axis=0), 0.0)[:, None, :]

    def sample_fix(first_rows):
        db, w = first_rows.shape
        return jnp.zeros((db, ds, w), F32).at[:, 0].set(first_rows).reshape(db * ds, w)

    fp_rkv, fp_l = prompt_fix(u_rkv), prompt_fix(u_lora)
    fs_rkv = sample_fix(shift[:, :3 * RW_W])
    fs_l = sample_fix(jnp.pad(shift[:, 3 * RW_W:], ((0, 0), (0, lpad))))

    def feat(off):
        return pl.BlockSpec((tm, tw), functools.partial(lambda i, j, o: (i, j + o), o=off))

    def fixp(off):
        return pl.BlockSpec((1, 1, tw), functools.partial(lambda i, j, o: (jnp.minimum(i, n_p - 1), 0, j + o), o=off))

    def fixs(off):
        return pl.BlockSpec((tm, tw), functools.partial(lambda i, j, o: (jnp.maximum(i - n_p, 0), j + o), o=off))

    def vec(off):
        return pl.BlockSpec((1, tw), functools.partial(lambda i, j, o: (0, j + o), o=off))

    lspec = pl.BlockSpec((tm, LORA_PAD), lambda i, j: (i, 0))
    lfixp = pl.BlockSpec((1, 1, LORA_PAD), lambda i, j: (jnp.minimum(i, n_p - 1), 0, 0))
    lfixs = pl.BlockSpec((tm, LORA_PAD), lambda i, j: (jnp.maximum(i - n_p, 0), 0))
    wspec = pl.BlockSpec((LORA_PAD, tw), lambda i, j: (0, j))
    in_specs = [feat(0), feat(nj), feat(2 * nj), lspec,
                fixp(0), fixp(nj), fixp(2 * nj), lfixp,
                fixs(0), fixs(nj), fixs(2 * nj), lfixs,
                vec(0), vec(nj), vec(2 * nj), pl.BlockSpec((1, LORA_PAD), lambda i, j: (0, 0)),
                wspec, wspec, wspec, vec(0), vec(0), vec(0), vec(0)]
    out_spec = pl.BlockSpec((tm, tw), lambda i, j: (i, j))
    outs = pl.pallas_call(
        functools.partial(_rw_prep_kernel, n_prompt_tiles=n_p, ds=ds),
        grid=(m // tm, nj),
        in_specs=in_specs,
        out_specs=[out_spec] * 7,
        out_shape=[jax.ShapeDtypeStruct((m, RW_W), F32 if i == 1 else BF16) for i in range(7)],
        compiler_params=_cparams("parallel", "arbitrary"),
        name="rw_prep",
    )(u_rkv, u_rkv, u_rkv, u_lora, fp_rkv, fp_rkv, fp_rkv, fp_l, fs_rkv, fs_rkv, fs_rkv, fs_l,
      mu_rkv, mu_rkv, mu_rkv, mu_l,
      w_w, w_a, w_g, w0.reshape(1, RW_W), a0.reshape(1, RW_W), k_k.reshape(1, RW_W), k_a.reshape(1, RW_W))
    return outs


def _rw_head_out(y, r, k, v, g, ln_w, ln_b, r_k):
    width = y.shape[1]
    ind, ind_t = _head_indicator(width).astype(BF16), _head_indicator_t(width).astype(BF16)
    mean = _dot_sel(_dot_sel(y, ind, 2) * (1.0 / RW_N), ind_t, 2)
    d = y - mean
    var_h = _dot_sel(d * d, ind, 2) * (1.0 / RW_N)
    rstd = _dot_sel(lax.rsqrt(var_h + RW_GN_EPS), ind_t, 2)
    yn = d * rstd * ln_w + ln_b
    bonus = _dot_sel(_dot_sel(r * k * r_k, ind, 2), ind_t, 2) * v
    return (yn + bonus) * g


def _rw_scan_kernel(r_ref, lw_ref, k_ref, v_ref, kk_ref, b_ref, g_ref, lnw_ref, lnb_ref, rk_ref, s_in,
                    o_ref, s_out, s_bd, *, chunk):
    t_id = pl.program_id(1)
    n_t = pl.num_programs(1)
    gw = RW_GROUP * RW_N
    n_groups = RW_W // gw
    rows = RW_GROUP * chunk

    ri = lax.broadcasted_iota(jnp.int32, (gw, gw), 0)
    ci = lax.broadcasted_iota(jnp.int32, (gw, gw), 1)
    state_mask = _div_pow2(ri, RW_N) == _div_pow2(ci, RW_N)

    @pl.when(t_id == 0)
    def _():
        kr = lax.broadcasted_iota(jnp.int32, (RW_N, gw), 0)
        kc = lax.broadcasted_iota(jnp.int32, (RW_N, gw), 1)
        spread = (_mod_pow2(kc, RW_N) == kr).astype(F32)
        for g in range(n_groups):
            tiled = _dot_sel(s_in[0, g * gw:(g + 1) * gw, :], spread, 3)
            s_bd[g] = jnp.where(state_mask, tiled, 0.0)

    ti = lax.broadcasted_iota(jnp.int32, (chunk, chunk), 0)
    tj = lax.broadcasted_iota(jnp.int32, (chunk, chunk), 1)
    lw = lw_ref[0]
    cum = _sel_dot((tj <= ti).astype(F32), lw, 3)
    e_pos = jnp.exp(cum)
    e_neg = jnp.exp(-cum)
    e_prev = jnp.exp(cum - lw)
    cum_last = cum[chunk - 1:chunk, :]
    e_rem = jnp.exp(cum_last - cum)
    c_all = jnp.exp(cum_last)
    r_in, k_in, b_in = r_ref[0].astype(F32), k_ref[0].astype(F32), b_ref[0].astype(F32)
    kt = kk_ref[0].astype(F32) * e_prev
    bt = b_in * e_neg
    k2t = k_in * e_neg
    rt = r_in * e_pos
    btc = b_in * e_rem
    k2tc = k_in * e_rem
    vv = v_ref[0].astype(F32)

    sr = lax.broadcasted_iota(jnp.int32, (rows, gw), 0)
    sc = lax.broadcasted_iota(jnp.int32, (rows, gw), 1)
    stack_mask = _div_pow2(sr, chunk) == _div_pow2(sc, RW_N)
    ar = lax.broadcasted_iota(jnp.int32, (rows, rows), 0)
    ac = lax.broadcasted_iota(jnp.int32, (rows, rows), 1)
    strict = ar > ac
    incl = ar >= ac
    eye = (ar == ac).astype(F32)

    def stack(x):
        return jnp.where(stack_mask, jnp.concatenate([x] * RW_GROUP, axis=0), 0.0).astype(BF16)

    gs = range(n_groups)
    lanes = [slice(g * gw, (g + 1) * gw) for g in gs]
    kr_s = [jnp.concatenate([stack(kt[:, ls]), stack(rt[:, ls])], axis=0) for ls in lanes]
    bt_s = [stack(bt[:, ls]) for ls in lanes]
    k2t_s = [stack(k2t[:, ls]) for ls in lanes]
    v_s = [stack(vv[:, ls]) for ls in lanes]
    s0 = [s_bd[g] for g in gs]
    p_b = [_dot_nt(kr_s[g], bt_s[g]) for g in gs]
    p_k = [_dot_nt(kr_s[g], k2t_s[g]) for g in gs]
    p_s = [_dot_nt(kr_s[g], s0[g]) for g in gs]
    a_b = [jnp.where(strict, p_b[g][:rows], 0.0) for g in gs]
    r_b = [jnp.where(incl, p_b[g][rows:], 0.0) for g in gs]
    ar_k = [jnp.concatenate([jnp.where(strict, p_k[g][:rows], 0.0), jnp.where(incl, p_k[g][rows:], 0.0)], axis=0)
            for g in gs]
    p_v = [_dot(ar_k[g], v_s[g]) for g in gs]
    rhs = [p_s[g][:rows] + p_v[g][:rows] for g in gs]
    inv = [eye - a_b[g] for g in gs]
    pw = a_b
    n = 2
    while n < chunk:
        pw = [_dot(pw[g], pw[g]) for g in gs]
        inv = [inv[g] + _dot(inv[g], pw[g]) for g in gs]
        n *= 2
    u = [-_dot(inv[g], rhs[g]) for g in gs]
    y_bd = [p_s[g][rows:] + p_v[g][rows:] + _dot(r_b[g], u[g]) for g in gs]
    ys = []
    for g in gs:
        y = y_bd[g][0:chunk]
        for h in range(1, RW_GROUP):
            y = y + y_bd[g][h * chunk:(h + 1) * chunk]
        ys.append(y)
    o_ref[0] = _rw_head_out(jnp.concatenate(ys, axis=1), r_in, k_in, vv, g_ref[0].astype(F32),
                            lnw_ref[...], lnb_ref[...], rk_ref[...]).astype(o_ref.dtype)
    for g in gs:
        uv = jnp.concatenate([u[g].astype(BF16), v_s[g]], axis=0)
        bk = jnp.concatenate([stack(btc[:, lanes[g]]), stack(k2tc[:, lanes[g]])], axis=0)
        s_bd[g] = s0[g] * c_all[:, lanes[g]] + _dot_tn(uv, bk)

    @pl.when(t_id == n_t - 1)
    def _():
        gr = lax.broadcasted_iota(jnp.int32, (gw, RW_N), 0)
        gc = lax.broadcasted_iota(jnp.int32, (gw, RW_N), 1)
        gather = (_mod_pow2(gr, RW_N) == gc).astype(F32)
        for g in range(n_groups):
            s_out[0, g * gw:(g + 1) * gw, :] = _dot_sel(s_bd[g], gather, 3)


def _rw_scan(r, lw, k, v, kk, b, g, ln_w, ln_b, r_k, state, chunk, bn, t):
    gw = RW_GROUP * RW_N
    nt = t // chunk
    seq = pl.BlockSpec((1, chunk, RW_W), lambda bi, ti: (0, bi * nt + ti, 0))
    vec = pl.BlockSpec((1, RW_W), lambda bi, ti: (0, 0))
    st = pl.BlockSpec((1, RW_W, RW_N), lambda bi, ti: (bi, 0, 0))
    o, s_new = pl.pallas_call(
        functools.partial(_rw_scan_kernel, chunk=chunk),
        grid=(bn, nt),
        in_specs=[seq] * 7 + [vec] * 3 + [st],
        out_specs=[seq, st],
        out_shape=[jax.ShapeDtypeStruct((1, bn * t, RW_W), BF16), jax.ShapeDtypeStruct((bn, RW_W, RW_N), F32)],
        scratch_shapes=[pltpu.VMEM((RW_W // gw, gw, gw), F32)],
        compiler_params=_cparams("parallel", "arbitrary"),
        name=f"rw_scan_c{chunk}",
    )(*(a[None] for a in (r, lw, k, v, kk, b, g)), ln_w.reshape(1, RW_W), ln_b.reshape(1, RW_W),
      r_k.reshape(1, RW_W), state.reshape(bn, RW_W, RW_N))
    return o[0], s_new.reshape(bn, RW_HEADS, RW_N, RW_N)


def _rw_step_kernel(r_ref, lw_ref, k_ref, v_ref, kk_ref, b_ref, s_in, y_ref, s_out, *, n_steps):
    sub = 8
    for t in range(n_steps):
        w = jnp.exp(lw_ref[t])
        kap, bb, k2, r = kk_ref[t], b_ref[t], k_ref[t], r_ref[t]
        src = s_in if t == 0 else s_out

        def body(blk, c, t=t, w=w, kap=kap, bb=bb, k2=k2, r=r, src=src):
            base = pl.multiple_of(blk * sub, sub)
            v_rows = v_ref[t, pl.ds(base, sub), :]
            y_rows = []
            for j in range(sub):
                sv = src[0, base + j]
                sa = -jnp.sum(sv * kap, axis=0, keepdims=True)
                sn = sv * w + sa * bb + v_rows[j:j + 1] * k2
                s_out[0, base + j] = sn
                y_rows.append(jnp.sum(sn * r, axis=0, keepdims=True))
            y_ref[t, pl.ds(base, sub), :] = jnp.concatenate(y_rows, axis=0)
            return c

        lax.fori_loop(0, RW_N // sub, body, 0)


def _rw_steps(r, lw, k, v, kk, b, state):
    n_steps, _, nb = r.shape
    assert nb % LANE == 0
    seq = pl.BlockSpec((n_steps, RW_N, nb), lambda h: (0, h, 0))
    st = pl.BlockSpec((1, RW_N, RW_N, nb), lambda h: (h, 0, 0, 0))
    return pl.pallas_call(
        functools.partial(_rw_step_kernel, n_steps=n_steps),
        grid=(RW_HEADS,),
        in_specs=[seq] * 6 + [st],
        out_specs=[seq, st],
        out_shape=[jax.ShapeDtypeStruct((n_steps, RW_W, nb), F32),
                   jax.ShapeDtypeStruct((RW_HEADS, RW_N, RW_N, nb), F32)],
        compiler_params=_cparams("parallel"),
        name="rw_steps",
    )(r, lw, k, v, kk, b, state)


def _rw_post_kernel(y_ref, r_ref, k_ref, v_ref, g_ref, lnw, lnb, rk, o_ref):
    f32 = lambda ref: ref[...].astype(F32)
    o_ref[...] = _rw_head_out(y_ref[...], f32(r_ref), f32(k_ref), f32(v_ref), f32(g_ref),
                              lnw[...], lnb[...], rk[...]).astype(o_ref.dtype)


def _rw_post(y, r, k, v, g, ln_w, ln_b, r_k, row_off, tm=512, tw=512):
    m = y.shape[0]
    off = row_off // tm
    own = pl.BlockSpec((tm, tw), lambda i, j: (i, j))
    feat = pl.BlockSpec((tm, tw), lambda i, j: (i + off, j))
    vec = pl.BlockSpec((1, tw), lambda i, j: (0, j))
    return pl.pallas_call(
        _rw_post_kernel,
        grid=(m // tm, RW_W // tw),
        in_specs=[own] + [feat] * 4 + [vec] * 3,
        out_specs=own,
        out_shape=jax.ShapeDtypeStruct((m, RW_W), BF16),
        compiler_params=_cparams("parallel", "parallel"),
        name="rw_post",
    )(y, r, k, v, g, ln_w.reshape(1, RW_W), ln_b.reshape(1, RW_W), r_k.reshape(1, RW_W))


def _merge_kernel(oa_ref, ob_ref, wa_ref, wb_ref, ga_ref, gb_ref, out_ref, wa_bf, wb_bf):
    @pl.when(pl.program_id(1) == 0)
    def _():
        wa_bf[...] = wa_ref[...].astype(BF16)
        wb_bf[...] = wb_ref[...].astype(BF16)

    ya = jnp.dot(oa_ref[...], wa_bf[...], preferred_element_type=F32)
    yb = jnp.dot(ob_ref[...], wb_bf[...], preferred_element_type=F32)
    out_ref[...] = (ga_ref[...].astype(F32) * ya + gb_ref[...].astype(F32) * yb).astype(out_ref.dtype)


def _merge(o_a, o_b, w_a, w_b, gates, tm):
    m = o_a.shape[0]
    tn = MM_TN
    nj = D_MODEL // tn
    return pl.pallas_call(
        _merge_kernel,
        grid=(nj, m // tm),
        in_specs=[pl.BlockSpec((tm, KV_W), lambda j, i: (i, 0)),
                  pl.BlockSpec((tm, RW_W), lambda j, i: (i, 0)),
                  pl.BlockSpec((KV_W, tn), lambda j, i: (0, j)),
                  pl.BlockSpec((RW_W, tn), lambda j, i: (0, j)),
                  pl.BlockSpec((tm, tn), lambda j, i: (i, j)),
                  pl.BlockSpec((tm, tn), lambda j, i: (i, j + nj))],
        out_specs=pl.BlockSpec((tm, tn), lambda j, i: (i, j)),
        out_shape=jax.ShapeDtypeStruct((m, D_MODEL), BF16),
        scratch_shapes=[pltpu.VMEM((KV_W, tn), BF16), pltpu.VMEM((RW_W, tn), BF16)],
        compiler_params=_cparams("parallel", "arbitrary"),
        name="merge",
    )(o_a, o_b, w_a, w_b, gates, gates)


def _router_kernel(x_ref, g_ref, wr_ref, br_ref, h_ref, id_ref, wt_ref):
    x = x_ref[...]
    ms = jnp.mean(x * x, axis=-1, keepdims=True)
    h = x * lax.rsqrt(ms + NORM_EPS) * g_ref[...]
    h_ref[...] = h.astype(h_ref.dtype)
    logits = _dot_hi(h, wr_ref[...]) + br_ref[...]
    lane = lax.broadcasted_iota(jnp.int32, logits.shape, 1)
    lane_f = lane.astype(F32)
    gmask = lane < MOE_GROUPS
    gl = jnp.where(gmask, logits, NEG)
    gm = jnp.max(gl, axis=-1, keepdims=True)
    gi = jnp.min(jnp.where(gl == gm, lane_f, float(LANE)), axis=-1, keepdims=True)
    g_prob = 1.0 / jnp.sum(jnp.where(gmask, jnp.exp(gl - gm), 0.0), axis=-1, keepdims=True)
    lo = MOE_GROUPS + gi * MOE_PER_GROUP
    emask = jnp.logical_and(lane_f >= lo, lane_f < lo + MOE_PER_GROUP)
    el = jnp.where(emask, logits, NEG)
    m1 = jnp.max(el, axis=-1, keepdims=True)
    i1 = jnp.min(jnp.where(el == m1, lane_f, float(LANE)), axis=-1, keepdims=True)
    el2 = jnp.where(lane_f == i1, NEG, el)
    m2 = jnp.max(el2, axis=-1, keepdims=True)
    i2 = jnp.min(jnp.where(el2 == m2, lane_f, float(LANE)), axis=-1, keepdims=True)
    t = jnp.exp(m2 - m1)
    w1 = g_prob / (1.0 + t)
    w2 = g_prob * t / (1.0 + t)
    ids = jnp.where(lane == 0, i1 - MOE_GROUPS, jnp.where(lane == 1, i2 - MOE_GROUPS, 0.0))
    id_ref[...] = ids.astype(jnp.int32)
    wt_ref[...] = jnp.where(lane == 0, w1, jnp.where(lane == 1, w2, 0.0))


def _router(x, g, w_router, b_router, tm=256):
    m, d = x.shape
    row = pl.BlockSpec((tm, d), lambda i: (i, 0))
    small = pl.BlockSpec((tm, LANE), lambda i: (i, 0))
    return pl.pallas_call(
        _router_kernel,
        grid=(m // tm,),
        in_specs=[row, pl.BlockSpec((1, d), lambda i: (0, 0)),
                  pl.BlockSpec((d, LANE), lambda i: (0, 0)), pl.BlockSpec((1, LANE), lambda i: (0, 0))],
        out_specs=[row, small, small],
        out_shape=[jax.ShapeDtypeStruct((m, d), F32), jax.ShapeDtypeStruct((m, LANE), jnp.int32),
                   jax.ShapeDtypeStruct((m, LANE), F32)],
        compiler_params=_cparams("parallel"),
        name="ffn_norm_router",
    )(x, g.reshape(1, d), w_router, b_router)


def _moe_kernel(te_ref, tv_ref, last_ref, x_ref, wg_ref, wu_ref, wd_ref, o_ref):
    i = pl.program_id(0)
    j = pl.program_id(1)
    valid = tv_ref[i] > 0

    @pl.when(j == 0)
    def _():
        o_ref[...] = jnp.zeros_like(o_ref)

    @pl.when(valid)
    def _():
        x = x_ref[...]
        gate = jnp.dot(x, wg_ref[0].astype(BF16), preferred_element_type=F32)
        up = jnp.dot(x, wu_ref[0].astype(BF16), preferred_element_type=F32)
        hidden = (gate * _sigmoid(gate) * up).astype(BF16)
        o_ref[...] += jnp.dot(hidden, wd_ref[0].astype(BF16), preferred_element_type=F32)


def _moe_experts(xs, tile_expert, tile_valid, last_tile, w_gate, w_up, w_down):
    p, d = xs.shape
    n_tiles = p // MOE_TM
    nf = EXPERT_FF // MOE_TF

    def f_idx(i, j, tv):
        return jnp.where(tv[i] > 0, j, nf - 1)

    def rows(i, j, te, tv, last):
        return (jnp.minimum(i, last[0]), 0)

    grid_spec = pltpu.PrefetchScalarGridSpec(
        num_scalar_prefetch=3,
        grid=(n_tiles, nf),
        in_specs=[pl.BlockSpec((MOE_TM, d), rows),
                  pl.BlockSpec((1, d, MOE_TF), lambda i, j, te, tv, last: (te[i], 0, f_idx(i, j, tv))),
                  pl.BlockSpec((1, d, MOE_TF), lambda i, j, te, tv, last: (te[i], 0, f_idx(i, j, tv))),
                  pl.BlockSpec((1, MOE_TF, d), lambda i, j, te, tv, last: (te[i], f_idx(i, j, tv), 0))],
        out_specs=pl.BlockSpec((MOE_TM, d), lambda i, j, te, tv, last: (i, 0)),
    )
    return pl.pallas_call(
        _moe_kernel,
        grid_spec=grid_spec,
        out_shape=jax.ShapeDtypeStruct((p, d), F32),
        compiler_params=_cparams("arbitrary", "arbitrary"),
        name="moe_experts",
    )(tile_expert, tile_valid, last_tile, xs, w_gate, w_up, w_down)


def _row_copy(src_hbm, row, dst_vmem, slot, sem):
    return pltpu.make_async_copy(src_hbm.at[pl.ds(row, 1)], dst_vmem.at[pl.ds(slot, 1)], sem)


WAIT_ROWS = 8


def _wait_rows(src_hbm, dst_vmem, sem, n):
    def wait_block(r, c):
        pltpu.make_async_copy(src_hbm.at[pl.ds(0, WAIT_ROWS)], dst_vmem.at[pl.ds(0, WAIT_ROWS)], sem).wait()
        return c

    def wait_row(r, c):
        _row_copy(src_hbm, 0, dst_vmem, 0, sem).wait()
        return c

    lax.fori_loop(0, n // WAIT_ROWS, wait_block, 0)
    lax.fori_loop(0, n % WAIT_ROWS, wait_row, 0)


def _dispatch_kernel(dest_ref, nv_ref, h_hbm, o_ref, buf, slot_tok, sem, *, n_assign):
    i = pl.program_id(0)
    n = nv_ref[i]

    @pl.when(i == 0)
    def _():
        buf[...] = jnp.zeros_like(buf)

        def invert(a, c):
            slot_tok[dest_ref[a]] = a >> 1
            return c

        lax.fori_loop(0, n_assign, invert, 0, unroll=8)

    def issue(r, c):
        _row_copy(h_hbm, slot_tok[i * MOE_TM + r], buf, r, sem).start()
        return c

    lax.fori_loop(0, n, issue, 0)
    _wait_rows(h_hbm, buf, sem, n)
    o_ref[...] = buf[...].astype(o_ref.dtype)


def _dispatch_rows(h, dest, tile_rows):
    d = h.shape[1]
    n_tiles = tile_rows.shape[0]
    p = n_tiles * MOE_TM
    grid_spec = pltpu.PrefetchScalarGridSpec(
        num_scalar_prefetch=2,
        grid=(n_tiles,),
        in_specs=[pl.BlockSpec(memory_space=pl.ANY)],
        out_specs=pl.BlockSpec((MOE_TM, d), lambda i, dst, nv: (i, 0)),
        scratch_shapes=[pltpu.VMEM((MOE_TM, d), F32), pltpu.SMEM((p,), jnp.int32), pltpu.SemaphoreType.DMA(())],
    )
    return pl.pallas_call(
        functools.partial(_dispatch_kernel, n_assign=dest.shape[0]),
        grid_spec=grid_spec,
        out_shape=jax.ShapeDtypeStruct((p, d), BF16),
        compiler_params=_cparams("arbitrary"),
        name="moe_dispatch",
    )(dest, tile_rows, h)


def _dispatch_plan(ids):
    t = ids.shape[0]
    a = t * 2
    flat_e = ids.reshape(a)
    experts = jnp.arange(N_EXPERTS, dtype=jnp.int32)
    onehot = (flat_e[:, None] == experts[None, :]).astype(jnp.int32)
    csum = jnp.cumsum(onehot, axis=0)
    counts = csum[-1]
    rank = jnp.sum(onehot * csum, axis=1) - 1
    tiles_per = (counts + MOE_TM - 1) // MOE_TM
    tile_end = jnp.cumsum(tiles_per)
    tile_start = tile_end - tiles_per
    dest = jnp.sum(onehot * tile_start[None, :], axis=1) * MOE_TM + rank
    n_tiles = -(-a // MOE_TM) + N_EXPERTS
    total = tile_end[-1]
    tile_ids = jnp.arange(n_tiles, dtype=jnp.int32)
    tile_valid = (tile_ids < total).astype(jnp.int32)
    tile_expert = jnp.sum((tile_end[None, :] <= jnp.minimum(tile_ids, total - 1)[:, None]).astype(jnp.int32), axis=1)
    tile_expert = jnp.minimum(tile_expert, N_EXPERTS - 1)
    of_tile = (tile_expert[:, None] == experts[None, :]).astype(jnp.int32)
    tile_pos = tile_ids - jnp.sum(of_tile * tile_start[None, :], axis=1)
    tile_count = jnp.sum(of_tile * counts[None, :], axis=1)
    tile_rows = jnp.where(tile_valid > 0, jnp.clip(tile_count - tile_pos * MOE_TM, 0, MOE_TM), 0)
    last_tile = (total - 1).astype(jnp.int32).reshape(1)
    return dest.astype(jnp.int32), tile_expert, tile_valid, tile_rows.astype(jnp.int32), last_tile


def _final_kernel(dest_ref, x_ref, w_ref, y_hbm, g_ref, o_ref, buf, sem, *, row_off, tm):
    base = (row_off + pl.program_id(0) * tm) * 2

    def issue(r, c):
        _row_copy(y_hbm, dest_ref[base + r], buf, (r & 1) * tm + (r >> 1), sem).start()
        return c

    lax.fori_loop(0, 2 * tm, issue, 0, unroll=8)
    _wait_rows(y_hbm, buf, sem, 2 * tm)
    w = w_ref[...]
    x = x_ref[...] + (buf[0:tm, :] * w[:, 0:1] + buf[tm:2 * tm, :] * w[:, 1:2])
    ms = jnp.mean(x * x, axis=-1, keepdims=True)
    o_ref[...] = x * lax.rsqrt(ms + NORM_EPS) * g_ref[...]


def _final(x, wts, y_rows, dest, g, row_off, rows, tm=128):
    d = x.shape[1]
    off = row_off // tm
    grid_spec = pltpu.PrefetchScalarGridSpec(
        num_scalar_prefetch=1,
        grid=(rows // tm,),
        in_specs=[pl.BlockSpec((tm, d), lambda i, dst: (i + off, 0)),
                  pl.BlockSpec((tm, LANE), lambda i, dst: (i + off, 0)),
                  pl.BlockSpec(memory_space=pl.ANY),
                  pl.BlockSpec((1, d), lambda i, dst: (0, 0))],
        out_specs=pl.BlockSpec((tm, d), lambda i, dst: (i, 0)),
        scratch_shapes=[pltpu.VMEM((2 * tm, d), F32), pltpu.SemaphoreType.DMA(())],
    )
    return pl.pallas_call(
        functools.partial(_final_kernel, row_off=row_off, tm=tm),
        grid_spec=grid_spec,
        out_shape=jax.ShapeDtypeStruct((rows, d), F32),
        compiler_params=_cparams("arbitrary"),
        name="final_norm",
    )(dest, x, wts, y_rows, g.reshape(1, d))


def _rope_tables(pos):
    half = HEAD_DIM // 2
    inv_freq = ROPE_THETA ** (-jnp.arange(half, dtype=F32) / half)
    ang = pos.astype(F32)[:, None] * inv_freq[None, :]
    cos, sin = jnp.cos(ang), jnp.sin(ang)
    return jnp.concatenate([cos, cos], axis=-1), jnp.concatenate([-sin, sin], axis=-1)


def kernel(x_prompt, x_sample, cache_k, cache_v, state_shift, state_wkv, norm_mix_g, w_in, rw_mu, rw_w0, rw_w2,
           rw_a0, rw_a2, rw_g2, rw_k_k, rw_k_a, rw_r_k, rw_ln_w, rw_ln_b, w_branch_a, w_branch_b, w_out,
           norm_ffn_g, router_group_w, router_group_b, router_expert_w, router_expert_b, exp_gate, exp_up,
           exp_down, norm_final_g):
    assert w_in.shape[0] == 1, "single-layer trunk"
    bp, sp, d = x_prompt.shape
    db, ds, _ = x_sample.shape
    mp, ms_ = bp * sp, db * ds
    m = mp + ms_
    past = cache_k.shape[2]
    tm_mm = m // 8

    x_all = jnp.concatenate([x_prompt.reshape(mp, d), x_sample.reshape(ms_, d)], axis=0)
    h = _rmsnorm(x_all, norm_mix_g[0], BF16)

    w_in_t = jnp.swapaxes(w_in, 1, 2).reshape(w_in.shape[2], d)
    cos_p, sin_p = _rope_tables(jnp.arange(sp, dtype=jnp.int32))
    cos_s, sin_s = _rope_tables(past + jnp.arange(ds, dtype=jnp.int32))
    cos = jnp.concatenate([jnp.tile(cos_p, (bp, 1)), jnp.tile(cos_s, (db, 1))], axis=0)
    sin = jnp.concatenate([jnp.tile(sin_p, (bp, 1)), jnp.tile(sin_s, (db, 1))], axis=0)
    rope_extras = [(cos, "row", 0), (sin, "row", 0)]
    def inproj(w, n_cols, col, epilogue, extras, dtype, name):
        return _matmul(h, w, n_cols, col // MM_TN, epilogue, extras, [dtype], tm_mm, name, w_transposed=True)[0]

    q = inproj(w_in_t, Q_W, 0, _ep_rope_q, rope_extras, BF16, "inproj_q")
    k = inproj(w_in_t, KV_W, COL_K, _ep_rope_k, rope_extras, F32, "inproj_k")
    v = inproj(w_in_t, KV_W, COL_V, _ep_plain, [], F32, "inproj_v")
    u_rkv = inproj(w_in_t, 3 * RW_W, COL_RW, _ep_plain, [], F32, "inproj_rkv")
    u_lora = inproj(w_in_t, LORA_PAD, COL_LORA, _ep_plain, [], F32, "inproj_lora")
    gates = inproj(w_in_t[COL_GATE:], 2 * d, 0, _ep_sigmoid, [], BF16, "inproj_gates")

    qp = q[:mp].reshape(bp, sp, Q_W)
    kp = k[:mp].reshape(bp, sp, KV_W)
    vp = v[:mp].reshape(bp, sp, KV_W)
    parts = []
    for g, dil in enumerate(DILATIONS):
        o_g, l_g = _attn_prompt_group(qp, kp, vp, g, dil)
        parts += [o_g.reshape(mp, KV_W), l_g.reshape(mp, KV_W)]
    oa_p = _attn_combine(parts)
    ks = k[mp:].reshape(db, ds, KV_HEADS, HEAD_DIM)
    vs = v[mp:].reshape(db, ds, KV_HEADS, HEAD_DIM)
    oa_s = _attn_sample(q[mp:].astype(F32).reshape(db, ds, N_GROUPS * KV_HEADS, HEAD_DIM), ks, vs,
                        cache_k.reshape(db, past, KV_HEADS, HEAD_DIM), cache_v.reshape(db, past, KV_HEADS, HEAD_DIM))
    o_a = jnp.concatenate([oa_p, oa_s.reshape(ms_, KV_W).astype(BF16)], axis=0)

    zl = functools.partial(jnp.zeros, dtype=F32)
    w_w = zl((LORA_PAD, RW_W)).at[:LORA_W].set(rw_w2[0])
    w_a = zl((LORA_PAD, RW_W)).at[LORA_W:LORA_W + LORA_A].set(rw_a2[0])
    w_g = zl((LORA_PAD, RW_W)).at[LORA_W + LORA_A:LORA_ALL].set(rw_g2[0])
    r_, lw_, k2_, v_, kk_, b_, g_ = _rw_prep(u_rkv, u_lora, state_shift[0], mp, sp, ds, rw_mu[0], w_w, w_a, w_g,
                                             rw_w0[0], rw_a0[0], rw_k_k[0], rw_k_a[0])

    scan_in = (r_, lw_, k2_, v_, kk_, b_)
    head_params = (rw_ln_w[0], rw_ln_b[0], rw_r_k[0])
    ob_p, wkv_p = _rw_scan(*scan_in, g_, *head_params, jnp.zeros((bp, RW_HEADS, RW_N, RW_N), F32),
                           RW_CHUNK, bp, sp)
    samp = [a[mp:].astype(F32).reshape(db, ds, RW_W).transpose(1, 2, 0) for a in scan_in]
    y_s, wkv_s = _rw_steps(*samp, jnp.transpose(state_wkv[0], (1, 2, 3, 0)))
    wkv_s = jnp.transpose(wkv_s, (3, 0, 1, 2))
    ob_s = _rw_post(y_s.transpose(2, 0, 1).reshape(ms_, RW_W), r_, k2_, v_, g_, *head_params, mp)
    o_b = jnp.concatenate([ob_p, ob_s], axis=0)

    merged = _merge(o_a, o_b, w_branch_a[0], w_branch_b[0], gates, tm_mm)
    (x1,) = _matmul(merged, w_out.reshape(d, d), d, 0, _ep_residual, [(x_all, "tile", 0)], [F32], tm_mm, "out_proj")

    w_router = jnp.concatenate([router_group_w[0], router_expert_w[0],
                                jnp.zeros((d, LANE - MOE_GROUPS - N_EXPERTS), F32)], axis=1)
    b_router = jnp.concatenate([router_group_b[0], router_expert_b[0],
                                jnp.zeros((LANE - MOE_GROUPS - N_EXPERTS,), F32)]).reshape(1, LANE)
    h2, ids, wts = _router(x1, norm_ffn_g[0], w_router, b_router)
    dest, tile_expert, tile_valid, tile_rows, last_tile = _dispatch_plan(ids[:, :2])
    xs = _dispatch_rows(h2, dest, tile_rows)
    yb = _moe_experts(xs, tile_expert, tile_valid, last_tile, exp_gate[0], exp_up[0], exp_down[0])
    y_prompt = _final(x1, wts, yb, dest, norm_final_g, 0, mp).reshape(bp, sp, d)
    y_sample = _final(x1, wts, yb, dest, norm_final_g, mp, ms_).reshape(db, ds, d)

    keep = min(BACK * DILATIONS[-1], sp)
    k_prompt = k[:mp].reshape(1, bp, sp, KV_HEADS, HEAD_DIM)[:, :, sp - keep:]
    v_prompt = v[:mp].reshape(1, bp, sp, KV_HEADS, HEAD_DIM)[:, :, sp - keep:]
    k_sample = ks.reshape(1, db, ds, KV_HEADS, HEAD_DIM)
    v_sample = vs.reshape(1, db, ds, KV_HEADS, HEAD_DIM)

    def last_rows(lo, nb, t):
        rows = lo + t - 1 + t * jnp.arange(nb, dtype=jnp.int32)
        a = jnp.take(u_rkv, rows, axis=0)
        b = jnp.take(u_lora, rows, axis=0)[:, :LORA_ALL]
        return jnp.concatenate([a, b], axis=-1)[None]

    return (y_prompt, y_sample, k_prompt, v_prompt, k_sample, v_sample,
            last_rows(0, bp, sp), last_rows(mp, db, ds), wkv_p[None], wkv_s[None])
```

```python
import functools
import math

import jax
import jax.numpy as jnp
from jax import lax
from jax.experimental import pallas as pl
from jax.experimental.pallas import tpu as pltpu

F32 = jnp.float32
BF16 = jnp.bfloat16
HI = lax.Precision.HIGHEST

D_MODEL = 4096
NORM_EPS = 1e-6
HEAD_DIM = 128
KV_HEADS = 8
DILATIONS = (1, 4, 16)
BACK = 128
N_GROUPS = 3
ROPE_THETA = 10000.0
ATT_SCALE = HEAD_DIM ** -0.5
Q_W = N_GROUPS * KV_HEADS * HEAD_DIM
KV_W = KV_HEADS * HEAD_DIM
RW_N = 64
RW_HEADS = 32
RW_W = RW_HEADS * RW_N
LORA_W, LORA_A, LORA_G = 96, 96, 256
LORA_ALL = LORA_W + LORA_A + LORA_G
RW_FEAT = 3 * RW_W + LORA_ALL
RW_GN_EPS = 64e-5
RW_DECAY_SCALE = math.exp(-0.5)
COL_K = Q_W
COL_V = Q_W + KV_W
COL_RW = Q_W + 2 * KV_W
COL_LORA = COL_RW + 3 * RW_W
COL_GATE = COL_RW + RW_FEAT
MOE_GROUPS = 8
MOE_PER_GROUP = 8
N_EXPERTS = 64
EXPERT_FF = 1024

LANE = 128
VMEM_LIMIT_BYTES = 56 * 1024 * 1024
MM_TN = 512
LORA_PAD = 512
MOE_TM = 384
MOE_TF = 256
RW_CHUNK = 64
RW_GROUP = 4
NEG = -1e30


def _cparams(*sem):
    return pltpu.CompilerParams(dimension_semantics=sem, vmem_limit_bytes=VMEM_LIMIT_BYTES)


def _dot(a, b):
    return jnp.dot(a.astype(BF16), b.astype(BF16), preferred_element_type=F32)


def _dot_nt(a, b):
    return lax.dot_general(a.astype(BF16), b.astype(BF16), (((1,), (1,)), ((), ())),
                           preferred_element_type=F32)


def _dot_tn(a, b):
    return lax.dot_general(a.astype(BF16), b.astype(BF16), (((0,), (0,)), ((), ())),
                           preferred_element_type=F32)


def _dot_hi(a, b):
    return jnp.dot(a, b, preferred_element_type=F32, precision=HI)


def _bf16_terms(x, terms):
    parts = []
    for _ in range(terms):
        p = x.astype(BF16)
        parts.append(p)
        x = x - p.astype(F32)
    return parts


def _dot_sel(x, sel, terms):
    sel = sel.astype(BF16)
    return sum(jnp.dot(p, sel, preferred_element_type=F32) for p in _bf16_terms(x, terms))


def _sel_dot(sel, x, terms):
    sel = sel.astype(BF16)
    return sum(jnp.dot(sel, p, preferred_element_type=F32) for p in _bf16_terms(x, terms))


def _sigmoid(x):
    return 1.0 / (1.0 + jnp.exp(-x))


def _div_pow2(x, n):
    return x >> (n.bit_length() - 1)


def _mod_pow2(x, n):
    return x & (n - 1)


def _rmsnorm_kernel(x_ref, g_ref, o_ref):
    x = x_ref[...]
    ms = jnp.mean(x * x, axis=-1, keepdims=True)
    o_ref[...] = (x * lax.rsqrt(ms + NORM_EPS) * g_ref[...]).astype(o_ref.dtype)


def _rmsnorm(x, g, out_dtype, tm=512):
    m, d = x.shape
    return pl.pallas_call(
        _rmsnorm_kernel,
        grid=(m // tm,),
        in_specs=[pl.BlockSpec((tm, d), lambda i: (i, 0)), pl.BlockSpec((1, d), lambda i: (0, 0))],
        out_specs=pl.BlockSpec((tm, d), lambda i: (i, 0)),
        out_shape=jax.ShapeDtypeStruct((m, d), out_dtype),
        compiler_params=_cparams("parallel"),
        name="rmsnorm",
    )(x, g.reshape(1, d))


def _mm_kernel(a_ref, w_ref, *refs, n_extra, epilogue, w_transposed):
    extra = refs[:n_extra]
    outs = refs[n_extra:-1]
    wbf = refs[-1]

    @pl.when(pl.program_id(1) == 0)
    def _():
        w = w_ref[...]
        wbf[...] = (w.T if w_transposed else w).astype(BF16)

    acc = jnp.dot(a_ref[...], wbf[...], preferred_element_type=F32)
    epilogue(acc, extra, outs)


def _matmul(a, w, n_cols, col_off_tiles, epilogue, extras, out_dtypes, tm, name, w_transposed=False):
    m, k = a.shape
    tn = MM_TN
    if w_transposed:
        w_spec = pl.BlockSpec((tn, k), lambda j, i: (j + col_off_tiles, 0))
    else:
        w_spec = pl.BlockSpec((k, tn), lambda j, i: (0, j + col_off_tiles))
    in_specs = [pl.BlockSpec((tm, k), lambda j, i: (i, 0)), w_spec]
    args = [a, w]
    for arr, kind, off in extras:
        if kind == "row":
            in_specs.append(pl.BlockSpec((tm, arr.shape[1]), lambda j, i: (i, 0)))
        elif kind == "tile":
            in_specs.append(pl.BlockSpec((tm, tn), functools.partial(lambda j, i, o: (i, j + o), o=off)))
        else:
            in_specs.append(pl.BlockSpec((1, tn), functools.partial(lambda j, i, o: (0, j + o), o=off)))
        args.append(arr)
    out_specs = [pl.BlockSpec((tm, tn), lambda j, i: (i, j)) for _ in out_dtypes]
    out_shape = [jax.ShapeDtypeStruct((m, n_cols), dt) for dt in out_dtypes]
    res = pl.pallas_call(
        functools.partial(_mm_kernel, n_extra=len(extras), epilogue=epilogue, w_transposed=w_transposed),
        grid=(n_cols // tn, m // tm),
        in_specs=in_specs,
        out_specs=out_specs,
        out_shape=out_shape,
        scratch_shapes=[pltpu.VMEM((k, tn), BF16)],
        compiler_params=_cparams("parallel", "arbitrary"),
        name=name,
    )(*args)
    return res


def _rope_tile(acc, cos, sin_signed):
    parts = []
    for h in range(MM_TN // HEAD_DIM):
        x = acc[:, h * HEAD_DIM:(h + 1) * HEAD_DIM]
        parts.append(x * cos + pltpu.roll(x, HEAD_DIM // 2, axis=1) * sin_signed)
    return jnp.concatenate(parts, axis=1)


def _ep_rope_q(acc, extra, outs):
    outs[0][...] = (_rope_tile(acc, extra[0][...], extra[1][...]) * ATT_SCALE).astype(outs[0].dtype)


def _ep_rope_k(acc, extra, outs):
    outs[0][...] = _rope_tile(acc, extra[0][...], extra[1][...]).astype(outs[0].dtype)


def _ep_plain(acc, extra, outs):
    outs[0][...] = acc.astype(outs[0].dtype)


def _ep_sigmoid(acc, extra, outs):
    outs[0][...] = _sigmoid(acc).astype(outs[0].dtype)


def _ep_residual(acc, extra, outs):
    outs[0][...] = (extra[0][...] + acc).astype(outs[0].dtype)


def _attn_prompt_kernel(q_ref, kp_ref, kc_ref, vp_ref, vc_ref, o_ref, l_ref, *, tq):
    first_neg = jnp.where(pl.program_id(2) == 0, NEG, 0.0)
    qi = lax.broadcasted_iota(jnp.int32, (BACK, BACK), 0)
    kj = lax.broadcasted_iota(jnp.int32, (BACK, BACK), 1)
    prev_band = kj >= qi
    cur_band = kj <= qi
    hs = [slice(h * HEAD_DIM, (h + 1) * HEAD_DIM) for h in range(KV_HEADS)]
    for jb in range(tq // BACK):
        rows = slice(jb * BACK, (jb + 1) * BACK)
        if jb == 0:
            k_prev = [kp_ref[0, :, s] for s in hs]
            v_prev = [vp_ref[0, :, s] for s in hs]
            prev_neg = first_neg
        else:
            prow = slice((jb - 1) * BACK, jb * BACK)
            k_prev = [kc_ref[0, prow, s] for s in hs]
            v_prev = [vc_ref[0, prow, s] for s in hs]
            prev_neg = 0.0
        qb = [q_ref[0, rows, s] for s in hs]
        s_p = [jnp.where(prev_band, _dot_nt(qb[h], k_prev[h]), NEG) + prev_neg for h in range(KV_HEADS)]
        s_c = [jnp.where(cur_band, _dot_nt(qb[h], kc_ref[0, rows, hs[h]]), NEG) for h in range(KV_HEADS)]
        m = [jnp.maximum(jnp.max(s_p[h], axis=-1, keepdims=True), jnp.max(s_c[h], axis=-1, keepdims=True))
             for h in range(KV_HEADS)]
        e_p = [jnp.exp(s_p[h] - m[h]) for h in range(KV_HEADS)]
        e_c = [jnp.exp(s_c[h] - m[h]) for h in range(KV_HEADS)]
        den = [jnp.sum(e_p[h], axis=-1, keepdims=True) + jnp.sum(e_c[h], axis=-1, keepdims=True)
               for h in range(KV_HEADS)]
        o = [(_dot(e_p[h], v_prev[h]) + _dot(e_c[h], vc_ref[0, rows, hs[h]])) / den[h] for h in range(KV_HEADS)]
        for h in range(KV_HEADS):
            o_ref[0, rows, hs[h]] = o[h]
            l_ref[0, rows, hs[h]] = jnp.broadcast_to(m[h] + jnp.log(den[h]), (BACK, HEAD_DIM))


def _attn_prompt_group(q, k, v, g, dil):
    b, s, _ = q.shape
    sub = s // dil
    tq = min(sub, 512)
    nq = sub // tq
    qv = q.reshape(b, sub, dil * Q_W)
    kv_ = k.reshape(b, sub, dil * KV_W)
    vv = v.reshape(b, sub, dil * KV_W)
    per = tq // BACK
    prev_map = lambda bi, r, i: (bi, jnp.maximum(i * per - 1, 0), r)
    cur_map = lambda bi, r, i: (bi, i, r)
    o, lse = pl.pallas_call(
        functools.partial(_attn_prompt_kernel, tq=tq),
        grid=(b, dil, nq),
        in_specs=[pl.BlockSpec((1, tq, KV_W), lambda bi, r, i: (bi, i, r * N_GROUPS + g)),
                  pl.BlockSpec((1, BACK, KV_W), prev_map),
                  pl.BlockSpec((1, tq, KV_W), cur_map),
                  pl.BlockSpec((1, BACK, KV_W), prev_map),
                  pl.BlockSpec((1, tq, KV_W), cur_map)],
        out_specs=[pl.BlockSpec((1, tq, KV_W), cur_map), pl.BlockSpec((1, tq, KV_W), cur_map)],
        out_shape=[jax.ShapeDtypeStruct((b, sub, dil * KV_W), F32)] * 2,
        compiler_params=_cparams("parallel", "parallel", "arbitrary"),
        name=f"attn_prompt_g{g}",
    )(qv, kv_, kv_, vv, vv)
    return o.reshape(b, s, KV_W), lse.reshape(b, s, KV_W)


def _attn_dilated_kernel(q_ref, k_ref, v_ref, o_ref, l_ref, qf, *, dil, sub):
    qf[...] = q_ref[0].astype(F32)
    nb = sub // BACK
    qi = lax.broadcasted_iota(jnp.int32, (BACK, BACK), 0)
    kj = lax.broadcasted_iota(jnp.int32, (BACK, BACK), 1)
    prev_band = kj >= qi
    cur_band = kj <= qi
    blocks = [(r, jb) for r in range(dil) for jb in range(nb)]

    def rows(r, jb):
        return pl.ds(r + dil * jb * BACK, BACK, stride=dil)

    qb = {blk: qf[rows(*blk), :] for blk in blocks}
    kc = {blk: k_ref[0, rows(*blk), :] for blk in blocks}
    vc = {blk: v_ref[0, rows(*blk), :] for blk in blocks}
    s_c = {blk: jnp.where(cur_band, _dot_nt(qb[blk], kc[blk]), NEG) for blk in blocks}
    s_p = {(r, jb): jnp.where(prev_band, _dot_nt(qb[(r, jb)], kc[(r, jb - 1)]), NEG)
           for (r, jb) in blocks if jb > 0}
    m, e_c, e_p, den, o = {}, {}, {}, {}, {}
    for blk in blocks:
        m[blk] = jnp.max(s_c[blk], axis=-1, keepdims=True)
        if blk in s_p:
            m[blk] = jnp.maximum(m[blk], jnp.max(s_p[blk], axis=-1, keepdims=True))
    for blk in blocks:
        e_c[blk] = jnp.exp(s_c[blk] - m[blk])
        den[blk] = jnp.sum(e_c[blk], axis=-1, keepdims=True)
        if blk in s_p:
            e_p[blk] = jnp.exp(s_p[blk] - m[blk])
            den[blk] = den[blk] + jnp.sum(e_p[blk], axis=-1, keepdims=True)
    for blk in blocks:
        acc = _dot(e_c[blk], vc[blk])
        if blk in e_p:
            acc = acc + _dot(e_p[blk], vc[(blk[0], blk[1] - 1)])
        o[blk] = acc / den[blk]
    for blk in blocks:
        o_ref[0, rows(*blk), :] = o[blk]
        l_ref[0, rows(*blk), :] = jnp.broadcast_to(m[blk] + jnp.log(den[blk]), (BACK, HEAD_DIM))


def _attn_dilated_group(q, k, v, g, dil):
    b, s, _ = q.shape
    sub = s // dil
    col = lambda bi, h: (bi, 0, h)
    blk = pl.BlockSpec((1, s, HEAD_DIM), col)
    return pl.pallas_call(
        functools.partial(_attn_dilated_kernel, dil=dil, sub=sub),
        grid=(b, KV_HEADS),
        in_specs=[pl.BlockSpec((1, s, HEAD_DIM), lambda bi, h: (bi, 0, g * KV_HEADS + h)), blk, blk],
        out_specs=[blk, blk],
        out_shape=[jax.ShapeDtypeStruct((b, s, KV_W), F32)] * 2,
        scratch_shapes=[pltpu.VMEM((s, HEAD_DIM), F32)],
        compiler_params=_cparams("parallel", "parallel"),
        name=f"attn_prompt_g{g}",
    )(q, k, v)


def _attn_combine_kernel(o0, l0, o1, l1, o2, l2, out_ref):
    la, lb, lc = l0[...], l1[...], l2[...]
    m = jnp.maximum(jnp.maximum(la, lb), lc)
    wa, wb, wc = jnp.exp(la - m), jnp.exp(lb - m), jnp.exp(lc - m)
    out = (wa * o0[...] + wb * o1[...] + wc * o2[...]) / (wa + wb + wc)
    out_ref[...] = out.astype(out_ref.dtype)


def _attn_combine(parts, tm=512):
    m, w = parts[0].shape
    spec = pl.BlockSpec((tm, w), lambda i: (i, 0))
    return pl.pallas_call(
        _attn_combine_kernel,
        grid=(m // tm,),
        in_specs=[spec] * 6,
        out_specs=spec,
        out_shape=jax.ShapeDtypeStruct((m, w), BF16),
        compiler_params=_cparams("parallel"),
        name="attn_combine",
    )(*parts)


def _attn_sample_kernel(q_ref, kn_ref, vn_ref, k0_ref, v0_ref, k1_ref, v1_ref, k2_ref, v2_ref, o_ref, *, n_new):
    tok = lax.broadcasted_iota(jnp.int32, (BACK, KV_HEADS, 1), 0)
    kc_refs = (k0_ref, k1_ref, k2_ref)
    vc_refs = (v0_ref, v1_ref, v2_ref)
    for s in range(n_new):
        outs, lses = [], []
        for g in range(N_GROUPS):
            qh = q_ref[0, s, g * KV_HEADS:(g + 1) * KV_HEADS, :][None]
            if g == 0:
                kc, vc = kc_refs[g][0], vc_refs[g][0]
                new = slice(0, s + 1)
            else:
                kc, vc = kc_refs[g][0, :, s], vc_refs[g][0, :, s]
                new = slice(s, s + 1)
            sc = jnp.sum(kc * qh, axis=-1, keepdims=True)
            if g == 0:
                sc = jnp.where(tok >= s, sc, NEG)
            sn = jnp.sum(kn_ref[0, new] * qh, axis=-1, keepdims=True)
            m = jnp.maximum(jnp.max(sc, axis=0, keepdims=True), jnp.max(sn, axis=0, keepdims=True))
            p = jnp.exp(sc - m)
            pn = jnp.exp(sn - m)
            den = jnp.sum(p, axis=0, keepdims=True) + jnp.sum(pn, axis=0, keepdims=True)
            o = (jnp.sum(p * vc, axis=0, keepdims=True)
                 + jnp.sum(pn * vn_ref[0, new], axis=0, keepdims=True)) / den
            outs.append(o)
            lses.append(m + jnp.log(den))
        mm = jnp.maximum(jnp.maximum(lses[0], lses[1]), lses[2])
        ws = [jnp.exp(l - mm) for l in lses]
        comb = (ws[0] * outs[0] + ws[1] * outs[1] + ws[2] * outs[2]) / (ws[0] + ws[1] + ws[2])
        o_ref[0, s] = comb[0].astype(o_ref.dtype)


def _attn_sample(q, k_new, v_new, cache_k, cache_v):
    db, n_new = q.shape[0], q.shape[1]
    w_buf = cache_k.shape[1]
    assert w_buf == BACK * DILATIONS[-1] and n_new <= DILATIONS[1]
    new_spec = pl.BlockSpec((1, n_new, KV_HEADS, HEAD_DIM), lambda b: (b, 0, 0, 0))
    specs = [pl.BlockSpec((1, n_new, N_GROUPS * KV_HEADS, HEAD_DIM), lambda b: (b, 0, 0, 0)), new_spec, new_spec]
    args = [q, k_new, v_new]
    for dil in DILATIONS:
        sub = w_buf // dil
        last = sub // BACK - 1
        if dil == 1:
            shape = (db, sub, KV_HEADS, HEAD_DIM)
            spec = pl.BlockSpec((1, BACK, KV_HEADS, HEAD_DIM), functools.partial(lambda b, l: (b, l, 0, 0), l=last))
        else:
            shape = (db, sub, dil, KV_HEADS, HEAD_DIM)
            spec = pl.BlockSpec((1, BACK, n_new, KV_HEADS, HEAD_DIM),
                                functools.partial(lambda b, l: (b, l, 0, 0, 0), l=last))
        for c in (cache_k, cache_v):
            specs.append(spec)
            args.append(c.reshape(shape))
    return pl.pallas_call(
        functools.partial(_attn_sample_kernel, n_new=n_new),
        grid=(db,),
        in_specs=specs,
        out_specs=new_spec,
        out_shape=jax.ShapeDtypeStruct((db, n_new, KV_HEADS, HEAD_DIM), F32),
        compiler_params=_cparams("parallel"),
        name="attn_sample",
    )(*args)


def _head_indicator(width):
    l = lax.broadcasted_iota(jnp.int32, (width, LANE), 0)
    h = lax.broadcasted_iota(jnp.int32, (width, LANE), 1)
    return (_div_pow2(l, RW_N) == h).astype(F32)


def _head_indicator_t(width):
    h = lax.broadcasted_iota(jnp.int32, (LANE, width), 0)
    l = lax.broadcasted_iota(jnp.int32, (LANE, width), 1)
    return (_div_pow2(l, RW_N) == h).astype(F32)


def _rw_prep_kernel(ur, uk, uv, ul, fpr, fpk, fpv, fpl, fsr, fsk, fsv, fsl, mur, muk, muv, mul, ww, wa, wg, w0, a0,
                    kk_, ka, r_o, lw_o, k_o, v_o, kk_o, b_o, g_o, *, n_prompt_tiles, ds):
    is_sample = pl.program_id(0) >= n_prompt_tiles

    def mixed(u_ref, fp_ref, fs_ref, mu_ref):
        u = u_ref[...]
        row = lax.broadcasted_iota(jnp.int32, u.shape, 0)
        rolled = pltpu.roll(u, 1, axis=0)
        p_prompt = jnp.where(row == 0, fp_ref[0], rolled)
        p_sample = jnp.where(_mod_pow2(row, ds) == 0, fs_ref[...], rolled)
        prev = jnp.where(is_sample, p_sample, p_prompt)
        return u + (prev - u) * mu_ref[...]

    xr = mixed(ur, fpr, fsr, mur)
    xk = mixed(uk, fpk, fsk, muk)
    xv = mixed(uv, fpv, fsv, muv)
    xl = mixed(ul, fpl, fsl, mul)
    col = lax.broadcasted_iota(jnp.int32, xl.shape, 1)
    act = jnp.where(col < LORA_W, jnp.tanh(xl),
                    jnp.where(col < LORA_W + LORA_A, xl,
                              jnp.where(col < LORA_ALL, _sigmoid(xl), 0.0)))
    zw = _dot(act, ww[...])
    za = _dot(act, wa[...])
    zg = _dot(act, wg[...])
    lw = -RW_DECAY_SCALE * _sigmoid(w0[...] + zw)
    a = _sigmoid(a0[...] + za)
    kk = xk * kk_[...]
    width = kk.shape[1]
    ss = _dot_sel(kk * kk, _head_indicator(width), 2)
    inv = 1.0 / jnp.maximum(jnp.sqrt(ss), 1e-12)
    kkn = kk * _dot_sel(inv, _head_indicator_t(width), 2)
    r_o[...] = xr.astype(r_o.dtype)
    lw_o[...] = lw
    k_o[...] = (xk * (1.0 + (a - 1.0) * ka[...])).astype(k_o.dtype)
    v_o[...] = xv.astype(v_o.dtype)
    kk_o[...] = kkn.astype(kk_o.dtype)
    b_o[...] = (kkn * a).astype(b_o.dtype)
    g_o[...] = zg.astype(g_o.dtype)


def _rw_prep(u_rkv, u_lora, shift, mp, sp, ds, mu, w_w, w_a, w_g, w0, a0, k_k, k_a, tm=512, tw=512):
    m = u_rkv.shape[0]
    nj = RW_W // tw
    n_p = mp // tm
    assert sp % tm == 0 and mp % tm == 0 and (m - mp) % tm == 0 and tm % ds == 0 and ds & (ds - 1) == 0
    mu_rkv = mu[:3 * RW_W].reshape(1, 3 * RW_W)
    lpad = LORA_PAD - LORA_ALL
    mu_l = jnp.pad(mu[3 * RW_W:], (0, lpad)).reshape(1, LORA_PAD)

    starts = jnp.arange(n_p, dtype=jnp.int32) * tm
    inside = ((starts % sp) != 0)[:, None]
    prev_idx = jnp.maximum(starts - 1, 0)

    def prompt_fix(u):
        return jnp.where(inside, jnp.take(u, prev_idx, axis=0), 0.0)[:, None, :]

    def sample_fix(first_rows):
        db, w = first_rows.shape
        return jnp.zeros((db, ds, w), F32).at[:, 0].set(first_rows).reshape(db * ds, w)

    fp_rkv, fp_l = prompt_fix(u_rkv), prompt_fix(u_lora)
    fs_rkv = sample_fix(shift[:, :3 * RW_W])
    fs_l = sample_fix(jnp.pad(shift[:, 3 * RW_W:], ((0, 0), (0, lpad))))

    def feat(off):
        return pl.BlockSpec((tm, tw), functools.partial(lambda i, j, o: (i, j + o), o=off))

    def fixp(off):
        return pl.BlockSpec((1, 1, tw), functools.partial(lambda i, j, o: (jnp.minimum(i, n_p - 1), 0, j + o), o=off))

    def fixs(off):
        return pl.BlockSpec((tm, tw), functools.partial(lambda i, j, o: (jnp.maximum(i - n_p, 0), j + o), o=off))

    def vec(off):
        return pl.BlockSpec((1, tw), functools.partial(lambda i, j, o: (0, j + o), o=off))

    lspec = pl.BlockSpec((tm, LORA_PAD), lambda i, j: (i, 0))
    lfixp = pl.BlockSpec((1, 1, LORA_PAD), lambda i, j: (jnp.minimum(i, n_p - 1), 0, 0))
    lfixs = pl.BlockSpec((tm, LORA_PAD), lambda i, j: (jnp.maximum(i - n_p, 0), 0))
    wspec = pl.BlockSpec((LORA_PAD, tw), lambda i, j: (0, j))
    in_specs = [feat(0), feat(nj), feat(2 * nj), lspec,
                fixp(0), fixp(nj), fixp(2 * nj), lfixp,
                fixs(0), fixs(nj), fixs(2 * nj), lfixs,
                vec(0), vec(nj), vec(2 * nj), pl.BlockSpec((1, LORA_PAD), lambda i, j: (0, 0)),
                wspec, wspec, wspec, vec(0), vec(0), vec(0), vec(0)]
    out_spec = pl.BlockSpec((tm, tw), lambda i, j: (i, j))
    outs = pl.pallas_call(
        functools.partial(_rw_prep_kernel, n_prompt_tiles=n_p, ds=ds),
        grid=(m // tm, nj),
        in_specs=in_specs,
        out_specs=[out_spec] * 7,
        out_shape=[jax.ShapeDtypeStruct((m, RW_W), F32 if i == 1 else BF16) for i in range(7)],
        compiler_params=_cparams("parallel", "arbitrary"),
        name="rw_prep",
    )(u_rkv, u_rkv, u_rkv, u_lora, fp_rkv, fp_rkv, fp_rkv, fp_l, fs_rkv, fs_rkv, fs_rkv, fs_l,
      mu_rkv, mu_rkv, mu_rkv, mu_l,
      w_w, w_a, w_g, w0.reshape(1, RW_W), a0.reshape(1, RW_W), k_k.reshape(1, RW_W), k_a.reshape(1, RW_W))
    return outs


def _rw_head_out(y, r, k, v, g, ln_w, ln_b, r_k):
    width = y.shape[1]
    ind, ind_t = _head_indicator(width).astype(BF16), _head_indicator_t(width).astype(BF16)
    mean = _dot_sel(_dot_sel(y, ind, 2) * (1.0 / RW_N), ind_t, 2)
    d = y - mean
    var_h = _dot_sel(d * d, ind, 2) * (1.0 / RW_N)
    rstd = _dot_sel(lax.rsqrt(var_h + RW_GN_EPS), ind_t, 2)
    yn = d * rstd * ln_w + ln_b
    bonus = _dot_sel(_dot_sel(r * k * r_k, ind, 2), ind_t, 2) * v
    return (yn + bonus) * g


def _rw_scan_kernel(r_ref, lw_ref, k_ref, v_ref, kk_ref, b_ref, g_ref, lnw_ref, lnb_ref, rk_ref, s_in,
                    o_ref, s_out, s_bd, *, chunk):
    t_id = pl.program_id(1)
    n_t = pl.num_programs(1)
    gw = RW_GROUP * RW_N
    n_groups = RW_W // gw
    rows = RW_GROUP * chunk

    ri = lax.broadcasted_iota(jnp.int32, (gw, gw), 0)
    ci = lax.broadcasted_iota(jnp.int32, (gw, gw), 1)
    state_mask = _div_pow2(ri, RW_N) == _div_pow2(ci, RW_N)

    @pl.when(t_id == 0)
    def _():
        kr = lax.broadcasted_iota(jnp.int32, (RW_N, gw), 0)
        kc = lax.broadcasted_iota(jnp.int32, (RW_N, gw), 1)
        spread = (_mod_pow2(kc, RW_N) == kr).astype(F32)
        for g in range(n_groups):
            tiled = _dot_sel(s_in[0, g * gw:(g + 1) * gw, :], spread, 3)
            s_bd[g] = jnp.where(state_mask, tiled, 0.0)

    ti = lax.broadcasted_iota(jnp.int32, (chunk, chunk), 0)
    tj = lax.broadcasted_iota(jnp.int32, (chunk, chunk), 1)
    lw = lw_ref[0]
    cum = _sel_dot((tj <= ti).astype(F32), lw, 3)
    e_pos = jnp.exp(cum)
    e_neg = jnp.exp(-cum)
    e_prev = jnp.exp(cum - lw)
    cum_last = cum[chunk - 1:chunk, :]
    e_rem = jnp.exp(cum_last - cum)
    c_all = jnp.exp(cum_last)
    r_in, k_in, b_in = r_ref[0].astype(F32), k_ref[0].astype(F32), b_ref[0].astype(F32)
    kt = kk_ref[0].astype(F32) * e_prev
    bt = b_in * e_neg
    k2t = k_in * e_neg
    rt = r_in * e_pos
    btc = b_in * e_rem
    k2tc = k_in * e_rem
    vv = v_ref[0].astype(F32)

    sr = lax.broadcasted_iota(jnp.int32, (rows, gw), 0)
    sc = lax.broadcasted_iota(jnp.int32, (rows, gw), 1)
    stack_mask = _div_pow2(sr, chunk) == _div_pow2(sc, RW_N)
    ar = lax.broadcasted_iota(jnp.int32, (rows, rows), 0)
    ac = lax.broadcasted_iota(jnp.int32, (rows, rows), 1)
    strict = ar > ac
    incl = ar >= ac
    eye = (ar == ac).astype(F32)

    def stack(x):
        return jnp.where(stack_mask, jnp.concatenate([x] * RW_GROUP, axis=0), 0.0).astype(BF16)

    gs = range(n_groups)
    lanes = [slice(g * gw, (g + 1) * gw) for g in gs]
    kr_s = [jnp.concatenate([stack(kt[:, ls]), stack(rt[:, ls])], axis=0) for ls in lanes]
    bt_s = [stack(bt[:, ls]) for ls in lanes]
    k2t_s = [stack(k2t[:, ls]) for ls in lanes]
    v_s = [stack(vv[:, ls]) for ls in lanes]
    s0 = [s_bd[g] for g in gs]
    p_b = [_dot_nt(kr_s[g], bt_s[g]) for g in gs]
    p_k = [_dot_nt(kr_s[g], k2t_s[g]) for g in gs]
    p_s = [_dot_nt(kr_s[g], s0[g]) for g in gs]
    a_b = [jnp.where(strict, p_b[g][:rows], 0.0) for g in gs]
    r_b = [jnp.where(incl, p_b[g][rows:], 0.0) for g in gs]
    ar_k = [jnp.concatenate([jnp.where(strict, p_k[g][:rows], 0.0), jnp.where(incl, p_k[g][rows:], 0.0)], axis=0)
            for g in gs]
    p_v = [_dot(ar_k[g], v_s[g]) for g in gs]
    rhs = [p_s[g][:rows] + p_v[g][:rows] for g in gs]
    inv = [eye - a_b[g] for g in gs]
    pw = a_b
    n = 2
    while n < chunk:
        pw = [_dot(pw[g], pw[g]) for g in gs]
        inv = [inv[g] + _dot(inv[g], pw[g]) for g in gs]
        n *= 2
    u = [-_dot(inv[g], rhs[g]) for g in gs]
    y_bd = [p_s[g][rows:] + p_v[g][rows:] + _dot(r_b[g], u[g]) for g in gs]
    ys = []
    for g in gs:
        y = y_bd[g][0:chunk]
        for h in range(1, RW_GROUP):
            y = y + y_bd[g][h * chunk:(h + 1) * chunk]
        ys.append(y)
    o_ref[0] = _rw_head_out(jnp.concatenate(ys, axis=1), r_in, k_in, vv, g_ref[0].astype(F32),
                            lnw_ref[...], lnb_ref[...], rk_ref[...]).astype(o_ref.dtype)
    for g in gs:
        uv = jnp.concatenate([u[g].astype(BF16), v_s[g]], axis=0)
        bk = jnp.concatenate([stack(btc[:, lanes[g]]), stack(k2tc[:, lanes[g]])], axis=0)
        s_bd[g] = s0[g] * c_all[:, lanes[g]] + _dot_tn(uv, bk)

    @pl.when(t_id == n_t - 1)
    def _():
        gr = lax.broadcasted_iota(jnp.int32, (gw, RW_N), 0)
        gc = lax.broadcasted_iota(jnp.int32, (gw, RW_N), 1)
        gather = (_mod_pow2(gr, RW_N) == gc).astype(F32)
        for g in range(n_groups):
            s_out[0, g * gw:(g + 1) * gw, :] = _dot_sel(s_bd[g], gather, 3)


def _rw_scan(r, lw, k, v, kk, b, g, ln_w, ln_b, r_k, state, chunk, bn, t):
    gw = RW_GROUP * RW_N
    nt = t // chunk
    seq = pl.BlockSpec((1, chunk, RW_W), lambda bi, ti: (0, bi * nt + ti, 0))
    vec = pl.BlockSpec((1, RW_W), lambda bi, ti: (0, 0))
    st = pl.BlockSpec((1, RW_W, RW_N), lambda bi, ti: (bi, 0, 0))
    o, s_new = pl.pallas_call(
        functools.partial(_rw_scan_kernel, chunk=chunk),
        grid=(bn, nt),
        in_specs=[seq] * 7 + [vec] * 3 + [st],
        out_specs=[seq, st],
        out_shape=[jax.ShapeDtypeStruct((1, bn * t, RW_W), BF16), jax.ShapeDtypeStruct((bn, RW_W, RW_N), F32)],
        scratch_shapes=[pltpu.VMEM((RW_W // gw, gw, gw), F32)],
        compiler_params=_cparams("parallel", "arbitrary"),
        name=f"rw_scan_c{chunk}",
    )(*(a[None] for a in (r, lw, k, v, kk, b, g)), ln_w.reshape(1, RW_W), ln_b.reshape(1, RW_W),
      r_k.reshape(1, RW_W), state.reshape(bn, RW_W, RW_N))
    return o[0], s_new.reshape(bn, RW_HEADS, RW_N, RW_N)


def _rw_step_kernel(r_ref, lw_ref, k_ref, v_ref, kk_ref, b_ref, s_in, y_ref, s_out, *, n_steps):
    sub = 8
    for t in range(n_steps):
        w = jnp.exp(lw_ref[t])
        kap, bb, k2, r = kk_ref[t], b_ref[t], k_ref[t], r_ref[t]
        src = s_in if t == 0 else s_out

        def body(blk, c, t=t, w=w, kap=kap, bb=bb, k2=k2, r=r, src=src):
            base = pl.multiple_of(blk * sub, sub)
            v_rows = v_ref[t, pl.ds(base, sub), :]
            y_rows = []
            for j in range(sub):
                sv = src[0, base + j]
                sa = -jnp.sum(sv * kap, axis=0, keepdims=True)
                sn = sv * w + sa * bb + v_rows[j:j + 1] * k2
                s_out[0, base + j] = sn
                y_rows.append(jnp.sum(sn * r, axis=0, keepdims=True))
            y_ref[t, pl.ds(base, sub), :] = jnp.concatenate(y_rows, axis=0)
            return c

        lax.fori_loop(0, RW_N // sub, body, 0)


def _rw_steps(r, lw, k, v, kk, b, state):
    n_steps, _, nb = r.shape
    assert nb % LANE == 0
    seq = pl.BlockSpec((n_steps, RW_N, nb), lambda h: (0, h, 0))
    st = pl.BlockSpec((1, RW_N, RW_N, nb), lambda h: (h, 0, 0, 0))
    return pl.pallas_call(
        functools.partial(_rw_step_kernel, n_steps=n_steps),
        grid=(RW_HEADS,),
        in_specs=[seq] * 6 + [st],
        out_specs=[seq, st],
        out_shape=[jax.ShapeDtypeStruct((n_steps, RW_W, nb), F32),
                   jax.ShapeDtypeStruct((RW_HEADS, RW_N, RW_N, nb), F32)],
        compiler_params=_cparams("parallel"),
        name="rw_steps",
    )(r, lw, k, v, kk, b, state)


def _rw_post_kernel(y_ref, r_ref, k_ref, v_ref, g_ref, lnw, lnb, rk, o_ref):
    f32 = lambda ref: ref[...].astype(F32)
    o_ref[...] = _rw_head_out(y_ref[...], f32(r_ref), f32(k_ref), f32(v_ref), f32(g_ref),
                              lnw[...], lnb[...], rk[...]).astype(o_ref.dtype)


def _rw_post(y, r, k, v, g, ln_w, ln_b, r_k, row_off, tm=512, tw=512):
    m = y.shape[0]
    off = row_off // tm
    own = pl.BlockSpec((tm, tw), lambda i, j: (i, j))
    feat = pl.BlockSpec((tm, tw), lambda i, j: (i + off, j))
    vec = pl.BlockSpec((1, tw), lambda i, j: (0, j))
    return pl.pallas_call(
        _rw_post_kernel,
        grid=(m // tm, RW_W // tw),
        in_specs=[own] + [feat] * 4 + [vec] * 3,
        out_specs=own,
        out_shape=jax.ShapeDtypeStruct((m, RW_W), BF16),
        compiler_params=_cparams("parallel", "parallel"),
        name="rw_post",
    )(y, r, k, v, g, ln_w.reshape(1, RW_W), ln_b.reshape(1, RW_W), r_k.reshape(1, RW_W))


def _merge_kernel(oa_ref, ob_ref, wa_ref, wb_ref, ga_ref, gb_ref, out_ref, wa_bf, wb_bf):
    @pl.when(pl.program_id(1) == 0)
    def _():
        wa_bf[...] = wa_ref[...].astype(BF16)
        wb_bf[...] = wb_ref[...].astype(BF16)

    ya = jnp.dot(oa_ref[...], wa_bf[...], preferred_element_type=F32)
    yb = jnp.dot(ob_ref[...], wb_bf[...], preferred_element_type=F32)
    out_ref[...] = (ga_ref[...].astype(F32) * ya + gb_ref[...].astype(F32) * yb).astype(out_ref.dtype)


def _merge(o_a, o_b, w_a, w_b, gates, tm):
    m = o_a.shape[0]
    tn = MM_TN
    nj = D_MODEL // tn
    return pl.pallas_call(
        _merge_kernel,
        grid=(nj, m // tm),
        in_specs=[pl.BlockSpec((tm, KV_W), lambda j, i: (i, 0)),
                  pl.BlockSpec((tm, RW_W), lambda j, i: (i, 0)),
                  pl.BlockSpec((KV_W, tn), lambda j, i: (0, j)),
                  pl.BlockSpec((RW_W, tn), lambda j, i: (0, j)),
                  pl.BlockSpec((tm, tn), lambda j, i: (i, j)),
                  pl.BlockSpec((tm, tn), lambda j, i: (i, j + nj))],
        out_specs=pl.BlockSpec((tm, tn), lambda j, i: (i, j)),
        out_shape=jax.ShapeDtypeStruct((m, D_MODEL), BF16),
        scratch_shapes=[pltpu.VMEM((KV_W, tn), BF16), pltpu.VMEM((RW_W, tn), BF16)],
        compiler_params=_cparams("parallel", "arbitrary"),
        name="merge",
    )(o_a, o_b, w_a, w_b, gates, gates)


def _router_kernel(x_ref, g_ref, wr_ref, br_ref, h_ref, id_ref, wt_ref):
    x = x_ref[...]
    ms = jnp.mean(x * x, axis=-1, keepdims=True)
    h = x * lax.rsqrt(ms + NORM_EPS) * g_ref[...]
    h_ref[...] = h.astype(h_ref.dtype)
    logits = _dot_hi(h, wr_ref[...]) + br_ref[...]
    lane = lax.broadcasted_iota(jnp.int32, logits.shape, 1)
    lane_f = lane.astype(F32)
    gmask = lane < MOE_GROUPS
    gl = jnp.where(gmask, logits, NEG)
    gm = jnp.max(gl, axis=-1, keepdims=True)
    gi = jnp.min(jnp.where(gl == gm, lane_f, float(LANE)), axis=-1, keepdims=True)
    g_prob = 1.0 / jnp.sum(jnp.where(gmask, jnp.exp(gl - gm), 0.0), axis=-1, keepdims=True)
    lo = MOE_GROUPS + gi * MOE_PER_GROUP
    emask = jnp.logical_and(lane_f >= lo, lane_f < lo + MOE_PER_GROUP)
    el = jnp.where(emask, logits, NEG)
    m1 = jnp.max(el, axis=-1, keepdims=True)
    i1 = jnp.min(jnp.where(el == m1, lane_f, float(LANE)), axis=-1, keepdims=True)
    el2 = jnp.where(lane_f == i1, NEG, el)
    m2 = jnp.max(el2, axis=-1, keepdims=True)
    i2 = jnp.min(jnp.where(el2 == m2, lane_f, float(LANE)), axis=-1, keepdims=True)
    t = jnp.exp(m2 - m1)
    w1 = g_prob / (1.0 + t)
    w2 = g_prob * t / (1.0 + t)
    ids = jnp.where(lane == 0, i1 - MOE_GROUPS, jnp.where(lane == 1, i2 - MOE_GROUPS, 0.0))
    id_ref[...] = ids.astype(jnp.int32)
    wt_ref[...] = jnp.where(lane == 0, w1, jnp.where(lane == 1, w2, 0.0))


def _router(x, g, w_router, b_router, tm=256):
    m, d = x.shape
    row = pl.BlockSpec((tm, d), lambda i: (i, 0))
    small = pl.BlockSpec((tm, LANE), lambda i: (i, 0))
    return pl.pallas_call(
        _router_kernel,
        grid=(m // tm,),
        in_specs=[row, pl.BlockSpec((1, d), lambda i: (0, 0)),
                  pl.BlockSpec((d, LANE), lambda i: (0, 0)), pl.BlockSpec((1, LANE), lambda i: (0, 0))],
        out_specs=[row, small, small],
        out_shape=[jax.ShapeDtypeStruct((m, d), F32), jax.ShapeDtypeStruct((m, LANE), jnp.int32),
                   jax.ShapeDtypeStruct((m, LANE), F32)],
        compiler_params=_cparams("parallel"),
        name="ffn_norm_router",
    )(x, g.reshape(1, d), w_router, b_router)


def _moe_kernel(te_ref, tv_ref, last_ref, x_ref, wg_ref, wu_ref, wd_ref, o_ref):
    i = pl.program_id(0)
    j = pl.program_id(1)
    valid = tv_ref[i] > 0

    @pl.when(j == 0)
    def _():
        o_ref[...] = jnp.zeros_like(o_ref)

    @pl.when(valid)
    def _():
        x = x_ref[...]
        gate = jnp.dot(x, wg_ref[0].astype(BF16), preferred_element_type=F32)
        up = jnp.dot(x, wu_ref[0].astype(BF16), preferred_element_type=F32)
        hidden = (gate * _sigmoid(gate) * up).astype(BF16)
        o_ref[...] += jnp.dot(hidden, wd_ref[0].astype(BF16), preferred_element_type=F32)


def _moe_experts(xs, tile_expert, tile_valid, last_tile, w_gate, w_up, w_down):
    p, d = xs.shape
    n_tiles = p // MOE_TM
    nf = EXPERT_FF // MOE_TF

    def f_idx(i, j, tv):
        return jnp.where(tv[i] > 0, j, nf - 1)

    def rows(i, j, te, tv, last):
        return (jnp.minimum(i, last[0]), 0)

    grid_spec = pltpu.PrefetchScalarGridSpec(
        num_scalar_prefetch=3,
        grid=(n_tiles, nf),
        in_specs=[pl.BlockSpec((MOE_TM, d), rows),
                  pl.BlockSpec((1, d, MOE_TF), lambda i, j, te, tv, last: (te[i], 0, f_idx(i, j, tv))),
                  pl.BlockSpec((1, d, MOE_TF), lambda i, j, te, tv, last: (te[i], 0, f_idx(i, j, tv))),
                  pl.BlockSpec((1, MOE_TF, d), lambda i, j, te, tv, last: (te[i], f_idx(i, j, tv), 0))],
        out_specs=pl.BlockSpec((MOE_TM, d), lambda i, j, te, tv, last: (i, 0)),
    )
    return pl.pallas_call(
        _moe_kernel,
        grid_spec=grid_spec,
        out_shape=jax.ShapeDtypeStruct((p, d), F32),
        compiler_params=_cparams("arbitrary", "arbitrary"),
        name="moe_experts",
    )(tile_expert, tile_valid, last_tile, xs, w_gate, w_up, w_down)


def _row_copy(src_hbm, row, dst_vmem, slot, sem):
    return pltpu.make_async_copy(src_hbm.at[pl.ds(row, 1)], dst_vmem.at[pl.ds(slot, 1)], sem)


WAIT_ROWS = 8


def _wait_rows(src_hbm, dst_vmem, sem, n):
    def wait_block(r, c):
        pltpu.make_async_copy(src_hbm.at[pl.ds(0, WAIT_ROWS)], dst_vmem.at[pl.ds(0, WAIT_ROWS)], sem).wait()
        return c

    def wait_row(r, c):
        _row_copy(src_hbm, 0, dst_vmem, 0, sem).wait()
        return c

    lax.fori_loop(0, n // WAIT_ROWS, wait_block, 0)
    lax.fori_loop(0, n % WAIT_ROWS, wait_row, 0)


def _dispatch_kernel(dest_ref, nv_ref, h_hbm, o_ref, buf, slot_tok, sem, *, n_assign):
    i = pl.program_id(0)
    n = nv_ref[i]

    @pl.when(i == 0)
    def _():
        buf[...] = jnp.zeros_like(buf)

        def invert(a, c):
            slot_tok[dest_ref[a]] = a >> 1
            return c

        lax.fori_loop(0, n_assign, invert, 0, unroll=8)

    def issue(r, c):
        _row_copy(h_hbm, slot_tok[i * MOE_TM + r], buf, r, sem).start()
        return c

    lax.fori_loop(0, n, issue, 0)
    _wait_rows(h_hbm, buf, sem, n)
    o_ref[...] = buf[...].astype(o_ref.dtype)


def _dispatch_rows(h, dest, tile_rows):
    d = h.shape[1]
    n_tiles = tile_rows.shape[0]
    p = n_tiles * MOE_TM
    grid_spec = pltpu.PrefetchScalarGridSpec(
        num_scalar_prefetch=2,
        grid=(n_tiles,),
        in_specs=[pl.BlockSpec(memory_space=pl.ANY)],
        out_specs=pl.BlockSpec((MOE_TM, d), lambda i, dst, nv: (i, 0)),
        scratch_shapes=[pltpu.VMEM((MOE_TM, d), F32), pltpu.SMEM((p,), jnp.int32), pltpu.SemaphoreType.DMA(())],
    )
    return pl.pallas_call(
        functools.partial(_dispatch_kernel, n_assign=dest.shape[0]),
        grid_spec=grid_spec,
        out_shape=jax.ShapeDtypeStruct((p, d), BF16),
        compiler_params=_cparams("arbitrary"),
        name="moe_dispatch",
    )(dest, tile_rows, h)


def _dispatch_plan(ids):
    t = ids.shape[0]
    a = t * 2
    flat_e = ids.reshape(a)
    experts = jnp.arange(N_EXPERTS, dtype=jnp.int32)
    onehot = (flat_e[:, None] == experts[None, :]).astype(jnp.int32)
    csum = jnp.cumsum(onehot, axis=0)
    counts = csum[-1]
    rank = jnp.sum(onehot * csum, axis=1) - 1
    tiles_per = (counts + MOE_TM - 1) // MOE_TM
    tile_end = jnp.cumsum(tiles_per)
    tile_start = tile_end - tiles_per
    dest = jnp.sum(onehot * tile_start[None, :], axis=1) * MOE_TM + rank
    n_tiles = -(-a // MOE_TM) + N_EXPERTS
    total = tile_end[-1]
    tile_ids = jnp.arange(n_tiles, dtype=jnp.int32)
    tile_valid = (tile_ids < total).astype(jnp.int32)
    tile_expert = jnp.sum((tile_end[None, :] <= jnp.minimum(tile_ids, total - 1)[:, None]).astype(jnp.int32), axis=1)
    tile_expert = jnp.minimum(tile_expert, N_EXPERTS - 1)
    of_tile = (tile_expert[:, None] == experts[None, :]).astype(jnp.int32)
    tile_pos = tile_ids - jnp.sum(of_tile * tile_start[None, :], axis=1)
    tile_count = jnp.sum(of_tile * counts[None, :], axis=1)
    tile_rows = jnp.where(tile_valid > 0, jnp.clip(tile_count - tile_pos * MOE_TM, 0, MOE_TM), 0)
    last_tile = (total - 1).astype(jnp.int32).reshape(1)
    return dest.astype(jnp.int32), tile_expert, tile_valid, tile_rows.astype(jnp.int32), last_tile


def _final_kernel(dest_ref, x_ref, w_ref, y_hbm, g_ref, o_ref, buf, sem, *, row_off, tm):
    base = (row_off + pl.program_id(0) * tm) * 2

    def issue(r, c):
        _row_copy(y_hbm, dest_ref[base + r], buf, (r & 1) * tm + (r >> 1), sem).start()
        return c

    lax.fori_loop(0, 2 * tm, issue, 0, unroll=8)
    _wait_rows(y_hbm, buf, sem, 2 * tm)
    w = w_ref[...]
    x = x_ref[...] + (buf[0:tm, :] * w[:, 0:1] + buf[tm:2 * tm, :] * w[:, 1:2])
    ms = jnp.mean(x * x, axis=-1, keepdims=True)
    o_ref[...] = x * lax.rsqrt(ms + NORM_EPS) * g_ref[...]


def _final(x, wts, y_rows, dest, g, row_off, rows, tm=128):
    d = x.shape[1]
    off = row_off // tm
    grid_spec = pltpu.PrefetchScalarGridSpec(
        num_scalar_prefetch=1,
        grid=(rows // tm,),
        in_specs=[pl.BlockSpec((tm, d), lambda i, dst: (i + off, 0)),
                  pl.BlockSpec((tm, LANE), lambda i, dst: (i + off, 0)),
                  pl.BlockSpec(memory_space=pl.ANY),
                  pl.BlockSpec((1, d), lambda i, dst: (0, 0))],
        out_specs=pl.BlockSpec((tm, d), lambda i, dst: (i, 0)),
        scratch_shapes=[pltpu.VMEM((2 * tm, d), F32), pltpu.SemaphoreType.DMA(())],
    )
    return pl.pallas_call(
        functools.partial(_final_kernel, row_off=row_off, tm=tm),
        grid_spec=grid_spec,
        out_shape=jax.ShapeDtypeStruct((rows, d), F32),
        compiler_params=_cparams("arbitrary"),
        name="final_norm",
    )(dest, x, wts, y_rows, g.reshape(1, d))


def _rope_tables(pos):
    half = HEAD_DIM // 2
    inv_freq = ROPE_THETA ** (-jnp.arange(half, dtype=F32) / half)
    ang = pos.astype(F32)[:, None] * inv_freq[None, :]
    cos, sin = jnp.cos(ang), jnp.sin(ang)
    return jnp.concatenate([cos, cos], axis=-1), jnp.concatenate([-sin, sin], axis=-1)


def kernel(x_prompt, x_sample, cache_k, cache_v, state_shift, state_wkv, norm_mix_g, w_in, rw_mu, rw_w0, rw_w2,
           rw_a0, rw_a2, rw_g2, rw_k_k, rw_k_a, rw_r_k, rw_ln_w, rw_ln_b, w_branch_a, w_branch_b, w_out,
           norm_ffn_g, router_group_w, router_group_b, router_expert_w, router_expert_b, exp_gate, exp_up,
           exp_down, norm_final_g):
    assert w_in.shape[0] == 1, "single-layer trunk"
    bp, sp, d = x_prompt.shape
    db, ds, _ = x_sample.shape
    mp, ms_ = bp * sp, db * ds
    m = mp + ms_
    past = cache_k.shape[2]
    tm_mm = m // 8

    x_all = jnp.concatenate([x_prompt.reshape(mp, d), x_sample.reshape(ms_, d)], axis=0)
    h = _rmsnorm(x_all, norm_mix_g[0], BF16)

    w_in_t = jnp.swapaxes(w_in, 1, 2).reshape(w_in.shape[2], d)
    cos_p, sin_p = _rope_tables(jnp.arange(sp, dtype=jnp.int32))
    cos_s, sin_s = _rope_tables(past + jnp.arange(ds, dtype=jnp.int32))
    cos = jnp.concatenate([jnp.tile(cos_p, (bp, 1)), jnp.tile(cos_s, (db, 1))], axis=0)
    sin = jnp.concatenate([jnp.tile(sin_p, (bp, 1)), jnp.tile(sin_s, (db, 1))], axis=0)
    rope_extras = [(cos, "row", 0), (sin, "row", 0)]
    def inproj(w, n_cols, col, epilogue, extras, dtype, name):
        return _matmul(h, w, n_cols, col // MM_TN, epilogue, extras, [dtype], tm_mm, name, w_transposed=True)[0]

    q = inproj(w_in_t, Q_W, 0, _ep_rope_q, rope_extras, BF16, "inproj_q")
    k = inproj(w_in_t, KV_W, COL_K, _ep_rope_k, rope_extras, F32, "inproj_k")
    v = inproj(w_in_t, KV_W, COL_V, _ep_plain, [], F32, "inproj_v")
    u_rkv = inproj(w_in_t, 3 * RW_W, COL_RW, _ep_plain, [], F32, "inproj_rkv")
    u_lora = inproj(w_in_t, LORA_PAD, COL_LORA, _ep_plain, [], F32, "inproj_lora")
    gates = inproj(w_in_t[COL_GATE:], 2 * d, 0, _ep_sigmoid, [], BF16, "inproj_gates")

    qp = q[:mp].reshape(bp, sp, Q_W)
    kp = k[:mp].reshape(bp, sp, KV_W)
    vp = v[:mp].reshape(bp, sp, KV_W)
    parts = []
    for g, dil in enumerate(DILATIONS):
        o_g, l_g = (_attn_prompt_group if dil == 1 else _attn_dilated_group)(qp, kp, vp, g, dil)
        parts += [o_g.reshape(mp, KV_W), l_g.reshape(mp, KV_W)]
    oa_p = _attn_combine(parts)
    ks = k[mp:].reshape(db, ds, KV_HEADS, HEAD_DIM)
    vs = v[mp:].reshape(db, ds, KV_HEADS, HEAD_DIM)
    oa_s = _attn_sample(q[mp:].astype(F32).reshape(db, ds, N_GROUPS * KV_HEADS, HEAD_DIM), ks, vs,
                        cache_k.reshape(db, past, KV_HEADS, HEAD_DIM), cache_v.reshape(db, past, KV_HEADS, HEAD_DIM))
    o_a = jnp.concatenate([oa_p, oa_s.reshape(ms_, KV_W).astype(BF16)], axis=0)

    zl = functools.partial(jnp.zeros, dtype=F32)
    w_w = zl((LORA_PAD, RW_W)).at[:LORA_W].set(rw_w2[0])
    w_a = zl((LORA_PAD, RW_W)).at[LORA_W:LORA_W + LORA_A].set(rw_a2[0])
    w_g = zl((LORA_PAD, RW_W)).at[LORA_W + LORA_A:LORA_ALL].set(rw_g2[0])
    r_, lw_, k2_, v_, kk_, b_, g_ = _rw_prep(u_rkv, u_lora, state_shift[0], mp, sp, ds, rw_mu[0], w_w, w_a, w_g,
                                             rw_w0[0], rw_a0[0], rw_k_k[0], rw_k_a[0])

    scan_in = (r_, lw_, k2_, v_, kk_, b_)
    head_params = (rw_ln_w[0], rw_ln_b[0], rw_r_k[0])
    ob_p, wkv_p = _rw_scan(*scan_in, g_, *head_params, jnp.zeros((bp, RW_HEADS, RW_N, RW_N), F32),
                           RW_CHUNK, bp, sp)
    samp = [a[mp:].astype(F32).reshape(db, ds, RW_W).transpose(1, 2, 0) for a in scan_in]
    y_s, wkv_s = _rw_steps(*samp, jnp.transpose(state_wkv[0], (1, 2, 3, 0)))
    wkv_s = jnp.transpose(wkv_s, (3, 0, 1, 2))
    ob_s = _rw_post(y_s.transpose(2, 0, 1).reshape(ms_, RW_W), r_, k2_, v_, g_, *head_params, mp)
    o_b = jnp.concatenate([ob_p, ob_s], axis=0)

    merged = _merge(o_a, o_b, w_branch_a[0], w_branch_b[0], gates, tm_mm)
    (x1,) = _matmul(merged, w_out.reshape(d, d), d, 0, _ep_residual, [(x_all, "tile", 0)], [F32], tm_mm, "out_proj")

    w_router = jnp.concatenate([router_group_w[0], router_expert_w[0],
                                jnp.zeros((d, LANE - MOE_GROUPS - N_EXPERTS), F32)], axis=1)
    b_router = jnp.concatenate([router_group_b[0], router_expert_b[0],
                                jnp.zeros((LANE - MOE_GROUPS - N_EXPERTS,), F32)]).reshape(1, LANE)
    h2, ids, wts = _router(x1, norm_ffn_g[0], w_router, b_router)
    dest, tile_expert, tile_valid, tile_rows, last_tile = _dispatch_plan(ids[:, :2])
    xs = _dispatch_rows(h2, dest, tile_rows)
    yb = _moe_experts(xs, tile_expert, tile_valid, last_tile, exp_gate[0], exp_up[0], exp_down[0])
    y_prompt = _final(x1, wts, yb, dest, norm_final_g, 0, mp).reshape(bp, sp, d)
    y_sample = _final(x1, wts, yb, dest, norm_final_g, mp, ms_).reshape(db, ds, d)

    keep = min(BACK * DILATIONS[-1], sp)
    k_prompt = k[:mp].reshape(1, bp, sp, KV_HEADS, HEAD_DIM)[:, :, sp - keep:]
    v_prompt = v[:mp].reshape(1, bp, sp, KV_HEADS, HEAD_DIM)[:, :, sp - keep:]
    k_sample = ks.reshape(1, db, ds, KV_HEADS, HEAD_DIM)
    v_sample = vs.reshape(1, db, ds, KV_HEADS, HEAD_DIM)

    def last_rows(lo, nb, t):
        rows = lo + t - 1 + t * jnp.arange(nb, dtype=jnp.int32)
        a = jnp.take(u_rkv, rows, axis=0)
        b = jnp.take(u_lora, rows, axis=0)[:, :LORA_ALL]
        return jnp.concatenate([a, b], axis=-1)[None]

    return (y_prompt, y_sample, k_prompt, v_prompt, k_sample, v_sample,
            last_rows(0, bp, sp), last_rows(mp, db, ds), wkv_p[None], wkv_s[None])
```

```python
import functools
import math

import jax
import jax.numpy as jnp
from jax import lax
from jax.experimental import pallas as pl
from jax.experimental.pallas import tpu as pltpu

F32 = jnp.float32
BF16 = jnp.bfloat16
HI = lax.Precision.HIGHEST

D_MODEL = 4096
NORM_EPS = 1e-6
HEAD_DIM = 128
KV_HEADS = 8
DILATIONS = (1, 4, 16)
BACK = 128
N_GROUPS = 3
ROPE_THETA = 10000.0
ATT_SCALE = HEAD_DIM ** -0.5
Q_W = N_GROUPS * KV_HEADS * HEAD_DIM
KV_W = KV_HEADS * HEAD_DIM
RW_N = 64
RW_HEADS = 32
RW_W = RW_HEADS * RW_N
LORA_W, LORA_A, LORA_G = 96, 96, 256
LORA_ALL = LORA_W + LORA_A + LORA_G
RW_FEAT = 3 * RW_W + LORA_ALL
RW_GN_EPS = 64e-5
RW_DECAY_SCALE = math.exp(-0.5)
COL_K = Q_W
COL_V = Q_W + KV_W
COL_RW = Q_W + 2 * KV_W
COL_LORA = COL_RW + 3 * RW_W
COL_GATE = COL_RW + RW_FEAT
MOE_GROUPS = 8
MOE_PER_GROUP = 8
N_EXPERTS = 64
EXPERT_FF = 1024

LANE = 128
VMEM_LIMIT_BYTES = 56 * 1024 * 1024
MM_TN = 512
LORA_PAD = 512
MOE_TM = 384
MOE_TF = 256
RW_CHUNK = 64
RW_GROUP = 4
NEG = -1e30


def _cparams(*sem):
    return pltpu.CompilerParams(dimension_semantics=sem, vmem_limit_bytes=VMEM_LIMIT_BYTES)


def _dot(a, b):
    return jnp.dot(a.astype(BF16), b.astype(BF16), preferred_element_type=F32)


def _dot_nt(a, b):
    return lax.dot_general(a.astype(BF16), b.astype(BF16), (((1,), (1,)), ((), ())),
                           preferred_element_type=F32)


def _dot_tn(a, b):
    return lax.dot_general(a.astype(BF16), b.astype(BF16), (((0,), (0,)), ((), ())),
                           preferred_element_type=F32)


def _dot_hi(a, b):
    return jnp.dot(a, b, preferred_element_type=F32, precision=HI)


def _bf16_terms(x, terms):
    parts = []
    for _ in range(terms):
        p = x.astype(BF16)
        parts.append(p)
        x = x - p.astype(F32)
    return parts


def _dot_sel(x, sel, terms):
    sel = sel.astype(BF16)
    return sum(jnp.dot(p, sel, preferred_element_type=F32) for p in _bf16_terms(x, terms))


def _sel_dot(sel, x, terms):
    sel = sel.astype(BF16)
    return sum(jnp.dot(sel, p, preferred_element_type=F32) for p in _bf16_terms(x, terms))


def _sigmoid(x):
    return 1.0 / (1.0 + jnp.exp(-x))


def _div_pow2(x, n):
    return x >> (n.bit_length() - 1)


def _mod_pow2(x, n):
    return x & (n - 1)


def _rmsnorm_kernel(x_ref, g_ref, o_ref):
    x = x_ref[...]
    ms = jnp.mean(x * x, axis=-1, keepdims=True)
    o_ref[...] = (x * lax.rsqrt(ms + NORM_EPS) * g_ref[...]).astype(o_ref.dtype)


def _rmsnorm(x, g, out_dtype, tm=512):
    m, d = x.shape
    return pl.pallas_call(
        _rmsnorm_kernel,
        grid=(m // tm,),
        in_specs=[pl.BlockSpec((tm, d), lambda i: (i, 0)), pl.BlockSpec((1, d), lambda i: (0, 0))],
        out_specs=pl.BlockSpec((tm, d), lambda i: (i, 0)),
        out_shape=jax.ShapeDtypeStruct((m, d), out_dtype),
        compiler_params=_cparams("parallel"),
        name="rmsnorm",
    )(x, g.reshape(1, d))


def _mm_kernel(a_ref, w_ref, *refs, n_extra, epilogue, w_transposed):
    extra = refs[:n_extra]
    outs = refs[n_extra:-1]
    wbf = refs[-1]

    @pl.when(pl.program_id(1) == 0)
    def _():
        w = w_ref[...]
        wbf[...] = (w.T if w_transposed else w).astype(BF16)

    acc = jnp.dot(a_ref[...], wbf[...], preferred_element_type=F32)
    epilogue(acc, extra, outs)


def _matmul(a, w, n_cols, col_off_tiles, epilogue, extras, out_dtypes, tm, name, w_transposed=False,
            w_row_off=None):
    m, k = a.shape
    tn = MM_TN
    if w_row_off is not None:
        assert w_transposed and w_row_off % 8 == 0
        w_spec = pl.BlockSpec((pl.Element(tn), pl.Element(k)), lambda j, i: (pl.multiple_of(w_row_off + j * tn, 8), 0))
    elif w_transposed:
        w_spec = pl.BlockSpec((tn, k), lambda j, i: (j + col_off_tiles, 0))
    else:
        w_spec = pl.BlockSpec((k, tn), lambda j, i: (0, j + col_off_tiles))
    in_specs = [pl.BlockSpec((tm, k), lambda j, i: (i, 0)), w_spec]
    args = [a, w]
    for arr, kind, off in extras:
        if kind == "row":
            in_specs.append(pl.BlockSpec((tm, arr.shape[1]), lambda j, i: (i, 0)))
        elif kind == "tile":
            in_specs.append(pl.BlockSpec((tm, tn), functools.partial(lambda j, i, o: (i, j + o), o=off)))
        else:
            in_specs.append(pl.BlockSpec((1, tn), functools.partial(lambda j, i, o: (0, j + o), o=off)))
        args.append(arr)
    out_specs = [pl.BlockSpec((tm, tn), lambda j, i: (i, j)) for _ in out_dtypes]
    out_shape = [jax.ShapeDtypeStruct((m, n_cols), dt) for dt in out_dtypes]
    res = pl.pallas_call(
        functools.partial(_mm_kernel, n_extra=len(extras), epilogue=epilogue, w_transposed=w_transposed),
        grid=(n_cols // tn, m // tm),
        in_specs=in_specs,
        out_specs=out_specs,
        out_shape=out_shape,
        scratch_shapes=[pltpu.VMEM((k, tn), BF16)],
        compiler_params=_cparams("parallel", "arbitrary"),
        name=name,
    )(*args)
    return res


def _rope_tile(acc, cos, sin_signed):
    parts = []
    for h in range(MM_TN // HEAD_DIM):
        x = acc[:, h * HEAD_DIM:(h + 1) * HEAD_DIM]
        parts.append(x * cos + pltpu.roll(x, HEAD_DIM // 2, axis=1) * sin_signed)
    return jnp.concatenate(parts, axis=1)


def _ep_rope_q(acc, extra, outs):
    outs[0][...] = (_rope_tile(acc, extra[0][...], extra[1][...]) * ATT_SCALE).astype(outs[0].dtype)


def _ep_rope_k(acc, extra, outs):
    outs[0][...] = _rope_tile(acc, extra[0][...], extra[1][...]).astype(outs[0].dtype)


def _ep_plain(acc, extra, outs):
    outs[0][...] = acc.astype(outs[0].dtype)


def _ep_sigmoid(acc, extra, outs):
    outs[0][...] = _sigmoid(acc).astype(outs[0].dtype)


def _ep_residual(acc, extra, outs):
    outs[0][...] = (extra[0][...] + acc).astype(outs[0].dtype)


def _attn_prompt_kernel(q_ref, kp_ref, kc_ref, vp_ref, vc_ref, o_ref, l_ref, *, tq):
    first_neg = jnp.where(pl.program_id(2) == 0, NEG, 0.0)
    qi = lax.broadcasted_iota(jnp.int32, (BACK, BACK), 0)
    kj = lax.broadcasted_iota(jnp.int32, (BACK, BACK), 1)
    prev_band = kj >= qi
    cur_band = kj <= qi
    hs = [slice(h * HEAD_DIM, (h + 1) * HEAD_DIM) for h in range(KV_HEADS)]
    for jb in range(tq // BACK):
        rows = slice(jb * BACK, (jb + 1) * BACK)
        if jb == 0:
            k_prev = [kp_ref[0, :, s] for s in hs]
            v_prev = [vp_ref[0, :, s] for s in hs]
            prev_neg = first_neg
        else:
            prow = slice((jb - 1) * BACK, jb * BACK)
            k_prev = [kc_ref[0, prow, s] for s in hs]
            v_prev = [vc_ref[0, prow, s] for s in hs]
            prev_neg = 0.0
        qb = [q_ref[0, rows, s] for s in hs]
        s_p = [jnp.where(prev_band, _dot_nt(qb[h], k_prev[h]), NEG) + prev_neg for h in range(KV_HEADS)]
        s_c = [jnp.where(cur_band, _dot_nt(qb[h], kc_ref[0, rows, hs[h]]), NEG) for h in range(KV_HEADS)]
        m = [jnp.maximum(jnp.max(s_p[h], axis=-1, keepdims=True), jnp.max(s_c[h], axis=-1, keepdims=True))
             for h in range(KV_HEADS)]
        e_p = [jnp.exp(s_p[h] - m[h]) for h in range(KV_HEADS)]
        e_c = [jnp.exp(s_c[h] - m[h]) for h in range(KV_HEADS)]
        den = [jnp.sum(e_p[h], axis=-1, keepdims=True) + jnp.sum(e_c[h], axis=-1, keepdims=True)
               for h in range(KV_HEADS)]
        o = [(_dot(e_p[h], v_prev[h]) + _dot(e_c[h], vc_ref[0, rows, hs[h]])) / den[h] for h in range(KV_HEADS)]
        for h in range(KV_HEADS):
            o_ref[0, rows, hs[h]] = o[h]
            l_ref[0, rows, hs[h]] = jnp.broadcast_to(m[h] + jnp.log(den[h]), (BACK, HEAD_DIM))


def _attn_prompt_group(q, k, v, g, dil):
    b, s, _ = q.shape
    sub = s // dil
    tq = min(sub, 512)
    nq = sub // tq
    qv = q.reshape(b, sub, dil * Q_W)
    kv_ = k.reshape(b, sub, dil * KV_W)
    vv = v.reshape(b, sub, dil * KV_W)
    per = tq // BACK
    prev_map = lambda bi, r, i: (bi, jnp.maximum(i * per - 1, 0), r)
    cur_map = lambda bi, r, i: (bi, i, r)
    o, lse = pl.pallas_call(
        functools.partial(_attn_prompt_kernel, tq=tq),
        grid=(b, dil, nq),
        in_specs=[pl.BlockSpec((1, tq, KV_W), lambda bi, r, i: (bi, i, r * N_GROUPS + g)),
                  pl.BlockSpec((1, BACK, KV_W), prev_map),
                  pl.BlockSpec((1, tq, KV_W), cur_map),
                  pl.BlockSpec((1, BACK, KV_W), prev_map),
                  pl.BlockSpec((1, tq, KV_W), cur_map)],
        out_specs=[pl.BlockSpec((1, tq, KV_W), cur_map), pl.BlockSpec((1, tq, KV_W), cur_map)],
        out_shape=[jax.ShapeDtypeStruct((b, sub, dil * KV_W), F32)] * 2,
        compiler_params=_cparams("parallel", "parallel", "arbitrary"),
        name=f"attn_prompt_g{g}",
    )(qv, kv_, kv_, vv, vv)
    return o.reshape(b, s, KV_W), lse.reshape(b, s, KV_W)


def _attn_dilated_kernel(q_ref, k_ref, v_ref, o_ref, l_ref, qf, *, dil, sub):
    qf[...] = q_ref[0].astype(F32)
    nb = sub // BACK
    qi = lax.broadcasted_iota(jnp.int32, (BACK, BACK), 0)
    kj = lax.broadcasted_iota(jnp.int32, (BACK, BACK), 1)
    prev_band = kj >= qi
    cur_band = kj <= qi
    blocks = [(r, jb) for r in range(dil) for jb in range(nb)]

    def rows(r, jb):
        return pl.ds(r + dil * jb * BACK, BACK, stride=dil)

    qb = {blk: qf[rows(*blk), :] for blk in blocks}
    kc = {blk: k_ref[0, rows(*blk), :] for blk in blocks}
    vc = {blk: v_ref[0, rows(*blk), :] for blk in blocks}
    s_c = {blk: jnp.where(cur_band, _dot_nt(qb[blk], kc[blk]), NEG) for blk in blocks}
    s_p = {(r, jb): jnp.where(prev_band, _dot_nt(qb[(r, jb)], kc[(r, jb - 1)]), NEG)
           for (r, jb) in blocks if jb > 0}
    m, e_c, e_p, den, o = {}, {}, {}, {}, {}
    for blk in blocks:
        m[blk] = jnp.max(s_c[blk], axis=-1, keepdims=True)
        if blk in s_p:
            m[blk] = jnp.maximum(m[blk], jnp.max(s_p[blk], axis=-1, keepdims=True))
    for blk in blocks:
        e_c[blk] = jnp.exp(s_c[blk] - m[blk])
        den[blk] = jnp.sum(e_c[blk], axis=-1, keepdims=True)
        if blk in s_p:
            e_p[blk] = jnp.exp(s_p[blk] - m[blk])
            den[blk] = den[blk] + jnp.sum(e_p[blk], axis=-1, keepdims=True)
    for blk in blocks:
        acc = _dot(e_c[blk], vc[blk])
        if blk in e_p:
            acc = acc + _dot(e_p[blk], vc[(blk[0], blk[1] - 1)])
        o[blk] = acc / den[blk]
    for blk in blocks:
        o_ref[0, rows(*blk), :] = o[blk]
        l_ref[0, rows(*blk), :] = jnp.broadcast_to(m[blk] + jnp.log(den[blk]), (BACK, HEAD_DIM))


def _attn_dilated_group(q, k, v, g, dil):
    b, s, _ = q.shape
    sub = s // dil
    col = lambda bi, h: (bi, 0, h)
    blk = pl.BlockSpec((1, s, HEAD_DIM), col)
    return pl.pallas_call(
        functools.partial(_attn_dilated_kernel, dil=dil, sub=sub),
        grid=(b, KV_HEADS),
        in_specs=[pl.BlockSpec((1, s, HEAD_DIM), lambda bi, h: (bi, 0, g * KV_HEADS + h)), blk, blk],
        out_specs=[blk, blk],
        out_shape=[jax.ShapeDtypeStruct((b, s, KV_W), F32)] * 2,
        scratch_shapes=[pltpu.VMEM((s, HEAD_DIM), F32)],
        compiler_params=_cparams("parallel", "parallel"),
        name=f"attn_prompt_g{g}",
    )(q, k, v)


def _attn_combine_kernel(o0, l0, o1, l1, o2, l2, out_ref):
    la, lb, lc = l0[...], l1[...], l2[...]
    m = jnp.maximum(jnp.maximum(la, lb), lc)
    wa, wb, wc = jnp.exp(la - m), jnp.exp(lb - m), jnp.exp(lc - m)
    out = (wa * o0[...] + wb * o1[...] + wc * o2[...]) / (wa + wb + wc)
    out_ref[...] = out.astype(out_ref.dtype)


def _attn_combine(parts, tm=512):
    m, w = parts[0].shape
    spec = pl.BlockSpec((tm, w), lambda i: (i, 0))
    return pl.pallas_call(
        _attn_combine_kernel,
        grid=(m // tm,),
        in_specs=[spec] * 6,
        out_specs=spec,
        out_shape=jax.ShapeDtypeStruct((m, w), BF16),
        compiler_params=_cparams("parallel"),
        name="attn_combine",
    )(*parts)


def _attn_sample_kernel(q_ref, kn_ref, vn_ref, k0_ref, v0_ref, k1_ref, v1_ref, k2_ref, v2_ref, o_ref, *, n_new):
    tok = lax.broadcasted_iota(jnp.int32, (BACK, KV_HEADS, 1), 0)
    kc_refs = (k0_ref, k1_ref, k2_ref)
    vc_refs = (v0_ref, v1_ref, v2_ref)
    for s in range(n_new):
        outs, lses = [], []
        for g in range(N_GROUPS):
            qh = q_ref[0, s, g * KV_HEADS:(g + 1) * KV_HEADS, :][None]
            if g == 0:
                kc, vc = kc_refs[g][0], vc_refs[g][0]
                new = slice(0, s + 1)
            else:
                kc, vc = kc_refs[g][0, :, s], vc_refs[g][0, :, s]
                new = slice(s, s + 1)
            sc = jnp.sum(kc * qh, axis=-1, keepdims=True)
            if g == 0:
                sc = jnp.where(tok >= s, sc, NEG)
            sn = jnp.sum(kn_ref[0, new] * qh, axis=-1, keepdims=True)
            m = jnp.maximum(jnp.max(sc, axis=0, keepdims=True), jnp.max(sn, axis=0, keepdims=True))
            p = jnp.exp(sc - m)
            pn = jnp.exp(sn - m)
            den = jnp.sum(p, axis=0, keepdims=True) + jnp.sum(pn, axis=0, keepdims=True)
            o = (jnp.sum(p * vc, axis=0, keepdims=True)
                 + jnp.sum(pn * vn_ref[0, new], axis=0, keepdims=True)) / den
            outs.append(o)
            lses.append(m + jnp.log(den))
        mm = jnp.maximum(jnp.maximum(lses[0], lses[1]), lses[2])
        ws = [jnp.exp(l - mm) for l in lses]
        comb = (ws[0] * outs[0] + ws[1] * outs[1] + ws[2] * outs[2]) / (ws[0] + ws[1] + ws[2])
        o_ref[0, s] = comb[0].astype(o_ref.dtype)


def _attn_sample(q, k_new, v_new, cache_k, cache_v):
    db, n_new = q.shape[0], q.shape[1]
    w_buf = cache_k.shape[1]
    assert w_buf == BACK * DILATIONS[-1] and n_new <= DILATIONS[1]
    new_spec = pl.BlockSpec((1, n_new, KV_HEADS, HEAD_DIM), lambda b: (b, 0, 0, 0))
    specs = [pl.BlockSpec((1, n_new, N_GROUPS * KV_HEADS, HEAD_DIM), lambda b: (b, 0, 0, 0)), new_spec, new_spec]
    args = [q, k_new, v_new]
    for dil in DILATIONS:
        sub = w_buf // dil
        last = sub // BACK - 1
        if dil == 1:
            shape = (db, sub, KV_HEADS, HEAD_DIM)
            spec = pl.BlockSpec((1, BACK, KV_HEADS, HEAD_DIM), functools.partial(lambda b, l: (b, l, 0, 0), l=last))
        else:
            shape = (db, sub, dil, KV_HEADS, HEAD_DIM)
            spec = pl.BlockSpec((1, BACK, n_new, KV_HEADS, HEAD_DIM),
                                functools.partial(lambda b, l: (b, l, 0, 0, 0), l=last))
        for c in (cache_k, cache_v):
            specs.append(spec)
            args.append(c.reshape(shape))
    return pl.pallas_call(
        functools.partial(_attn_sample_kernel, n_new=n_new),
        grid=(db,),
        in_specs=specs,
        out_specs=new_spec,
        out_shape=jax.ShapeDtypeStruct((db, n_new, KV_HEADS, HEAD_DIM), F32),
        compiler_params=_cparams("parallel"),
        name="attn_sample",
    )(*args)


def _head_indicator(width):
    l = lax.broadcasted_iota(jnp.int32, (width, LANE), 0)
    h = lax.broadcasted_iota(jnp.int32, (width, LANE), 1)
    return (_div_pow2(l, RW_N) == h).astype(F32)


def _head_indicator_t(width):
    h = lax.broadcasted_iota(jnp.int32, (LANE, width), 0)
    l = lax.broadcasted_iota(jnp.int32, (LANE, width), 1)
    return (_div_pow2(l, RW_N) == h).astype(F32)


def _rw_prep_kernel(ur, uk, uv, ul, fpr, fpk, fpv, fpl, fsr, fsk, fsv, fsl, mur, muk, muv, mul, ww, wa, wg, w0, a0,
                    kk_, ka, r_o, lw_o, k_o, v_o, kk_o, b_o, g_o, *, n_prompt_tiles, ds):
    is_sample = pl.program_id(0) >= n_prompt_tiles

    def mixed(u_ref, fp_ref, fs_ref, mu_ref):
        u = u_ref[...]
        row = lax.broadcasted_iota(jnp.int32, u.shape, 0)
        rolled = pltpu.roll(u, 1, axis=0)
        p_prompt = jnp.where(row == 0, fp_ref[0], rolled)
        p_sample = jnp.where(_mod_pow2(row, ds) == 0, fs_ref[...], rolled)
        prev = jnp.where(is_sample, p_sample, p_prompt)
        return u + (prev - u) * mu_ref[...]

    xr = mixed(ur, fpr, fsr, mur)
    xk = mixed(uk, fpk, fsk, muk)
    xv = mixed(uv, fpv, fsv, muv)
    xl = mixed(ul, fpl, fsl, mul)
    col = lax.broadcasted_iota(jnp.int32, xl.shape, 1)
    act = jnp.where(col < LORA_W, jnp.tanh(xl),
                    jnp.where(col < LORA_W + LORA_A, xl,
                              jnp.where(col < LORA_ALL, _sigmoid(xl), 0.0)))
    zw = _dot(act, ww[...])
    za = _dot(act, wa[...])
    zg = _dot(act, wg[...])
    lw = -RW_DECAY_SCALE * _sigmoid(w0[...] + zw)
    a = _sigmoid(a0[...] + za)
    kk = xk * kk_[...]
    width = kk.shape[1]
    ss = _dot_sel(kk * kk, _head_indicator(width), 2)
    inv = 1.0 / jnp.maximum(jnp.sqrt(ss), 1e-12)
    kkn = kk * _dot_sel(inv, _head_indicator_t(width), 2)
    r_o[...] = xr.astype(r_o.dtype)
    lw_o[...] = lw
    k_o[...] = (xk * (1.0 + (a - 1.0) * ka[...])).astype(k_o.dtype)
    v_o[...] = xv.astype(v_o.dtype)
    kk_o[...] = kkn.astype(kk_o.dtype)
    b_o[...] = (kkn * a).astype(b_o.dtype)
    g_o[...] = zg.astype(g_o.dtype)


def _rw_prep(u_rkv, u_lora, shift, mp, sp, ds, mu, w_w, w_a, w_g, w0, a0, k_k, k_a, tm=512, tw=512):
    m = u_rkv.shape[0]
    nj = RW_W // tw
    n_p = mp // tm
    assert sp % tm == 0 and mp % tm == 0 and (m - mp) % tm == 0 and tm % ds == 0 and ds & (ds - 1) == 0
    mu_rkv = mu[:3 * RW_W].reshape(1, 3 * RW_W)
    lpad = LORA_PAD - LORA_ALL
    mu_l = jnp.pad(mu[3 * RW_W:], (0, lpad)).reshape(1, LORA_PAD)

    starts = jnp.arange(n_p, dtype=jnp.int32) * tm
    inside = ((starts % sp) != 0)[:, None]
    prev_idx = jnp.maximum(starts - 1, 0)

    def prompt_fix(u):
        return jnp.where(inside, jnp.take(u, prev_idx, axis=0), 0.0)[:, None, :]

    def sample_fix(first_rows):
        db, w = first_rows.shape
        return jnp.zeros((db, ds, w), F32).at[:, 0].set(first_rows).reshape(db * ds, w)

    fp_rkv, fp_l = prompt_fix(u_rkv), prompt_fix(u_lora)
    fs_rkv = sample_fix(shift[:, :3 * RW_W])
    fs_l = sample_fix(jnp.pad(shift[:, 3 * RW_W:], ((0, 0), (0, lpad))))

    def feat(off):
        return pl.BlockSpec((tm, tw), functools.partial(lambda i, j, o: (i, j + o), o=off))

    def fixp(off):
        return pl.BlockSpec((1, 1, tw), functools.partial(lambda i, j, o: (jnp.minimum(i, n_p - 1), 0, j + o), o=off))

    def fixs(off):
        return pl.BlockSpec((tm, tw), functools.partial(lambda i, j, o: (jnp.maximum(i - n_p, 0), j + o), o=off))

    def vec(off):
        return pl.BlockSpec((1, tw), functools.partial(lambda i, j, o: (0, j + o), o=off))

    lspec = pl.BlockSpec((tm, LORA_PAD), lambda i, j: (i, 0))
    lfixp = pl.BlockSpec((1, 1, LORA_PAD), lambda i, j: (jnp.minimum(i, n_p - 1), 0, 0))
    lfixs = pl.BlockSpec((tm, LORA_PAD), lambda i, j: (jnp.maximum(i - n_p, 0), 0))
    wspec = pl.BlockSpec((LORA_PAD, tw), lambda i, j: (0, j))
    in_specs = [feat(0), feat(nj), feat(2 * nj), lspec,
                fixp(0), fixp(nj), fixp(2 * nj), lfixp,
                fixs(0), fixs(nj), fixs(2 * nj), lfixs,
                vec(0), vec(nj), vec(2 * nj), pl.BlockSpec((1, LORA_PAD), lambda i, j: (0, 0)),
                wspec, wspec, wspec, vec(0), vec(0), vec(0), vec(0)]
    out_spec = pl.BlockSpec((tm, tw), lambda i, j: (i, j))
    outs = pl.pallas_call(
        functools.partial(_rw_prep_kernel, n_prompt_tiles=n_p, ds=ds),
        grid=(m // tm, nj),
        in_specs=in_specs,
        out_specs=[out_spec] * 7,
        out_shape=[jax.ShapeDtypeStruct((m, RW_W), F32 if i == 1 else BF16) for i in range(7)],
        compiler_params=_cparams("parallel", "arbitrary"),
        name="rw_prep",
    )(u_rkv, u_rkv, u_rkv, u_lora, fp_rkv, fp_rkv, fp_rkv, fp_l, fs_rkv, fs_rkv, fs_rkv, fs_l,
      mu_rkv, mu_rkv, mu_rkv, mu_l,
      w_w, w_a, w_g, w0.reshape(1, RW_W), a0.reshape(1, RW_W), k_k.reshape(1, RW_W), k_a.reshape(1, RW_W))
    return outs


def _rw_head_out(y, r, k, v, g, ln_w, ln_b, r_k):
    width = y.shape[1]
    ind, ind_t = _head_indicator(width).astype(BF16), _head_indicator_t(width).astype(BF16)
    mean = _dot_sel(_dot_sel(y, ind, 2) * (1.0 / RW_N), ind_t, 2)
    d = y - mean
    var_h = _dot_sel(d * d, ind, 2) * (1.0 / RW_N)
    rstd = _dot_sel(lax.rsqrt(var_h + RW_GN_EPS), ind_t, 2)
    yn = d * rstd * ln_w + ln_b
    bonus = _dot_sel(_dot_sel(r * k * r_k, ind, 1), ind_t, 2) * v
    return (yn + bonus) * g


def _rw_scan_kernel(r_ref, lw_ref, k_ref, v_ref, kk_ref, b_ref, g_ref, lnw_ref, lnb_ref, rk_ref, s_in,
                    o_ref, s_out, s_bd, *, chunk):
    t_id = pl.program_id(1)
    n_t = pl.num_programs(1)
    gw = RW_GROUP * RW_N
    n_groups = RW_W // gw
    rows = RW_GROUP * chunk

    ri = lax.broadcasted_iota(jnp.int32, (gw, gw), 0)
    ci = lax.broadcasted_iota(jnp.int32, (gw, gw), 1)
    state_mask = _div_pow2(ri, RW_N) == _div_pow2(ci, RW_N)

    @pl.when(t_id == 0)
    def _():
        kr = lax.broadcasted_iota(jnp.int32, (RW_N, gw), 0)
        kc = lax.broadcasted_iota(jnp.int32, (RW_N, gw), 1)
        spread = (_mod_pow2(kc, RW_N) == kr).astype(F32)
        for g in range(n_groups):
            tiled = _dot_sel(s_in[0, g * gw:(g + 1) * gw, :], spread, 3)
            s_bd[g] = jnp.where(state_mask, tiled, 0.0)

    ti = lax.broadcasted_iota(jnp.int32, (chunk, chunk), 0)
    tj = lax.broadcasted_iota(jnp.int32, (chunk, chunk), 1)
    lw = lw_ref[0]
    cum = _sel_dot((tj <= ti).astype(F32), lw, 3)
    e_pos = jnp.exp(cum)
    e_neg = jnp.exp(-cum)
    e_prev = jnp.exp(cum - lw)
    cum_last = cum[chunk - 1:chunk, :]
    e_rem = jnp.exp(cum_last - cum)
    c_all = jnp.exp(cum_last)
    r_in, k_in, b_in = r_ref[0].astype(F32), k_ref[0].astype(F32), b_ref[0].astype(F32)
    kt = kk_ref[0].astype(F32) * e_prev
    bt = b_in * e_neg
    k2t = k_in * e_neg
    rt = r_in * e_pos
    btc = b_in * e_rem
    k2tc = k_in * e_rem
    vv = v_ref[0].astype(F32)

    sr = lax.broadcasted_iota(jnp.int32, (rows, gw), 0)
    sc = lax.broadcasted_iota(jnp.int32, (rows, gw), 1)
    stack_mask = _div_pow2(sr, chunk) == _div_pow2(sc, RW_N)
    ar = lax.broadcasted_iota(jnp.int32, (rows, rows), 0)
    ac = lax.broadcasted_iota(jnp.int32, (rows, rows), 1)
    strict = ar > ac
    incl = ar >= ac
    eye = (ar == ac).astype(F32)

    def stack(x):
        return jnp.where(stack_mask, jnp.concatenate([x] * RW_GROUP, axis=0), 0.0).astype(BF16)

    gs = range(n_groups)
    lanes = [slice(g * gw, (g + 1) * gw) for g in gs]
    kr_s = [jnp.concatenate([stack(kt[:, ls]), stack(rt[:, ls])], axis=0) for ls in lanes]
    bt_s = [stack(bt[:, ls]) for ls in lanes]
    k2t_s = [stack(k2t[:, ls]) for ls in lanes]
    v_s = [stack(vv[:, ls]) for ls in lanes]
    s0 = [s_bd[g] for g in gs]
    p_b = [_dot_nt(kr_s[g], bt_s[g]) for g in gs]
    p_k = [_dot_nt(kr_s[g], k2t_s[g]) for g in gs]
    p_s = [_dot_nt(kr_s[g], s0[g]) for g in gs]
    a_b = [jnp.where(strict, p_b[g][:rows], 0.0) for g in gs]
    r_b = [jnp.where(incl, p_b[g][rows:], 0.0) for g in gs]
    ar_k = [jnp.concatenate([jnp.where(strict, p_k[g][:rows], 0.0), jnp.where(incl, p_k[g][rows:], 0.0)], axis=0)
            for g in gs]
    p_v = [_dot(ar_k[g], v_s[g]) for g in gs]
    rhs = [p_s[g][:rows] + p_v[g][:rows] for g in gs]
    inv = [eye - a_b[g] for g in gs]
    pw = a_b
    n = 2
    while n < chunk:
        pw = [_dot(pw[g], pw[g]) for g in gs]
        inv = [inv[g] + _dot(inv[g], pw[g]) for g in gs]
        n *= 2
    u = [-_dot(inv[g], rhs[g]) for g in gs]
    y_bd = [p_s[g][rows:] + p_v[g][rows:] + _dot(r_b[g], u[g]) for g in gs]
    ys = []
    for g in gs:
        y = y_bd[g][0:chunk]
        for h in range(1, RW_GROUP):
            y = y + y_bd[g][h * chunk:(h + 1) * chunk]
        ys.append(y)
    o_ref[0] = _rw_head_out(jnp.concatenate(ys, axis=1), r_in, k_in, vv, g_ref[0].astype(F32),
                            lnw_ref[...], lnb_ref[...], rk_ref[...]).astype(o_ref.dtype)
    for g in gs:
        uv = jnp.concatenate([u[g].astype(BF16), v_s[g]], axis=0)
        bk = jnp.concatenate([stack(btc[:, lanes[g]]), stack(k2tc[:, lanes[g]])], axis=0)
        s_bd[g] = s0[g] * c_all[:, lanes[g]] + _dot_tn(uv, bk)

    @pl.when(t_id == n_t - 1)
    def _():
        gr = lax.broadcasted_iota(jnp.int32, (gw, RW_N), 0)
        gc = lax.broadcasted_iota(jnp.int32, (gw, RW_N), 1)
        gather = (_mod_pow2(gr, RW_N) == gc).astype(F32)
        for g in range(n_groups):
            s_out[0, g * gw:(g + 1) * gw, :] = _dot_sel(s_bd[g], gather, 3)


def _rw_scan(r, lw, k, v, kk, b, g, ln_w, ln_b, r_k, state, chunk, bn, t):
    gw = RW_GROUP * RW_N
    nt = t // chunk
    seq = pl.BlockSpec((1, chunk, RW_W), lambda bi, ti: (0, bi * nt + ti, 0))
    vec = pl.BlockSpec((1, RW_W), lambda bi, ti: (0, 0))
    st = pl.BlockSpec((1, RW_W, RW_N), lambda bi, ti: (bi, 0, 0))
    o, s_new = pl.pallas_call(
        functools.partial(_rw_scan_kernel, chunk=chunk),
        grid=(bn, nt),
        in_specs=[seq] * 7 + [vec] * 3 + [st],
        out_specs=[seq, st],
        out_shape=[jax.ShapeDtypeStruct((1, bn * t, RW_W), BF16), jax.ShapeDtypeStruct((bn, RW_W, RW_N), F32)],
        scratch_shapes=[pltpu.VMEM((RW_W // gw, gw, gw), F32)],
        compiler_params=_cparams("parallel", "arbitrary"),
        name=f"rw_scan_c{chunk}",
    )(*(a[None] for a in (r, lw, k, v, kk, b, g)), ln_w.reshape(1, RW_W), ln_b.reshape(1, RW_W),
      r_k.reshape(1, RW_W), state.reshape(bn, RW_W, RW_N))
    return o[0], s_new.reshape(bn, RW_HEADS, RW_N, RW_N)


def _rw_step_kernel(r_ref, lw_ref, k_ref, v_ref, kk_ref, b_ref, s_in, y_ref, s_out, *, n_steps):
    sub = 8
    for t in range(n_steps):
        w = jnp.exp(lw_ref[t])
        kap, bb, k2, r = kk_ref[t], b_ref[t], k_ref[t], r_ref[t]
        src = s_in if t == 0 else s_out

        def body(blk, c, t=t, w=w, kap=kap, bb=bb, k2=k2, r=r, src=src):
            base = pl.multiple_of(blk * sub, sub)
            v_rows = v_ref[t, pl.ds(base, sub), :]
            y_rows = []
            for j in range(sub):
                sv = src[0, base + j]
                sa = -jnp.sum(sv * kap, axis=0, keepdims=True)
                sn = sv * w + sa * bb + v_rows[j:j + 1] * k2
                s_out[0, base + j] = sn
                y_rows.append(jnp.sum(sn * r, axis=0, keepdims=True))
            y_ref[t, pl.ds(base, sub), :] = jnp.concatenate(y_rows, axis=0)
            return c

        lax.fori_loop(0, RW_N // sub, body, 0)


def _rw_steps(r, lw, k, v, kk, b, state):
    n_steps, _, nb = r.shape
    assert nb % LANE == 0
    seq = pl.BlockSpec((n_steps, RW_N, nb), lambda h: (0, h, 0))
    st = pl.BlockSpec((1, RW_N, RW_N, nb), lambda h: (h, 0, 0, 0))
    return pl.pallas_call(
        functools.partial(_rw_step_kernel, n_steps=n_steps),
        grid=(RW_HEADS,),
        in_specs=[seq] * 6 + [st],
        out_specs=[seq, st],
        out_shape=[jax.ShapeDtypeStruct((n_steps, RW_W, nb), F32),
                   jax.ShapeDtypeStruct((RW_HEADS, RW_N, RW_N, nb), F32)],
        compiler_params=_cparams("parallel"),
        name="rw_steps",
    )(r, lw, k, v, kk, b, state)


def _rw_post_kernel(y_ref, r_ref, k_ref, v_ref, g_ref, lnw, lnb, rk, o_ref):
    f32 = lambda ref: ref[...].astype(F32)
    o_ref[...] = _rw_head_out(y_ref[...], f32(r_ref), f32(k_ref), f32(v_ref), f32(g_ref),
                              lnw[...], lnb[...], rk[...]).astype(o_ref.dtype)


def _rw_post(y, r, k, v, g, ln_w, ln_b, r_k, row_off, tm=512, tw=512):
    m = y.shape[0]
    off = row_off // tm
    own = pl.BlockSpec((tm, tw), lambda i, j: (i, j))
    feat = pl.BlockSpec((tm, tw), lambda i, j: (i + off, j))
    vec = pl.BlockSpec((1, tw), lambda i, j: (0, j))
    return pl.pallas_call(
        _rw_post_kernel,
        grid=(m // tm, RW_W // tw),
        in_specs=[own] + [feat] * 4 + [vec] * 3,
        out_specs=own,
        out_shape=jax.ShapeDtypeStruct((m, RW_W), BF16),
        compiler_params=_cparams("parallel", "parallel"),
        name="rw_post",
    )(y, r, k, v, g, ln_w.reshape(1, RW_W), ln_b.reshape(1, RW_W), r_k.reshape(1, RW_W))


def _merge_kernel(oa_ref, ob_ref, wa_ref, wb_ref, ga_ref, gb_ref, out_ref, wa_bf, wb_bf):
    @pl.when(pl.program_id(1) == 0)
    def _():
        wa_bf[...] = wa_ref[...].astype(BF16)
        wb_bf[...] = wb_ref[...].astype(BF16)

    ya = jnp.dot(oa_ref[...], wa_bf[...], preferred_element_type=F32)
    yb = jnp.dot(ob_ref[...], wb_bf[...], preferred_element_type=F32)
    out_ref[...] = (ga_ref[...].astype(F32) * ya + gb_ref[...].astype(F32) * yb).astype(out_ref.dtype)


def _merge(o_a, o_b, w_a, w_b, gates, tm):
    m = o_a.shape[0]
    tn = MM_TN
    nj = D_MODEL // tn
    return pl.pallas_call(
        _merge_kernel,
        grid=(nj, m // tm),
        in_specs=[pl.BlockSpec((tm, KV_W), lambda j, i: (i, 0)),
                  pl.BlockSpec((tm, RW_W), lambda j, i: (i, 0)),
                  pl.BlockSpec((KV_W, tn), lambda j, i: (0, j)),
                  pl.BlockSpec((RW_W, tn), lambda j, i: (0, j)),
                  pl.BlockSpec((tm, tn), lambda j, i: (i, j)),
                  pl.BlockSpec((tm, tn), lambda j, i: (i, j + nj))],
        out_specs=pl.BlockSpec((tm, tn), lambda j, i: (i, j)),
        out_shape=jax.ShapeDtypeStruct((m, D_MODEL), BF16),
        scratch_shapes=[pltpu.VMEM((KV_W, tn), BF16), pltpu.VMEM((RW_W, tn), BF16)],
        compiler_params=_cparams("parallel", "arbitrary"),
        name="merge",
    )(o_a, o_b, w_a, w_b, gates, gates)


def _router_kernel(x_ref, g_ref, wr_ref, br_ref, h_ref, id_ref, wt_ref):
    x = x_ref[...]
    ms = jnp.mean(x * x, axis=-1, keepdims=True)
    h = x * lax.rsqrt(ms + NORM_EPS) * g_ref[...]
    h_ref[...] = h.astype(h_ref.dtype)
    logits = _dot_hi(h, wr_ref[...]) + br_ref[...]
    lane = lax.broadcasted_iota(jnp.int32, logits.shape, 1)
    lane_f = lane.astype(F32)
    gmask = lane < MOE_GROUPS
    gl = jnp.where(gmask, logits, NEG)
    gm = jnp.max(gl, axis=-1, keepdims=True)
    gi = jnp.min(jnp.where(gl == gm, lane_f, float(LANE)), axis=-1, keepdims=True)
    g_prob = 1.0 / jnp.sum(jnp.where(gmask, jnp.exp(gl - gm), 0.0), axis=-1, keepdims=True)
    lo = MOE_GROUPS + gi * MOE_PER_GROUP
    emask = jnp.logical_and(lane_f >= lo, lane_f < lo + MOE_PER_GROUP)
    el = jnp.where(emask, logits, NEG)
    m1 = jnp.max(el, axis=-1, keepdims=True)
    i1 = jnp.min(jnp.where(el == m1, lane_f, float(LANE)), axis=-1, keepdims=True)
    el2 = jnp.where(lane_f == i1, NEG, el)
    m2 = jnp.max(el2, axis=-1, keepdims=True)
    i2 = jnp.min(jnp.where(el2 == m2, lane_f, float(LANE)), axis=-1, keepdims=True)
    t = jnp.exp(m2 - m1)
    w1 = g_prob / (1.0 + t)
    w2 = g_prob * t / (1.0 + t)
    ids = jnp.where(lane == 0, i1 - MOE_GROUPS, jnp.where(lane == 1, i2 - MOE_GROUPS, 0.0))
    id_ref[...] = ids.astype(jnp.int32)
    wt_ref[...] = jnp.where(lane == 0, w1, jnp.where(lane == 1, w2, 0.0))


def _router(x, g, w_router, b_router, tm=256):
    m, d = x.shape
    row = pl.BlockSpec((tm, d), lambda i: (i, 0))
    small = pl.BlockSpec((tm, LANE), lambda i: (i, 0))
    return pl.pallas_call(
        _router_kernel,
        grid=(m // tm,),
        in_specs=[row, pl.BlockSpec((1, d), lambda i: (0, 0)),
                  pl.BlockSpec((d, LANE), lambda i: (0, 0)), pl.BlockSpec((1, LANE), lambda i: (0, 0))],
        out_specs=[row, small, small],
        out_shape=[jax.ShapeDtypeStruct((m, d), F32), jax.ShapeDtypeStruct((m, LANE), jnp.int32),
                   jax.ShapeDtypeStruct((m, LANE), F32)],
        compiler_params=_cparams("parallel"),
        name="ffn_norm_router",
    )(x, g.reshape(1, d), w_router, b_router)


def _moe_kernel(te_ref, tv_ref, last_ref, x_ref, wg_ref, wu_ref, wd_ref, o_ref):
    i = pl.program_id(0)
    j = pl.program_id(1)
    valid = tv_ref[i] > 0

    @pl.when(j == 0)
    def _():
        o_ref[...] = jnp.zeros_like(o_ref)

    @pl.when(valid)
    def _():
        x = x_ref[...]
        gate = jnp.dot(x, wg_ref[0].astype(BF16), preferred_element_type=F32)
        up = jnp.dot(x, wu_ref[0].astype(BF16), preferred_element_type=F32)
        hidden = (gate * _sigmoid(gate) * up).astype(BF16)
        o_ref[...] += jnp.dot(hidden, wd_ref[0].astype(BF16), preferred_element_type=F32)


def _moe_experts(xs, tile_expert, tile_valid, last_tile, w_gate, w_up, w_down):
    p, d = xs.shape
    n_tiles = p // MOE_TM
    nf = EXPERT_FF // MOE_TF

    def f_idx(i, j, tv):
        return jnp.where(tv[i] > 0, j, nf - 1)

    def rows(i, j, te, tv, last):
        return (jnp.minimum(i, last[0]), 0)

    grid_spec = pltpu.PrefetchScalarGridSpec(
        num_scalar_prefetch=3,
        grid=(n_tiles, nf),
        in_specs=[pl.BlockSpec((MOE_TM, d), rows),
                  pl.BlockSpec((1, d, MOE_TF), lambda i, j, te, tv, last: (te[i], 0, f_idx(i, j, tv))),
                  pl.BlockSpec((1, d, MOE_TF), lambda i, j, te, tv, last: (te[i], 0, f_idx(i, j, tv))),
                  pl.BlockSpec((1, MOE_TF, d), lambda i, j, te, tv, last: (te[i], f_idx(i, j, tv), 0))],
        out_specs=pl.BlockSpec((MOE_TM, d), lambda i, j, te, tv, last: (i, 0)),
    )
    return pl.pallas_call(
        _moe_kernel,
        grid_spec=grid_spec,
        out_shape=jax.ShapeDtypeStruct((p, d), F32),
        compiler_params=_cparams("arbitrary", "arbitrary"),
        name="moe_experts",
    )(tile_expert, tile_valid, last_tile, xs, w_gate, w_up, w_down)


def _row_copy(src_hbm, row, dst_vmem, slot, sem):
    return pltpu.make_async_copy(src_hbm.at[pl.ds(row, 1)], dst_vmem.at[pl.ds(slot, 1)], sem)


WAIT_ROWS = 8


def _wait_rows(src_hbm, dst_vmem, sem, n):
    def wait_block(r, c):
        pltpu.make_async_copy(src_hbm.at[pl.ds(0, WAIT_ROWS)], dst_vmem.at[pl.ds(0, WAIT_ROWS)], sem).wait()
        return c

    def wait_row(r, c):
        _row_copy(src_hbm, 0, dst_vmem, 0, sem).wait()
        return c

    lax.fori_loop(0, n // WAIT_ROWS, wait_block, 0)
    lax.fori_loop(0, n % WAIT_ROWS, wait_row, 0)


def _dispatch_kernel(dest_ref, nv_ref, h_hbm, o_ref, buf, slot_tok, sem, *, n_assign):
    i = pl.program_id(0)
    n = nv_ref[i]

    @pl.when(i == 0)
    def _():
        buf[...] = jnp.zeros_like(buf)

        def invert(a, c):
            slot_tok[dest_ref[a]] = a >> 1
            return c

        lax.fori_loop(0, n_assign, invert, 0, unroll=8)

    def issue(r, c):
        _row_copy(h_hbm, slot_tok[i * MOE_TM + r], buf, r, sem).start()
        return c

    lax.fori_loop(0, n, issue, 0)
    _wait_rows(h_hbm, buf, sem, n)
    o_ref[...] = buf[...].astype(o_ref.dtype)


def _dispatch_rows(h, dest, tile_rows):
    d = h.shape[1]
    n_tiles = tile_rows.shape[0]
    p = n_tiles * MOE_TM
    grid_spec = pltpu.PrefetchScalarGridSpec(
        num_scalar_prefetch=2,
        grid=(n_tiles,),
        in_specs=[pl.BlockSpec(memory_space=pl.ANY)],
        out_specs=pl.BlockSpec((MOE_TM, d), lambda i, dst, nv: (i, 0)),
        scratch_shapes=[pltpu.VMEM((MOE_TM, d), F32), pltpu.SMEM((p,), jnp.int32), pltpu.SemaphoreType.DMA(())],
    )
    return pl.pallas_call(
        functools.partial(_dispatch_kernel, n_assign=dest.shape[0]),
        grid_spec=grid_spec,
        out_shape=jax.ShapeDtypeStruct((p, d), BF16),
        compiler_params=_cparams("arbitrary"),
        name="moe_dispatch",
    )(dest, tile_rows, h)


def _dispatch_plan(ids):
    t = ids.shape[0]
    a = t * 2
    flat_e = ids.reshape(a)
    experts = jnp.arange(N_EXPERTS, dtype=jnp.int32)
    onehot = (flat_e[:, None] == experts[None, :]).astype(jnp.int32)
    csum = jnp.cumsum(onehot, axis=0)
    counts = csum[-1]
    rank = jnp.sum(onehot * csum, axis=1) - 1
    tiles_per = (counts + MOE_TM - 1) // MOE_TM
    tile_end = jnp.cumsum(tiles_per)
    tile_start = tile_end - tiles_per
    dest = jnp.sum(onehot * tile_start[None, :], axis=1) * MOE_TM + rank
    n_tiles = -(-a // MOE_TM) + N_EXPERTS
    total = tile_end[-1]
    tile_ids = jnp.arange(n_tiles, dtype=jnp.int32)
    tile_valid = (tile_ids < total).astype(jnp.int32)
    tile_expert = jnp.sum((tile_end[None, :] <= jnp.minimum(tile_ids, total - 1)[:, None]).astype(jnp.int32), axis=1)
    tile_expert = jnp.minimum(tile_expert, N_EXPERTS - 1)
    of_tile = (tile_expert[:, None] == experts[None, :]).astype(jnp.int32)
    tile_pos = tile_ids - jnp.sum(of_tile * tile_start[None, :], axis=1)
    tile_count = jnp.sum(of_tile * counts[None, :], axis=1)
    tile_rows = jnp.where(tile_valid > 0, jnp.clip(tile_count - tile_pos * MOE_TM, 0, MOE_TM), 0)
    last_tile = (total - 1).astype(jnp.int32).reshape(1)
    return dest.astype(jnp.int32), tile_expert, tile_valid, tile_rows.astype(jnp.int32), last_tile


def _final_kernel(dest_ref, x_ref, w_ref, y_hbm, g_ref, o_ref, buf, sem, *, row_off, tm):
    base = (row_off + pl.program_id(0) * tm) * 2

    def issue(r, c):
        _row_copy(y_hbm, dest_ref[base + r], buf, (r & 1) * tm + (r >> 1), sem).start()
        return c

    lax.fori_loop(0, 2 * tm, issue, 0, unroll=8)
    _wait_rows(y_hbm, buf, sem, 2 * tm)
    w = w_ref[...]
    x = x_ref[...] + (buf[0:tm, :] * w[:, 0:1] + buf[tm:2 * tm, :] * w[:, 1:2])
    ms = jnp.mean(x * x, axis=-1, keepdims=True)
    o_ref[...] = x * lax.rsqrt(ms + NORM_EPS) * g_ref[...]


def _final(x, wts, y_rows, dest, g, row_off, rows, tm=128):
    d = x.shape[1]
    off = row_off // tm
    grid_spec = pltpu.PrefetchScalarGridSpec(
        num_scalar_prefetch=1,
        grid=(rows // tm,),
        in_specs=[pl.BlockSpec((tm, d), lambda i, dst: (i + off, 0)),
                  pl.BlockSpec((tm, LANE), lambda i, dst: (i + off, 0)),
                  pl.BlockSpec(memory_space=pl.ANY),
                  pl.BlockSpec((1, d), lambda i, dst: (0, 0))],
        out_specs=pl.BlockSpec((tm, d), lambda i, dst: (i, 0)),
        scratch_shapes=[pltpu.VMEM((2 * tm, d), F32), pltpu.SemaphoreType.DMA(())],
    )
    return pl.pallas_call(
        functools.partial(_final_kernel, row_off=row_off, tm=tm),
        grid_spec=grid_spec,
        out_shape=jax.ShapeDtypeStruct((rows, d), F32),
        compiler_params=_cparams("arbitrary"),
        name="final_norm",
    )(dest, x, wts, y_rows, g.reshape(1, d))


def _rope_tables(pos):
    half = HEAD_DIM // 2
    inv_freq = ROPE_THETA ** (-jnp.arange(half, dtype=F32) / half)
    ang = pos.astype(F32)[:, None] * inv_freq[None, :]
    cos, sin = jnp.cos(ang), jnp.sin(ang)
    return jnp.concatenate([cos, cos], axis=-1), jnp.concatenate([-sin, sin], axis=-1)


def kernel(x_prompt, x_sample, cache_k, cache_v, state_shift, state_wkv, norm_mix_g, w_in, rw_mu, rw_w0, rw_w2,
           rw_a0, rw_a2, rw_g2, rw_k_k, rw_k_a, rw_r_k, rw_ln_w, rw_ln_b, w_branch_a, w_branch_b, w_out,
           norm_ffn_g, router_group_w, router_group_b, router_expert_w, router_expert_b, exp_gate, exp_up,
           exp_down, norm_final_g):
    assert w_in.shape[0] == 1, "single-layer trunk"
    bp, sp, d = x_prompt.shape
    db, ds, _ = x_sample.shape
    mp, ms_ = bp * sp, db * ds
    m = mp + ms_
    past = cache_k.shape[2]
    tm_mm = m // 8

    x_all = jnp.concatenate([x_prompt.reshape(mp, d), x_sample.reshape(ms_, d)], axis=0)
    h = _rmsnorm(x_all, norm_mix_g[0], BF16)

    w_in_t = jnp.swapaxes(w_in, 1, 2).reshape(w_in.shape[2], d)
    cos_p, sin_p = _rope_tables(jnp.arange(sp, dtype=jnp.int32))
    cos_s, sin_s = _rope_tables(past + jnp.arange(ds, dtype=jnp.int32))
    cos = jnp.concatenate([jnp.tile(cos_p, (bp, 1)), jnp.tile(cos_s, (db, 1))], axis=0)
    sin = jnp.concatenate([jnp.tile(sin_p, (bp, 1)), jnp.tile(sin_s, (db, 1))], axis=0)
    rope_extras = [(cos, "row", 0), (sin, "row", 0)]
    def inproj(w, n_cols, col, epilogue, extras, dtype, name, row_off=None):
        return _matmul(h, w, n_cols, col // MM_TN, epilogue, extras, [dtype], tm_mm, name, w_transposed=True,
                       w_row_off=row_off)[0]

    q = inproj(w_in_t, Q_W, 0, _ep_rope_q, rope_extras, BF16, "inproj_q")
    k = inproj(w_in_t, KV_W, COL_K, _ep_rope_k, rope_extras, F32, "inproj_k")
    v = inproj(w_in_t, KV_W, COL_V, _ep_plain, [], F32, "inproj_v")
    u_rkv = inproj(w_in_t, 3 * RW_W, COL_RW, _ep_plain, [], F32, "inproj_rkv")
    u_lora = inproj(w_in_t, LORA_PAD, COL_LORA, _ep_plain, [], F32, "inproj_lora")
    gates = inproj(w_in_t, 2 * d, 0, _ep_sigmoid, [], BF16, "inproj_gates", row_off=COL_GATE)

    qp = q[:mp].reshape(bp, sp, Q_W)
    kp = k[:mp].reshape(bp, sp, KV_W)
    vp = v[:mp].reshape(bp, sp, KV_W)
    parts = []
    for g, dil in enumerate(DILATIONS):
        o_g, l_g = (_attn_prompt_group if dil == 1 else _attn_dilated_group)(qp, kp, vp, g, dil)
        parts += [o_g.reshape(mp, KV_W), l_g.reshape(mp, KV_W)]
    oa_p = _attn_combine(parts)
    ks = k[mp:].reshape(db, ds, KV_HEADS, HEAD_DIM)
    vs = v[mp:].reshape(db, ds, KV_HEADS, HEAD_DIM)
    oa_s = _attn_sample(q[mp:].astype(F32).reshape(db, ds, N_GROUPS * KV_HEADS, HEAD_DIM), ks, vs,
                        cache_k.reshape(db, past, KV_HEADS, HEAD_DIM), cache_v.reshape(db, past, KV_HEADS, HEAD_DIM))
    o_a = jnp.concatenate([oa_p, oa_s.reshape(ms_, KV_W).astype(BF16)], axis=0)

    zl = functools.partial(jnp.zeros, dtype=F32)
    w_w = zl((LORA_PAD, RW_W)).at[:LORA_W].set(rw_w2[0])
    w_a = zl((LORA_PAD, RW_W)).at[LORA_W:LORA_W + LORA_A].set(rw_a2[0])
    w_g = zl((LORA_PAD, RW_W)).at[LORA_W + LORA_A:LORA_ALL].set(rw_g2[0])
    r_, lw_, k2_, v_, kk_, b_, g_ = _rw_prep(u_rkv, u_lora, state_shift[0], mp, sp, ds, rw_mu[0], w_w, w_a, w_g,
                                             rw_w0[0], rw_a0[0], rw_k_k[0], rw_k_a[0])

    scan_in = (r_, lw_, k2_, v_, kk_, b_)
    head_params = (rw_ln_w[0], rw_ln_b[0], rw_r_k[0])
    ob_p, wkv_p = _rw_scan(*scan_in, g_, *head_params, jnp.zeros((bp, RW_HEADS, RW_N, RW_N), F32),
                           RW_CHUNK, bp, sp)
    samp = [a[mp:].astype(F32).reshape(db, ds, RW_W).transpose(1, 2, 0) for a in scan_in]
    y_s, wkv_s = _rw_steps(*samp, jnp.transpose(state_wkv[0], (1, 2, 3, 0)))
    wkv_s = jnp.transpose(wkv_s, (3, 0, 1, 2))
    ob_s = _rw_post(y_s.transpose(2, 0, 1).reshape(ms_, RW_W), r_, k2_, v_, g_, *head_params, mp)
    o_b = jnp.concatenate([ob_p, ob_s], axis=0)

    merged = _merge(o_a, o_b, w_branch_a[0], w_branch_b[0], gates, tm_mm)
    (x1,) = _matmul(merged, w_out.reshape(d, d), d, 0, _ep_residual, [(x_all, "tile", 0)], [F32], tm_mm, "out_proj")

    w_router = jnp.concatenate([router_group_w[0], router_expert_w[0],
                                jnp.zeros((d, LANE - MOE_GROUPS - N_EXPERTS), F32)], axis=1)
    b_router = jnp.concatenate([router_group_b[0], router_expert_b[0],
                                jnp.zeros((LANE - MOE_GROUPS - N_EXPERTS,), F32)]).reshape(1, LANE)
    h2, ids, wts = _router(x1, norm_ffn_g[0], w_router, b_router)
    dest, tile_expert, tile_valid, tile_rows, last_tile = _dispatch_plan(ids[:, :2])
    xs = _dispatch_rows(h2, dest, tile_rows)
    yb = _moe_experts(xs, tile_expert, tile_valid, last_tile, exp_gate[0], exp_up[0], exp_down[0])
    y_prompt = _final(x1, wts, yb, dest, norm_final_g, 0, mp).reshape(bp, sp, d)
    y_sample = _final(x1, wts, yb, dest, norm_final_g, mp, ms_).reshape(db, ds, d)

    keep = min(BACK * DILATIONS[-1], sp)
    k_prompt = k[:mp].reshape(1, bp, sp, KV_HEADS, HEAD_DIM)[:, :, sp - keep:]
    v_prompt = v[:mp].reshape(1, bp, sp, KV_HEADS, HEAD_DIM)[:, :, sp - keep:]
    k_sample = ks.reshape(1, db, ds, KV_HEADS, HEAD_DIM)
    v_sample = vs.reshape(1, db, ds, KV_HEADS, HEAD_DIM)

    def last_rows(lo, nb, t):
        rows = lo + t - 1 + t * jnp.arange(nb, dtype=jnp.int32)
        a = jnp.take(u_rkv, rows, axis=0)
        b = jnp.take(u_lora, rows, axis=0)[:, :LORA_ALL]
        return jnp.concatenate([a, b], axis=-1)[None]

    return (y_prompt, y_sample, k_prompt, v_prompt, k_sample, v_sample,
            last_rows(0, bp, sp), last_rows(mp, db, ds), wkv_p[None], wkv_s[None])
```
